```python
import math
import jax
import jax.numpy as jnp
from jax import lax
import numpy as np

D_MODEL = 1024
BATCH = 2
SEQ = 8192
DEPTH = 4

N_MIXERS = 3
Q_BLK = 128
EPS = 1e-6
NEG = -1e30
FORCE = 1e4
NSA_HEADS = 16
NSA_DK = 64
NSA_KV = 4
NSA_REP = NSA_HEADS // NSA_KV
CMP_LEN = 32
CMP_STRIDE = 16
CMP_HID = 256
SEL_LEN = 64
SEL_TOPK = 16
WIN = 512
NSA_QW = NSA_HEADS * NSA_DK
NSA_KVW = NSA_KV * NSA_DK
NSA_IN = NSA_QW + 6 * NSA_KVW + 3 * NSA_HEADS
SB_HEADS = 16
SB_DH = D_MODEL // SB_HEADS
DIFF_HEADS = 8
DIFF_DH = D_MODEL // (2 * DIFF_HEADS)
D_FF = 2816
N_A = (DEPTH + 2) // 3
N_B = (DEPTH + 1) // 3
N_C = DEPTH // 3

kernel_name = "hybrid_nsa_stickbreak_diffattn_macaron"


def rms_norm(x, g):
    xf = x.astype(jnp.float32)
    y = xf * lax.rsqrt(jnp.mean(xf * xf, axis=-1, keepdims=True) + EPS)
    return (y * g.astype(jnp.float32)).astype(x.dtype)


def alibi_slopes(n):
    return jnp.exp2(-8.0 * jnp.arange(1, n + 1, dtype=jnp.float32) / n)


def swiglu(h, w1, w2):
    gate, up = jnp.split(h @ w1, 2, axis=-1)
    return (jax.nn.silu(gate) * up) @ w2


def sweep_blocks(fn, seq):
    out = lax.map(fn, jnp.arange(seq // Q_BLK, dtype=jnp.int32) * Q_BLK)
    out = jnp.moveaxis(out, 0, 1)
    return out.reshape(out.shape[0], seq, *out.shape[3:])


def nsa_mixer(h, w_in, cmp_pe, cmp_w1, cmp_w2, w_out):
    B, T, _ = h.shape
    f32 = jnp.float32
    bounds = [int(b) for b in np.cumsum([NSA_QW] + [NSA_KVW] * 6)]
    parts = jnp.split(h @ w_in, bounds, axis=-1)
    q = parts[0].reshape(B, T, NSA_KV, NSA_REP, NSA_DK)
    kc, vc, ks, vs, kw, vw = [p.reshape(B, T, NSA_KV, NSA_DK) for p in parts[1:7]]
    gates = jax.nn.sigmoid(parts[7].astype(f32)).reshape(B, T, NSA_KV, NSA_REP, 3)
    scale = NSA_DK ** -0.5
    slopes = alibi_slopes(NSA_HEADS).reshape(NSA_KV, NSA_REP)

    n_cmp = (T - CMP_LEN) // CMP_STRIDE + 1
    cmp_start = jnp.arange(n_cmp) * CMP_STRIDE
    tok_idx = cmp_start[:, None] + jnp.arange(CMP_LEN)[None, :]

    def compress(a, j):
        blk = a[:, tok_idx] + cmp_pe[j][None, None, :, None, :]
        blk = jnp.moveaxis(blk, 3, 2).reshape(B, n_cmp, NSA_KV, CMP_LEN * NSA_DK)
        return jax.nn.gelu(blk @ cmp_w1[j]) @ cmp_w2[j]

    k_cmp = compress(kc, 0)
    v_cmp = compress(vc, 1)
    cmp_end = cmp_start + CMP_LEN - 1

    n_sel = T // SEL_LEN
    topk = min(SEL_TOPK, n_sel)
    sel_start = jnp.arange(n_sel) * SEL_LEN
    overlap = ((cmp_start[:, None] < sel_start[None, :] + SEL_LEN)
               & (cmp_start[:, None] + CMP_LEN > sel_start[None, :])).astype(f32)
    ks_blk = jnp.moveaxis(ks.reshape(B, n_sel, SEL_LEN, NSA_KV, NSA_DK), 3, 1)
    vs_blk = jnp.moveaxis(vs.reshape(B, n_sel, SEL_LEN, NSA_KV, NSA_DK), 3, 1)
    b_i = jnp.arange(B)[:, None, None, None]
    g_i = jnp.arange(NSA_KV)[None, :, None, None]
    sel_ids = jnp.arange(n_sel)

    kw_pad = jnp.pad(kw, ((0, 0), (WIN, 0), (0, 0), (0, 0)))
    vw_pad = jnp.pad(vw, ((0, 0), (WIN, 0), (0, 0), (0, 0)))

    def block(q0):
        t = q0 + jnp.arange(Q_BLK)
        qb = lax.dynamic_slice_in_dim(q, q0, Q_BLK, axis=1)
        s = jnp.einsum('bqgrd,bngd->bgrqn', qb, k_cmp).astype(f32) * scale
        dist = (t[:, None] - cmp_end[None, :]).astype(f32)
        valid = dist >= 0
        s = jnp.where(valid, s - slopes[None, :, :, None, None] * dist, NEG)
        p_cmp = jax.nn.softmax(s, axis=-1) * valid
        o_cmp = jnp.einsum('bgrqn,bngd->bqgrd', p_cmp.astype(v_cmp.dtype), v_cmp)
        imp = jnp.einsum('bgrqn,ns->bgqs', p_cmp, overlap)
        cur = t // SEL_LEN
        forced = ((sel_ids[None, :] == 0) | (sel_ids[None, :] == cur[:, None])
                  | (sel_ids[None, :] == cur[:, None] - 1))
        imp = jnp.where(forced, imp + FORCE, imp)
        imp = jnp.where(sel_ids[None, :] <= cur[:, None], imp, NEG)
        _, idx = lax.top_k(imp, topk)
        k_g = ks_blk[b_i, g_i, idx]
        v_g = vs_blk[b_i, g_i, idx]
        pos = idx[..., None] * SEL_LEN + jnp.arange(SEL_LEN)
        dist = (t[None, None, :, None, None] - pos).astype(f32)[:, :, None]
        s = jnp.einsum('bqgrd,bgqkld->bgrqkl', qb, k_g).astype(f32) * scale
        s = jnp.where(dist >= 0, s - slopes[None, :, :, None, None, None] * dist, NEG)
        p_sel = jax.nn.softmax(s, axis=(-2, -1))
        o_sel = jnp.einsum('bgrqkl,bgqkld->bqgrd', p_sel.astype(v_g.dtype), v_g)
        kwb = lax.dynamic_slice_in_dim(kw_pad, q0, WIN + Q_BLK, axis=1)
        vwb = lax.dynamic_slice_in_dim(vw_pad, q0, WIN + Q_BLK, axis=1)
        src = q0 - WIN + jnp.arange(WIN + Q_BLK)
        d_int = t[:, None] - src[None, :]
        valid = (d_int >= 0) & (d_int < WIN) & (src[None, :] >= 0)
        s = jnp.einsum('bqgrd,bsgd->bgrqs', qb, kwb).astype(f32) * scale
        s = jnp.where(valid, s - slopes[None, :, :, None, None] * d_int.astype(f32), NEG)
        p_win = jax.nn.softmax(s, axis=-1)
        o_win = jnp.einsum('bgrqs,bsgd->bqgrd', p_win.astype(vwb.dtype), vwb)
        g = lax.dynamic_slice_in_dim(gates, q0, Q_BLK, axis=1)
        o = g[..., 0:1] * o_cmp + g[..., 1:2] * o_sel + g[..., 2:3] * o_win
        return o.reshape(B, Q_BLK, NSA_QW)

    return sweep_blocks(block, T) @ w_out


def stick_breaking_mixer(h, w_in, w_out):
    B, T, _ = h.shape
    f32 = jnp.float32
    q, k, v = [a.reshape(B, T, SB_HEADS, SB_DH) for a in jnp.split(h @ w_in, 3, axis=-1)]
    scale = SB_DH ** -0.5
    src = jnp.arange(T)

    def block(q0):
        t = q0 + jnp.arange(Q_BLK)
        qb = lax.dynamic_slice_in_dim(q, q0, Q_BLK, axis=1)
        z = jnp.einsum('bqhd,bkhd->bhqk', qb, k).astype(f32) * scale
        before = src[None, :] < t[:, None]
        log_1m = jnp.where(before, jax.nn.log_sigmoid(-z), 0.0)
        log_rem = lax.cumsum(log_1m, axis=3, reverse=True) - log_1m
        a = jnp.where(before, jnp.exp(jax.nn.log_sigmoid(z) + log_rem), 0.0)
        o = jnp.einsum('bhqk,bkhd->bqhd', a.astype(v.dtype), v)
        return o.reshape(B, Q_BLK, D_MODEL)

    return sweep_blocks(block, T) @ w_out


def diff_attention_mixer(h, w_in, lam, subln_g, w_out, layer_idx):
    B, T, _ = h.shape
    f32 = jnp.float32
    lam_init = 0.8 - 0.6 * math.exp(-0.3 * layer_idx)
    lam_f = lam.astype(f32)
    lam_full = jnp.exp(jnp.sum(lam_f[0] * lam_f[1])) - jnp.exp(jnp.sum(lam_f[2] * lam_f[3])) + lam_init
    qa, ka, va = jnp.split(h @ w_in, 3, axis=-1)
    q = qa.reshape(B, T, DIFF_HEADS, 2, DIFF_DH)
    k = ka.reshape(B, T, DIFF_HEADS, 2, DIFF_DH)
    v = va.reshape(B, T, DIFF_HEADS, 2 * DIFF_DH)
    scale = DIFF_DH ** -0.5
    slopes = alibi_slopes(DIFF_HEADS)
    src = jnp.arange(T)

    def block(q0):
        t = q0 + jnp.arange(Q_BLK)
        qb = lax.dynamic_slice_in_dim(q, q0, Q_BLK, axis=1)
        dist = (t[:, None] - src[None, :]).astype(f32)
        s = jnp.einsum('bqhmd,bkhmd->bhmqk', qb, k).astype(f32) * scale
        s = jnp.where(dist >= 0, s - slopes[None, :, None, None, None] * dist, NEG)
        p = jax.nn.softmax(s, axis=-1)
        a = p[:, :, 0] - lam_full * p[:, :, 1]
        o = jnp.einsum('bhqk,bkhe->bqhe', a.astype(v.dtype), v)
        o = rms_norm(o, subln_g) * (1.0 - lam_init)
        return o.reshape(B, Q_BLK, D_MODEL)

    return sweep_blocks(block, T) @ w_out


def setup_inputs(seed: int = 0) -> dict:
    key = jax.random.key(seed)
    ks = jax.random.split(key, 18)

    def nrm(k, shape, s):
        return s * jax.random.normal(k, shape, jnp.float32)

    return {
        'x': nrm(ks[0], (BATCH, SEQ, D_MODEL), 1.0),
        'c': nrm(ks[1], (BATCH, D_MODEL), 1.0),
        'ada_w': nrm(ks[2], (DEPTH, D_MODEL, 9 * D_MODEL), 0.5 * D_MODEL ** -0.5),
        'ada_b': nrm(ks[3], (DEPTH, 9 * D_MODEL), 0.02),
        'norm_g': 1.0 + nrm(ks[4], (DEPTH, 6, D_MODEL), 0.05),
        'ffn_w1': nrm(ks[5], (DEPTH, 2, D_MODEL, 2 * D_FF), D_MODEL ** -0.5),
        'ffn_w2': nrm(ks[6], (DEPTH, 2, D_FF, D_MODEL), D_FF ** -0.5),
        'nsa_w_in': nrm(ks[7], (N_A, D_MODEL, NSA_IN), D_MODEL ** -0.5),
        'nsa_cmp_pe': nrm(ks[8], (N_A, 2, CMP_LEN, NSA_DK), 0.1),
        'nsa_cmp_w1': nrm(ks[9], (N_A, 2, CMP_LEN * NSA_DK, CMP_HID), (CMP_LEN * NSA_DK) ** -0.5),
        'nsa_cmp_w2': nrm(ks[10], (N_A, 2, CMP_HID, NSA_DK), CMP_HID ** -0.5),
        'nsa_w_out': nrm(ks[11], (N_A, NSA_QW, D_MODEL), NSA_QW ** -0.5),
        'sb_w_in': nrm(ks[12], (N_B, D_MODEL, 3 * D_MODEL), D_MODEL ** -0.5),
        'sb_w_out': nrm(ks[13], (N_B, D_MODEL, D_MODEL), D_MODEL ** -0.5),
        'diff_w_in': nrm(ks[14], (N_C, D_MODEL, 3 * D_MODEL), D_MODEL ** -0.5),
        'diff_lam': nrm(ks[15], (N_C, 4, DIFF_DH), 0.1),
        'diff_subln_g': 1.0 + nrm(ks[16], (N_C, 2 * DIFF_DH), 0.05),
        'diff_w_out': nrm(ks[17], (N_C, D_MODEL, D_MODEL), D_MODEL ** -0.5),
    }


def reference(x, c, ada_w, ada_b, norm_g, ffn_w1, ffn_w2, nsa_w_in, nsa_cmp_pe, nsa_cmp_w1, nsa_cmp_w2,
              nsa_w_out, sb_w_in, sb_w_out, diff_w_in, diff_lam, diff_subln_g, diff_w_out):
    B, T, D = x.shape
    cond = jax.nn.silu(c)
    for i in range(DEPTH):
        mod = (cond @ ada_w[i] + ada_b[i]).reshape(B, 3, 3, D)

        def sublayer(x, sidx, fn, res_w):
            shift = mod[:, sidx, 0][:, None, :]
            scl = mod[:, sidx, 1][:, None, :]
            gate = mod[:, sidx, 2][:, None, :]
            hh = rms_norm(x, norm_g[i, 2 * sidx]) * (1.0 + scl) + shift
            return x + res_w * gate * rms_norm(fn(hh), norm_g[i, 2 * sidx + 1])

        x = sublayer(x, 0, lambda hh: swiglu(hh, ffn_w1[i, 0], ffn_w2[i, 0]), 0.5)
        kind, j = i % N_MIXERS, i // N_MIXERS
        if kind == 0:
            mixer = lambda hh: nsa_mixer(hh, nsa_w_in[j], nsa_cmp_pe[j], nsa_cmp_w1[j], nsa_cmp_w2[j], nsa_w_out[j])
        elif kind == 1:
            mixer = lambda hh: stick_breaking_mixer(hh, sb_w_in[j], sb_w_out[j])
        else:
            mixer = lambda hh: diff_attention_mixer(hh, diff_w_in[j], diff_lam[j], diff_subln_g[j], diff_w_out[j], i)
        x = sublayer(x, 1, mixer, 1.0)
        x = sublayer(x, 2, lambda hh: swiglu(hh, ffn_w1[i, 1], ffn_w2[i, 1]), 0.5)
    return x
```

```python
import functools
import math

import jax
import jax.numpy as jnp
from jax import lax
from jax.experimental import pallas as pl
from jax.experimental.pallas import tpu as pltpu

F32 = jnp.float32
BF16 = jnp.bfloat16

D_MODEL = 1024
DEPTH = 4
N_MIXERS = 3
EPS = 1e-6
NEG = -1e30
FORCE = 1e4
LOG2E = 1.4426950408889634
NSA_HEADS = 16
NSA_DK = 64
NSA_KV = 4
NSA_REP = NSA_HEADS // NSA_KV
CMP_LEN = 32
CMP_STRIDE = 16
CMP_HID = 256
SEL_LEN = 64
SEL_TOPK = 16
WIN = 512
NSA_QW = NSA_HEADS * NSA_DK
NSA_KVW = NSA_KV * NSA_DK
N_SEL_PAD = 128
SB_HEADS = 16
SB_DH = D_MODEL // SB_HEADS
DIFF_HEADS = 8
DIFF_DH = D_MODEL // (2 * DIFF_HEADS)
D_FF = 2816

LANES = 128
ROW_TILE = 512
FF_CHUNK = 256
VMEM_LIMIT = 56 * 1024 * 1024


def _cparams(sem):
    return pltpu.CompilerParams(dimension_semantics=sem, vmem_limit_bytes=VMEM_LIMIT)


def _const_spec(shape):
    n = len(shape)
    return pl.BlockSpec(shape, lambda *_: (0,) * n)


def _rms(x, g):
    return x * lax.rsqrt(jnp.mean(x * x, axis=-1, keepdims=True) + EPS) * g


def _dot(a, b):
    return jnp.dot(a, b, preferred_element_type=F32)


def _dot_nt(a, b):
    return lax.dot_general(a, b, (((1,), (1,)), ((), ())), preferred_element_type=F32)


def _mod_kernel(c_ref, w_ref, b_ref, o_ref):
    c = c_ref[...]
    cond = c * jax.nn.sigmoid(c)
    o_ref[0] = _dot(cond.astype(BF16), w_ref[0].astype(BF16)) + b_ref[0]


def _modulation(c, ada_w, ada_b):
    depth, d, n = ada_w.shape
    b = c.shape[0]
    rows = 8
    c_pad = jnp.zeros((rows, d), F32).at[:b].set(c)
    nc = 1152
    out = pl.pallas_call(
        _mod_kernel,
        grid=(depth, n // nc),
        in_specs=[
            pl.BlockSpec((rows, d), lambda i, j: (0, 0)),
            pl.BlockSpec((1, d, nc), lambda i, j: (i, 0, j)),
            pl.BlockSpec((1, 1, nc), lambda i, j: (i, 0, j)),
        ],
        out_specs=pl.BlockSpec((1, rows, nc), lambda i, j: (i, 0, j)),
        out_shape=jax.ShapeDtypeStruct((depth, rows, n), F32),
        compiler_params=_cparams(("arbitrary", "arbitrary")),
        name="adaln_mod",
    )(c_pad, ada_w, ada_b.reshape(depth, 1, n))
    return out[:, :b].reshape(depth, b, 3, 3, d)


def _inproj_kernel(*refs, tt, nca, ncb, has_b):
    if has_b:
        (x_ref, g_ref, scl_ref, sh_ref, wa_ref, cs_ref, b0_ref, pv_ref, wb_ref, cb_ref,
         oa_ref, ob_ref) = refs
    else:
        x_ref, g_ref, scl_ref, sh_ref, wa_ref, cs_ref, b0_ref, pv_ref, oa_ref = refs
    x = x_ref[0]
    hh = (_rms(x, g_ref[...]) * (1.0 + scl_ref[0]) + sh_ref[0]).astype(BF16)
    t0 = pl.program_id(1) * tt
    pos = (t0 + lax.broadcasted_iota(jnp.int32, (tt, 1), 0)).astype(F32)
    na = wa_ref.shape[1]
    for c in range(na // nca):
        sl = slice(c * nca, (c + 1) * nca)
        acc = _dot(hh, wa_ref[:, sl])
        y = acc * cs_ref[:, sl] + b0_ref[:, sl] + pos * pv_ref[:, sl]
        oa_ref[0, :, sl] = y.astype(oa_ref.dtype)
    if has_b:
        tpos = t0 + lax.broadcasted_iota(jnp.int32, (1, tt), 1)
        phi = ((tpos >> 7) << 7).astype(F32)
        plo = (tpos & 127).astype(F32)
        nb = wb_ref.shape[0]
        for c in range(nb // ncb):
            sl = slice(c * ncb, (c + 1) * ncb)
            acc = _dot_nt(wb_ref[sl, :], hh)
            cc = cb_ref[sl, :]
            y = acc + cc[:, 0:1] * phi + cc[:, 1:2] * plo + cc[:, 2:3]
            ob_ref[0, sl, :] = y.astype(ob_ref.dtype)


def _inproj(x, g, scl, shift, wa, cs, b0, pv, wb=None, cb=None):
    bsz, t, d = x.shape
    tt = min(ROW_TILE, t)
    na = wa.shape[1]
    has_b = wb is not None
    in_specs = [
        pl.BlockSpec((1, tt, d), lambda b, i: (b, i, 0)),
        _const_spec((1, d)),
        pl.BlockSpec((1, 1, d), lambda b, i: (b, 0, 0)),
        pl.BlockSpec((1, 1, d), lambda b, i: (b, 0, 0)),
        _const_spec((d, na)),
        _const_spec((1, na)),
        _const_spec((1, na)),
        _const_spec((1, na)),
    ]
    args = [x, g.reshape(1, d), scl, shift, wa, cs, b0, pv]
    out_specs = [pl.BlockSpec((1, tt, na), lambda b, i: (b, i, 0))]
    out_shape = [jax.ShapeDtypeStruct((bsz, t, na), BF16)]
    ncb = 512
    if has_b:
        nb = wb.shape[0]
        in_specs += [_const_spec((nb, d)), _const_spec((nb, 8))]
        args += [wb, cb]
        out_specs.append(pl.BlockSpec((1, nb, tt), lambda b, i: (b, 0, i)))
        out_shape.append(jax.ShapeDtypeStruct((bsz, nb, t), BF16))
    outs = pl.pallas_call(
        functools.partial(_inproj_kernel, tt=tt, nca=512, ncb=ncb, has_b=has_b),
        grid=(bsz, t // tt),
        in_specs=in_specs,
        out_specs=out_specs,
        out_shape=out_shape,
        compiler_params=_cparams(("arbitrary", "arbitrary")),
        name="mixer_inproj",
    )(*args)
    return outs if has_b else (outs[0], None)


def _outproj_kernel(o_ref, w_ref, x_ref, gate_ref, g2_ref, out_ref):
    y = _dot(o_ref[0], w_ref[...])
    out_ref[0] = x_ref[0] + gate_ref[0] * _rms(y, g2_ref[...])


def _outproj(o, w, x, gate, g2):
    bsz, t, d = x.shape
    tt = min(ROW_TILE, t)
    k = o.shape[2]
    return pl.pallas_call(
        _outproj_kernel,
        grid=(bsz, t // tt),
        in_specs=[
            pl.BlockSpec((1, tt, k), lambda b, i: (b, i, 0)),
            _const_spec((k, d)),
            pl.BlockSpec((1, tt, d), lambda b, i: (b, i, 0)),
            pl.BlockSpec((1, 1, d), lambda b, i: (b, 0, 0)),
            _const_spec((1, d)),
        ],
        out_specs=pl.BlockSpec((1, tt, d), lambda b, i: (b, i, 0)),
        out_shape=jax.ShapeDtypeStruct((bsz, t, d), F32),
        compiler_params=_cparams(("arbitrary", "arbitrary")),
        name="mixer_outproj",
    )(o, w, x, gate, g2.reshape(1, d))


def _ffn_kernel(x_ref, g1_ref, scl_ref, sh_ref, gate_ref, g2_ref, w1_ref, w2_ref, o_ref, acc_ref):
    x = x_ref[0]
    hh = (_rms(x, g1_ref[...]) * (1.0 + scl_ref[0]) + sh_ref[0]).astype(BF16)
    dff = w2_ref.shape[0]
    for c in range(dff // FF_CHUNK):
        lo = c * FF_CHUNK
        gt = _dot(hh, w1_ref[:, lo:lo + FF_CHUNK])
        up = _dot(hh, w1_ref[:, dff + lo:dff + lo + FF_CHUNK])
        act = (gt * jax.nn.sigmoid(gt) * up).astype(BF16)
        part = _dot(act, w2_ref[lo:lo + FF_CHUNK, :])
        if c == 0:
            acc_ref[...] = part
        else:
            acc_ref[...] += part
    o_ref[0] = x + 0.5 * gate_ref[0] * _rms(acc_ref[...], g2_ref[...])


def _ffn(x, g1, scl, shift, gate, g2, w1, w2):
    bsz, t, d = x.shape
    tt = min(ROW_TILE, t)
    vec = pl.BlockSpec((1, 1, d), lambda b, i: (b, 0, 0))
    return pl.pallas_call(
        _ffn_kernel,
        grid=(bsz, t // tt),
        in_specs=[
            pl.BlockSpec((1, tt, d), lambda b, i: (b, i, 0)),
            _const_spec((1, d)), vec, vec, vec, _const_spec((1, d)),
            _const_spec(w1.shape), _const_spec(w2.shape),
        ],
        out_specs=pl.BlockSpec((1, tt, d), lambda b, i: (b, i, 0)),
        out_shape=jax.ShapeDtypeStruct((bsz, t, d), F32),
        scratch_shapes=[pltpu.VMEM((tt, d), F32)],
        compiler_params=_cparams(("arbitrary", "arbitrary")),
        name="ffn",
    )(x, g1.reshape(1, d), scl, shift, gate, g2.reshape(1, d), w1, w2)


def _online_step(s, m, l, acc, v):
    m_new = jnp.maximum(m, jnp.max(s, axis=-1, keepdims=True))
    alpha = jnp.exp2(m - m_new)
    p = jnp.exp2(s - m_new)
    l = alpha * l + jnp.sum(p, axis=-1, keepdims=True)
    acc = alpha * acc + _dot(p.astype(BF16), v)
    return m_new, l, acc


def _softmax_init(rows, width):
    return (jnp.full((rows, 1), NEG, F32), jnp.zeros((rows, 1), F32), jnp.zeros((rows, width), F32))


def _gelu_tanh(x):
    return 0.5 * x * (1.0 + jnp.tanh(math.sqrt(2.0 / math.pi) * (x + 0.044715 * (x * x * x))))


def _cmp_kernel(xk_ref, xv_ref, pe_ref, w1_ref, w2kt_ref, w2v_ref, kt_ref, v_ref):
    nc = xk_ref.shape[2]
    half = xk_ref.shape[3]

    def hidden(x_ref, j):
        x = x_ref[0, 0].astype(F32)
        top = _dot((x + pe_ref[j, 0:1, :]).astype(BF16), w1_ref[j, 0:half, :])
        bot = _dot((x + pe_ref[j, 1:2, :]).astype(BF16), w1_ref[j, half:2 * half, :])
        return _gelu_tanh(top + pltpu.roll(bot, nc - 1, axis=0)).astype(BF16)

    hk = hidden(xk_ref, 0)
    hv = hidden(xv_ref, 1)
    kt_ref[0, 0, 0:NSA_DK, :] = _dot_nt(w2kt_ref[...], hk).astype(BF16)
    n = lax.broadcasted_iota(jnp.int32, (1, nc), 1)
    end = n * CMP_STRIDE + (CMP_LEN - 1)
    phi = ((end >> 7) << 7).astype(F32)
    plo = (end & 127).astype(F32)
    r = lax.broadcasted_iota(jnp.int32, (LANES - NSA_DK, 1), 0)
    aug = jnp.where(r < 3, phi, jnp.where(r < 6, plo, jnp.where(r == 6, 1.0, 0.0)))
    kt_ref[0, 0, NSA_DK:LANES, :] = aug.astype(BF16)
    v_ref[0, 0] = _dot(hv, w2v_ref[...]).astype(BF16)


def _nsa_compress(xk, xv, pe, w1, w2kt, w2v):
    bsz, g, nc, half = xk.shape
    xspec = pl.BlockSpec((1, 1, nc, half), lambda b, j: (b, j, 0, 0))
    return pl.pallas_call(
        _cmp_kernel,
        grid=(bsz, g),
        in_specs=[xspec, xspec, _const_spec(pe.shape), _const_spec(w1.shape), _const_spec(w2kt.shape),
                  _const_spec(w2v.shape)],
        out_specs=[pl.BlockSpec((1, 1, LANES, nc), lambda b, j: (b, j, 0, 0)),
                   pl.BlockSpec((1, 1, nc, LANES), lambda b, j: (b, j, 0, 0))],
        out_shape=[jax.ShapeDtypeStruct((bsz, g, LANES, nc), BF16),
                   jax.ShapeDtypeStruct((bsz, g, nc, LANES), BF16)],
        compiler_params=_cparams(("arbitrary", "arbitrary")),
        name="nsa_compress",
    )(xk, xv, pe, w1, w2kt, w2v)


def _split3(x):
    hi = x.astype(BF16)
    r1 = x - hi.astype(F32)
    mid = r1.astype(BF16)
    lo = (r1 - mid.astype(F32)).astype(BF16)
    return hi, mid, lo


def _nsa_attn_kernel(qa_ref, kct_ref, vc_ref, kst_ref, vs_ref, kwt_ref, vw_ref, gt_ref, e_ref, ov_ref,
                     o_ref, *, tq, tk):
    i = pl.program_id(2)
    q0 = i * tq
    rows = NSA_REP * tq
    qa = qa_ref[0]
    qs = jnp.concatenate([qa[:, r * LANES:(r + 1) * LANES] for r in range(NSA_REP)], axis=0)
    t_row = q0 + (lax.broadcasted_iota(jnp.int32, (rows, 1), 0) & (tq - 1))

    nc = kct_ref.shape[3]
    s = _dot(qs, kct_ref[0, 0])
    cend = lax.broadcasted_iota(jnp.int32, (1, nc), 1) * CMP_STRIDE + (CMP_LEN - 1)
    valid = cend <= t_row
    s = jnp.where(valid, s, NEG)
    e = jnp.where(valid, jnp.exp2(s - jnp.max(s, axis=-1, keepdims=True)), 0.0)
    l = jnp.sum(e, axis=-1, keepdims=True)
    p = e / jnp.where(l > 0.0, l, 1.0)
    o_cmp = _dot(p.astype(BF16), vc_ref[0, 0])

    p4 = p[0:tq]
    for r in range(1, NSA_REP):
        p4 = p4 + p[r * tq:(r + 1) * tq]
    ov = ov_ref[...]
    imp = sum(_dot(part, ov) for part in _split3(p4))

    tq_row = q0 + lax.broadcasted_iota(jnp.int32, (tq, 1), 0)
    cur = tq_row >> 6
    sid = lax.broadcasted_iota(jnp.int32, (1, N_SEL_PAD), 1)
    forced = (sid == 0) | (sid == cur) | (sid == cur - 1)
    work = jnp.where(forced, imp + FORCE, imp)
    work = jnp.where(sid <= cur, work, NEG)
    selb = jnp.full((tq, N_SEL_PAD), NEG, F32)
    for _ in range(SEL_TOPK):
        mx = jnp.max(work, axis=-1, keepdims=True)
        first = jnp.min(jnp.where(work == mx, sid, N_SEL_PAD), axis=-1, keepdims=True)
        pick = sid == first
        selb = jnp.where(pick, 0.0, selb)
        work = jnp.where(pick, -3.0e38, work)
    selb = selb.astype(BF16)
    selb4 = jnp.concatenate([selb] * NSA_REP, axis=0)

    def sel_scores(st):
        return _dot(qs, kst_ref[0, :, pl.ds(st, tk)]) + _dot(selb4, e_ref[:, pl.ds(st, tk)])

    def sel_body(j, carry):
        st = pl.multiple_of(j * tk, tk)
        return _online_step(sel_scores(st), *carry, vs_ref[0, pl.ds(st, tk), :])

    jd = q0 // tk
    carry = lax.fori_loop(0, jd, sel_body, _softmax_init(rows, LANES))
    st = pl.multiple_of(jd * tk, tk)
    col = st + lax.broadcasted_iota(jnp.int32, (1, tk), 1)
    s = jnp.where(col <= t_row, sel_scores(st), NEG)
    _, l, acc = _online_step(s, *carry, vs_ref[0, pl.ds(st, tk), :])
    o_sel = acc / l

    wk = WIN + tq
    st = pl.multiple_of(jnp.maximum(q0 - WIN, 0), LANES)
    col = st + lax.broadcasted_iota(jnp.int32, (1, wk), 1)
    valid = (col <= t_row) & (col > t_row - WIN)
    s = jnp.where(valid, _dot(qs, kwt_ref[0, :, pl.ds(st, wk)]), NEG)
    e = jnp.exp2(s - jnp.max(s, axis=-1, keepdims=True))
    o_win = _dot(e.astype(BF16), vw_ref[0, pl.ds(st, wk), :]) / jnp.sum(e, axis=-1, keepdims=True)

    gs = jax.nn.sigmoid(gt_ref[0].astype(F32))
    outs = []
    for r in range(NSA_REP):
        sl = slice(r * tq, (r + 1) * tq)
        outs.append(gs[:, 3 * r:3 * r + 1] * o_cmp[sl] + gs[:, 3 * r + 1:3 * r + 2] * o_sel[sl]
                    + gs[:, 3 * r + 2:3 * r + 3] * o_win[sl])
    pairs = [outs[2 * a] + pltpu.roll(outs[2 * a + 1], NSA_DK, axis=1) for a in range(NSA_REP // 2)]
    o_ref[0] = jnp.concatenate(pairs, axis=1).astype(o_ref.dtype)


def _nsa_attention(oa, ob, kct, vcp, e_mat, ov_mat, t):
    bsz = oa.shape[0]
    tq, tk = 128, 512
    qw = NSA_REP * LANES
    vs_blk = (NSA_HEADS * LANES + 2 * NSA_KVW) // LANES
    vw_blk = vs_blk + NSA_KV
    gt_blk = vw_blk + NSA_KV
    nc = kct.shape[3]
    return pl.pallas_call(
        functools.partial(_nsa_attn_kernel, tq=tq, tk=tk),
        grid=(bsz, NSA_KV, t // tq),
        in_specs=[
            pl.BlockSpec((1, tq, qw), lambda b, g, i: (b, i, g)),
            pl.BlockSpec((1, 1, LANES, nc), lambda b, g, i: (b, g, 0, 0)),
            pl.BlockSpec((1, 1, nc, LANES), lambda b, g, i: (b, g, 0, 0)),
            pl.BlockSpec((1, LANES, t), lambda b, g, i: (b, g, 0)),
            pl.BlockSpec((1, t, LANES), lambda b, g, i: (b, 0, vs_blk + g)),
            pl.BlockSpec((1, LANES, t), lambda b, g, i: (b, NSA_KV + g, 0)),
            pl.BlockSpec((1, t, LANES), lambda b, g, i: (b, 0, vw_blk + g)),
            pl.BlockSpec((1, tq, LANES), lambda b, g, i: (b, i, gt_blk + g)),
            _const_spec(e_mat.shape),
            _const_spec(ov_mat.shape),
        ],
        out_specs=pl.BlockSpec((1, tq, NSA_REP * NSA_DK), lambda b, g, i: (b, i, g)),
        out_shape=jax.ShapeDtypeStruct((bsz, t, NSA_QW), BF16),
        compiler_params=_cparams(("arbitrary", "arbitrary", "arbitrary")),
        name="nsa_attention",
    )(oa, kct, vcp, ob, oa, ob, oa, oa, e_mat, ov_mat)


def _sb_kernel(q_ref, kt_ref, v_ref, o_ref, *, tq):
    i = pl.program_id(2)
    q = q_ref[0]
    lane = lax.broadcasted_iota(jnp.int32, (1, LANES), 1)
    row = lax.broadcasted_iota(jnp.int32, (tq, 1), 0)
    col = lax.broadcasted_iota(jnp.int32, (1, tq), 1)
    before = col < row
    upper = (lax.broadcasted_iota(jnp.int32, (tq, tq), 0) > lax.broadcasted_iota(jnp.int32, (tq, tq), 1)
             ).astype(BF16)

    def tile(j, rem_right, acc, qm, masked):
        st = pl.multiple_of(j * tq, tq)
        z = _dot(qm, kt_ref[0, :, pl.ds(st, tq)])
        log_1m = -(jnp.maximum(z, 0.0) + jnp.log1p(jnp.exp(-jnp.abs(z))))
        if masked:
            log_1m = jnp.where(before, log_1m, 0.0)
        hi = log_1m.astype(BF16)
        lo = (log_1m - hi.astype(F32)).astype(BF16)
        rem = _dot(hi, upper) + _dot(lo, upper)
        ex = z + log_1m + rem + rem_right
        if masked:
            ex = jnp.where(before, ex, NEG)
        acc = acc + _dot(jnp.exp(ex).astype(BF16), v_ref[0, pl.ds(st, tq), :])
        return rem_right + rem[:, 0:1] + log_1m[:, 0:1], acc

    outs = []
    for h in range(2):
        qm = jnp.where((lane >= h * SB_DH) & (lane < (h + 1) * SB_DH), q, jnp.zeros_like(q))
        carry = tile(i, jnp.zeros((tq, 1), F32), jnp.zeros((tq, LANES), F32), qm, True)
        _, acc = lax.fori_loop(0, i, lambda it, c: tile(i - 1 - it, c[0], c[1], qm, False), carry)
        outs.append(acc)
    o_ref[0] = jnp.where(lane < SB_DH, outs[0], outs[1]).astype(o_ref.dtype)


def _sb_attention(oa, ob, t):
    bsz = oa.shape[0]
    tq = 256
    npair = SB_HEADS // 2
    return pl.pallas_call(
        functools.partial(_sb_kernel, tq=tq),
        grid=(bsz, npair, t // tq),
        in_specs=[
            pl.BlockSpec((1, tq, LANES), lambda b, h, i: (b, i, h)),
            pl.BlockSpec((1, LANES, t), lambda b, h, i: (b, h, 0)),
            pl.BlockSpec((1, t, LANES), lambda b, h, i: (b, 0, npair + h)),
        ],
        out_specs=pl.BlockSpec((1, tq, LANES), lambda b, h, i: (b, i, h)),
        out_shape=jax.ShapeDtypeStruct((bsz, t, D_MODEL), BF16),
        compiler_params=_cparams(("arbitrary", "arbitrary", "arbitrary")),
        name="sb_attention",
    )(oa, ob, oa)


def _diff_kernel(lam_ref, g_ref, qa_ref, kt_ref, v_ref, o_ref, *, tq, tk, lam_init):
    i = pl.program_id(2)
    q0 = i * tq
    qa = qa_ref[0]
    lane = lax.broadcasted_iota(jnp.int32, (1, 2 * LANES), 1)
    t_row = q0 + lax.broadcasted_iota(jnp.int32, (tq, 1), 0)
    jd = q0 // tk
    outs = []
    for m_ in range(2):
        keep = ((lane >= m_ * DIFF_DH) & (lane < (m_ + 1) * DIFF_DH)) | (lane >= LANES)
        qm = jnp.where(keep, qa, jnp.zeros_like(qa))

        def body(j, carry, qm=qm):
            st = pl.multiple_of(j * tk, tk)
            return _online_step(_dot(qm, kt_ref[0, :, pl.ds(st, tk)]), *carry, v_ref[0, pl.ds(st, tk), :])

        carry = lax.fori_loop(0, jd, body, _softmax_init(tq, LANES))
        st = pl.multiple_of(jd * tk, tk)
        col = st + lax.broadcasted_iota(jnp.int32, (1, tk), 1)
        s = jnp.where(col <= t_row, _dot(qm, kt_ref[0, :, pl.ds(st, tk)]), NEG)
        _, l, acc = _online_step(s, *carry, v_ref[0, pl.ds(st, tk), :])
        outs.append(acc / l)
    lam = lam_ref[...]
    lam_full = (jnp.exp(jnp.sum(lam[0:1] * lam[1:2], axis=-1, keepdims=True))
                - jnp.exp(jnp.sum(lam[2:3] * lam[3:4], axis=-1, keepdims=True)) + lam_init)
    o = outs[0] - lam_full * outs[1]
    o_ref[0] = (_rms(o, g_ref[...]) * (1.0 - lam_init)).astype(o_ref.dtype)


def _diff_attention(oa, ob, lam, subln_g, lam_init, t):
    bsz = oa.shape[0]
    tq, tk = 256, 512
    v_blk = DIFF_HEADS * 2
    return pl.pallas_call(
        functools.partial(_diff_kernel, tq=tq, tk=tk, lam_init=lam_init),
        grid=(bsz, DIFF_HEADS, t // tq),
        in_specs=[
            _const_spec(lam.shape),
            _const_spec((1, 2 * DIFF_DH)),
            pl.BlockSpec((1, tq, 2 * LANES), lambda b, h, i: (b, i, h)),
            pl.BlockSpec((1, 2 * LANES, t), lambda b, h, i: (b, h, 0)),
            pl.BlockSpec((1, t, LANES), lambda b, h, i: (b, 0, v_blk + h)),
        ],
        out_specs=pl.BlockSpec((1, tq, LANES), lambda b, h, i: (b, i, h)),
        out_shape=jax.ShapeDtypeStruct((bsz, t, D_MODEL), BF16),
        compiler_params=_cparams(("arbitrary", "arbitrary", "arbitrary")),
        name="diff_attention",
    )(lam, subln_g.reshape(1, 2 * DIFF_DH), oa, ob, oa)


def _alibi_slopes(n):
    return jnp.exp2(-8.0 * jnp.arange(1, n + 1, dtype=F32) / n)


def _alibi_query_cols(slopes, width, first):
    n = slopes.shape[0]
    sl2 = slopes * LOG2E
    parts = jnp.stack([p.astype(F32) for p in _split3(sl2)], axis=1)
    b0 = jnp.zeros((n, width), F32)
    b0 = b0.at[:, first:first + 3].set(parts).at[:, first + 3:first + 6].set(parts)
    pv = jnp.zeros((n, width), F32).at[:, first + 6].set(-sl2)
    return b0.reshape(1, n * width), pv.reshape(1, n * width)


def _alibi_key_rows(nblocks, width, first):
    c = jnp.zeros((nblocks, width, 8), F32)
    c = c.at[:, first:first + 3, 0].set(1.0).at[:, first + 3:first + 6, 1].set(1.0).at[:, first + 6, 2].set(1.0)
    return c.reshape(nblocks * width, 8)


def _pad_blocks(w, nblocks, width):
    d = w.shape[0]
    k = w.shape[1] // nblocks
    return jnp.pad(w.reshape(d, nblocks, k), ((0, 0), (0, 0), (0, width - k))).reshape(d, nblocks * width)


def _nsa_weights(w_in):
    d = w_in.shape[0]
    bounds = [0, NSA_QW] + [NSA_QW + (k + 1) * NSA_KVW for k in range(6)]
    wq, wkc, wvc, wks, wvs, wkw, wvw = [w_in[:, bounds[k]:bounds[k + 1]] for k in range(7)]
    wg = w_in[:, bounds[7]:]
    wa = jnp.concatenate([
        _pad_blocks(wq, NSA_HEADS, LANES), wkc, wvc, _pad_blocks(wvs, NSA_KV, LANES),
        _pad_blocks(wvw, NSA_KV, LANES), _pad_blocks(wg, NSA_KV, LANES)], axis=1).astype(BF16)
    na = wa.shape[1]
    qcols = NSA_HEADS * LANES
    scale = NSA_DK ** -0.5 * LOG2E
    cs_q = jnp.where(jnp.arange(LANES) < NSA_DK, scale, 1.0).astype(F32)
    cs = jnp.concatenate([jnp.tile(cs_q, NSA_HEADS), jnp.ones((na - qcols,), F32)]).reshape(1, na)
    b0q, pvq = _alibi_query_cols(_alibi_slopes(NSA_HEADS), LANES, NSA_DK)
    b0 = jnp.pad(b0q, ((0, 0), (0, na - qcols)))
    pv = jnp.pad(pvq, ((0, 0), (0, na - qcols)))
    wb = jnp.concatenate([_pad_blocks(wks, NSA_KV, LANES), _pad_blocks(wkw, NSA_KV, LANES)], axis=1).T
    cb = _alibi_key_rows(2 * NSA_KV, LANES, NSA_DK)
    return wa, cs, b0, pv, wb.astype(BF16), cb


def _sb_weights(w_in):
    d = w_in.shape[0]
    wq, wk, wv = jnp.split(w_in, 3, axis=1)
    wa = jnp.concatenate([wq, wv], axis=1).astype(BF16)
    na = wa.shape[1]
    cs = jnp.concatenate([jnp.full((d,), SB_DH ** -0.5, F32), jnp.ones((d,), F32)]).reshape(1, na)
    zero = jnp.zeros((1, na), F32)
    return wa, cs, zero, zero, wk.T.astype(BF16), jnp.zeros((d, 8), F32)


def _diff_weights(w_in):
    d = w_in.shape[0]
    wq, wk, wv = jnp.split(w_in, 3, axis=1)
    width = 2 * LANES
    wa = jnp.concatenate([_pad_blocks(wq, DIFF_HEADS, width), wv], axis=1).astype(BF16)
    na = wa.shape[1]
    qcols = DIFF_HEADS * width
    scale = DIFF_DH ** -0.5 * LOG2E
    cs_q = jnp.where(jnp.arange(width) < 2 * DIFF_DH, scale, 1.0).astype(F32)
    cs = jnp.concatenate([jnp.tile(cs_q, DIFF_HEADS), jnp.ones((na - qcols,), F32)]).reshape(1, na)
    b0q, pvq = _alibi_query_cols(_alibi_slopes(DIFF_HEADS), width, 2 * DIFF_DH)
    b0 = jnp.pad(b0q, ((0, 0), (0, na - qcols)))
    pv = jnp.pad(pvq, ((0, 0), (0, na - qcols)))
    wb = _pad_blocks(wk, DIFF_HEADS, width).T.astype(BF16)
    cb = _alibi_key_rows(DIFF_HEADS, width, 2 * DIFF_DH)
    return wa, cs, b0, pv, wb, cb


def _nsa_mixer(x, g, scl, shift, w_in, cmp_pe, cmp_w1, cmp_w2):
    bsz, t, _ = x.shape
    wa, cs, b0, pv, wb, cb = _nsa_weights(w_in)
    oa, ob = _inproj(x, g, scl, shift, wa, cs, b0, pv, wb, cb)
    nc = t // CMP_STRIDE
    kc0 = NSA_HEADS * LANES

    def grouped(lo):
        a = oa[:, :, lo:lo + NSA_KVW].reshape(bsz, nc, CMP_STRIDE, NSA_KV, NSA_DK)
        return jnp.moveaxis(a, 3, 1).reshape(bsz, NSA_KV, nc, CMP_STRIDE * NSA_DK)

    pe = cmp_pe.reshape(2, 2, CMP_STRIDE * NSA_DK)
    w2kt = cmp_w2[0].T.astype(BF16)
    w2v = jnp.pad(cmp_w2[1], ((0, 0), (0, LANES - NSA_DK))).astype(BF16)
    kct, vcp = _nsa_compress(grouped(kc0), grouped(kc0 + NSA_KVW), pe, cmp_w1.astype(BF16), w2kt, w2v)
    tok = jnp.arange(t)
    e_mat = (jnp.arange(N_SEL_PAD)[:, None] == (tok // SEL_LEN)[None, :]).astype(BF16)
    cstart = jnp.arange(nc) * CMP_STRIDE
    sstart = jnp.arange(N_SEL_PAD) * SEL_LEN
    ov_mat = ((cstart[:, None] < sstart[None, :] + SEL_LEN)
              & (cstart[:, None] + CMP_LEN > sstart[None, :])
              & (jnp.arange(nc)[:, None] < nc - 1)).astype(BF16)
    return _nsa_attention(oa, ob, kct, vcp, e_mat, ov_mat, t)


def _sb_mixer(x, g, scl, shift, w_in):
    t = x.shape[1]
    wa, cs, b0, pv, wb, cb = _sb_weights(w_in)
    oa, ob = _inproj(x, g, scl, shift, wa, cs, b0, pv, wb, cb)
    return _sb_attention(oa, ob, t)


def _diff_mixer(x, g, scl, shift, w_in, lam, subln_g, layer_idx):
    t = x.shape[1]
    wa, cs, b0, pv, wb, cb = _diff_weights(w_in)
    oa, ob = _inproj(x, g, scl, shift, wa, cs, b0, pv, wb, cb)
    lam_init = 0.8 - 0.6 * math.exp(-0.3 * layer_idx)
    return _diff_attention(oa, ob, lam, subln_g, lam_init, t)


def kernel(x, c, ada_w, ada_b, norm_g, ffn_w1, ffn_w2, nsa_w_in, nsa_cmp_pe, nsa_cmp_w1, nsa_cmp_w2,
           nsa_w_out, sb_w_in, sb_w_out, diff_w_in, diff_lam, diff_subln_g, diff_w_out):
    bsz, t, d = x.shape
    assert d == D_MODEL and t % ROW_TILE == 0 and t >= WIN + 128 and t <= N_SEL_PAD * SEL_LEN
    mod = _modulation(c, ada_w, ada_b)
    for i in range(DEPTH):
        def mods(sidx):
            return tuple(mod[i, :, sidx, k][:, None, :] for k in range(3))

        shift, scl, gate = mods(0)
        x = _ffn(x, norm_g[i, 0], scl, shift, gate, norm_g[i, 1],
                 ffn_w1[i, 0].astype(BF16), ffn_w2[i, 0].astype(BF16))
        shift, scl, gate = mods(1)
        kind, j = i % N_MIXERS, i // N_MIXERS
        if kind == 0:
            o = _nsa_mixer(x, norm_g[i, 2], scl, shift, nsa_w_in[j], nsa_cmp_pe[j], nsa_cmp_w1[j],
                           nsa_cmp_w2[j])
            w_out = nsa_w_out[j]
        elif kind == 1:
            o = _sb_mixer(x, norm_g[i, 2], scl, shift, sb_w_in[j])
            w_out = sb_w_out[j]
        else:
            o = _diff_mixer(x, norm_g[i, 2], scl, shift, diff_w_in[j], diff_lam[j], diff_subln_g[j], i)
            w_out = diff_w_out[j]
        x = _outproj(o, w_out.astype(BF16), x, gate, norm_g[i, 3])
        shift, scl, gate = mods(2)
        x = _ffn(x, norm_g[i, 4], scl, shift, gate, norm_g[i, 5],
                 ffn_w1[i, 1].astype(BF16), ffn_w2[i, 1].astype(BF16))
    return x
```

```python
import functools
import math

import jax
import jax.numpy as jnp
from jax import lax
from jax.experimental import pallas as pl
from jax.experimental.pallas import tpu as pltpu

F32 = jnp.float32
BF16 = jnp.bfloat16

D_MODEL = 1024
DEPTH = 4
N_MIXERS = 3
EPS = 1e-6
NEG = -1e30
FORCE = 1e4
LOG2E = 1.4426950408889634
NSA_HEADS = 16
NSA_DK = 64
NSA_KV = 4
NSA_REP = NSA_HEADS // NSA_KV
CMP_LEN = 32
CMP_STRIDE = 16
CMP_HID = 256
SEL_LEN = 64
SEL_TOPK = 16
WIN = 512
NSA_QW = NSA_HEADS * NSA_DK
NSA_KVW = NSA_KV * NSA_DK
N_SEL_PAD = 128
SB_HEADS = 16
SB_DH = D_MODEL // SB_HEADS
DIFF_HEADS = 8
DIFF_DH = D_MODEL // (2 * DIFF_HEADS)
D_FF = 2816

LANES = 128
ROW_TILE = 512
FF_CHUNK = 256
DIFF_TQ, DIFF_TK = 256, 512
SB_TQ, SB_TK = 512, 256
VMEM_LIMIT = 56 * 1024 * 1024


def _cparams(sem):
    return pltpu.CompilerParams(dimension_semantics=sem, vmem_limit_bytes=VMEM_LIMIT)


def _const_spec(shape):
    n = len(shape)
    return pl.BlockSpec(shape, lambda *_: (0,) * n)


def _rms(x, g):
    return x * lax.rsqrt(jnp.mean(x * x, axis=-1, keepdims=True) + EPS) * g


def _dot(a, b):
    return jnp.dot(a, b, preferred_element_type=F32)


def _dot_nt(a, b):
    return lax.dot_general(a, b, (((1,), (1,)), ((), ())), preferred_element_type=F32)


def _mod_kernel(c_ref, w_ref, b_ref, o_ref):
    c = c_ref[...]
    cond = c * jax.nn.sigmoid(c)
    o_ref[0] = _dot(cond.astype(BF16), w_ref[0].astype(BF16)) + b_ref[0]


def _modulation(c, ada_w, ada_b):
    depth, d, n = ada_w.shape
    b = c.shape[0]
    rows = 8
    c_pad = jnp.zeros((rows, d), F32).at[:b].set(c)
    nc = 1152
    out = pl.pallas_call(
        _mod_kernel,
        grid=(depth, n // nc),
        in_specs=[
            pl.BlockSpec((rows, d), lambda i, j: (0, 0)),
            pl.BlockSpec((1, d, nc), lambda i, j: (i, 0, j)),
            pl.BlockSpec((1, 1, nc), lambda i, j: (i, 0, j)),
        ],
        out_specs=pl.BlockSpec((1, rows, nc), lambda i, j: (i, 0, j)),
        out_shape=jax.ShapeDtypeStruct((depth, rows, n), F32),
        compiler_params=_cparams(("arbitrary", "arbitrary")),
        name="adaln_mod",
    )(c_pad, ada_w, ada_b.reshape(depth, 1, n))
    return out[:, :b].reshape(depth, b, 3, 3, d)


def _inproj_kernel(*refs, tt, nca, ncb, has_b):
    if has_b:
        (x_ref, g_ref, scl_ref, sh_ref, wa_ref, cs_ref, b0_ref, pv_ref, wb_ref, cb_ref,
         oa_ref, ob_ref) = refs
    else:
        x_ref, g_ref, scl_ref, sh_ref, wa_ref, cs_ref, b0_ref, pv_ref, oa_ref = refs
    x = x_ref[0]
    hh = (_rms(x, g_ref[...]) * (1.0 + scl_ref[0]) + sh_ref[0]).astype(BF16)
    t0 = pl.program_id(1) * tt
    pos = (t0 + lax.broadcasted_iota(jnp.int32, (tt, 1), 0)).astype(F32)
    na = wa_ref.shape[1]
    for c in range(na // nca):
        sl = slice(c * nca, (c + 1) * nca)
        acc = _dot(hh, wa_ref[:, sl])
        y = acc * cs_ref[:, sl] + b0_ref[:, sl] + pos * pv_ref[:, sl]
        oa_ref[0, :, sl] = y.astype(oa_ref.dtype)
    if has_b:
        tpos = t0 + lax.broadcasted_iota(jnp.int32, (1, tt), 1)
        phi = ((tpos >> 7) << 7).astype(F32)
        plo = (tpos & 127).astype(F32)
        nb = wb_ref.shape[0]
        for c in range(nb // ncb):
            sl = slice(c * ncb, (c + 1) * ncb)
            acc = _dot_nt(wb_ref[sl, :], hh)
            cc = cb_ref[sl, :]
            y = acc + cc[:, 0:1] * phi + cc[:, 1:2] * plo + cc[:, 2:3]
            ob_ref[0, sl, :] = y.astype(ob_ref.dtype)


def _inproj(x, g, scl, shift, wa, cs, b0, pv, wb=None, cb=None):
    bsz, t, d = x.shape
    tt = min(ROW_TILE, t)
    na = wa.shape[1]
    has_b = wb is not None
    in_specs = [
        pl.BlockSpec((1, tt, d), lambda b, i: (b, i, 0)),
        _const_spec((1, d)),
        pl.BlockSpec((1, 1, d), lambda b, i: (b, 0, 0)),
        pl.BlockSpec((1, 1, d), lambda b, i: (b, 0, 0)),
        _const_spec((d, na)),
        _const_spec((1, na)),
        _const_spec((1, na)),
        _const_spec((1, na)),
    ]
    args = [x, g.reshape(1, d), scl, shift, wa, cs, b0, pv]
    out_specs = [pl.BlockSpec((1, tt, na), lambda b, i: (b, i, 0))]
    out_shape = [jax.ShapeDtypeStruct((bsz, t, na), BF16)]
    ncb = 512
    if has_b:
        nb = wb.shape[0]
        in_specs += [_const_spec((nb, d)), _const_spec((nb, 8))]
        args += [wb, cb]
        out_specs.append(pl.BlockSpec((1, nb, tt), lambda b, i: (b, 0, i)))
        out_shape.append(jax.ShapeDtypeStruct((bsz, nb, t), BF16))
    outs = pl.pallas_call(
        functools.partial(_inproj_kernel, tt=tt, nca=512, ncb=ncb, has_b=has_b),
        grid=(bsz, t // tt),
        in_specs=in_specs,
        out_specs=out_specs,
        out_shape=out_shape,
        compiler_params=_cparams(("arbitrary", "arbitrary")),
        name="mixer_inproj",
    )(*args)
    return outs if has_b else (outs[0], None)


def _outproj_kernel(o_ref, w_ref, x_ref, gate_ref, g2_ref, out_ref):
    y = _dot(o_ref[0], w_ref[...])
    out_ref[0] = x_ref[0] + gate_ref[0] * _rms(y, g2_ref[...])


def _outproj(o, w, x, gate, g2):
    bsz, t, d = x.shape
    tt = min(ROW_TILE, t)
    k = o.shape[2]
    return pl.pallas_call(
        _outproj_kernel,
        grid=(bsz, t // tt),
        in_specs=[
            pl.BlockSpec((1, tt, k), lambda b, i: (b, i, 0)),
            _const_spec((k, d)),
            pl.BlockSpec((1, tt, d), lambda b, i: (b, i, 0)),
            pl.BlockSpec((1, 1, d), lambda b, i: (b, 0, 0)),
            _const_spec((1, d)),
        ],
        out_specs=pl.BlockSpec((1, tt, d), lambda b, i: (b, i, 0)),
        out_shape=jax.ShapeDtypeStruct((bsz, t, d), F32),
        compiler_params=_cparams(("arbitrary", "arbitrary")),
        name="mixer_outproj",
    )(o, w, x, gate, g2.reshape(1, d))


def _ffn_kernel(x_ref, g1_ref, scl_ref, sh_ref, gate_ref, g2_ref, w1_ref, w2_ref, o_ref, acc_ref):
    x = x_ref[0]
    hh = (_rms(x, g1_ref[...]) * (1.0 + scl_ref[0]) + sh_ref[0]).astype(BF16)
    dff = w2_ref.shape[0]
    for c in range(dff // FF_CHUNK):
        lo = c * FF_CHUNK
        gt = _dot(hh, w1_ref[:, lo:lo + FF_CHUNK])
        up = _dot(hh, w1_ref[:, dff + lo:dff + lo + FF_CHUNK])
        act = (gt * jax.nn.sigmoid(gt) * up).astype(BF16)
        part = _dot(act, w2_ref[lo:lo + FF_CHUNK, :])
        if c == 0:
            acc_ref[...] = part
        else:
            acc_ref[...] += part
    o_ref[0] = x + 0.5 * gate_ref[0] * _rms(acc_ref[...], g2_ref[...])


def _ffn(x, g1, scl, shift, gate, g2, w1, w2):
    bsz, t, d = x.shape
    tt = min(ROW_TILE, t)
    vec = pl.BlockSpec((1, 1, d), lambda b, i: (b, 0, 0))
    return pl.pallas_call(
        _ffn_kernel,
        grid=(bsz, t // tt),
        in_specs=[
            pl.BlockSpec((1, tt, d), lambda b, i: (b, i, 0)),
            _const_spec((1, d)), vec, vec, vec, _const_spec((1, d)),
            _const_spec(w1.shape), _const_spec(w2.shape),
        ],
        out_specs=pl.BlockSpec((1, tt, d), lambda b, i: (b, i, 0)),
        out_shape=jax.ShapeDtypeStruct((bsz, t, d), F32),
        scratch_shapes=[pltpu.VMEM((tt, d), F32)],
        compiler_params=_cparams(("arbitrary", "arbitrary")),
        name="ffn",
    )(x, g1.reshape(1, d), scl, shift, gate, g2.reshape(1, d), w1, w2)


def _online_step(s, m, l, acc, v):
    m_new = jnp.maximum(m, jnp.max(s, axis=-1, keepdims=True))
    alpha = jnp.exp2(m - m_new)
    p = jnp.exp2(s - m_new)
    l = alpha * l + jnp.sum(p, axis=-1, keepdims=True)
    acc = alpha * acc + _dot(p.astype(BF16), v)
    return m_new, l, acc


def _softmax_init(rows, width):
    return (jnp.full((rows, 1), NEG, F32), jnp.zeros((rows, 1), F32), jnp.zeros((rows, width), F32))


def _gelu_tanh(x):
    return 0.5 * x * (1.0 + jnp.tanh(math.sqrt(2.0 / math.pi) * (x + 0.044715 * (x * x * x))))


def _cmp_kernel(xk_ref, xv_ref, pe_ref, w1_ref, w2kt_ref, w2v_ref, kt_ref, v_ref):
    nc = xk_ref.shape[2]
    half = xk_ref.shape[3]

    def hidden(x_ref, j):
        x = x_ref[0, 0].astype(F32)
        top = _dot((x + pe_ref[j, 0:1, :]).astype(BF16), w1_ref[j, 0:half, :])
        bot = _dot((x + pe_ref[j, 1:2, :]).astype(BF16), w1_ref[j, half:2 * half, :])
        return _gelu_tanh(top + pltpu.roll(bot, nc - 1, axis=0)).astype(BF16)

    hk = hidden(xk_ref, 0)
    hv = hidden(xv_ref, 1)
    kt_ref[0, 0, 0:NSA_DK, :] = _dot_nt(w2kt_ref[...], hk).astype(BF16)
    n = lax.broadcasted_iota(jnp.int32, (1, nc), 1)
    end = n * CMP_STRIDE + (CMP_LEN - 1)
    phi = ((end >> 7) << 7).astype(F32)
    plo = (end & 127).astype(F32)
    r = lax.broadcasted_iota(jnp.int32, (LANES - NSA_DK, 1), 0)
    aug = jnp.where(r < 3, phi, jnp.where(r < 6, plo, jnp.where(r == 6, 1.0, 0.0)))
    kt_ref[0, 0, NSA_DK:LANES, :] = aug.astype(BF16)
    v_ref[0, 0] = _dot(hv, w2v_ref[...]).astype(BF16)


def _nsa_compress(xk, xv, pe, w1, w2kt, w2v):
    bsz, g, nc, half = xk.shape
    xspec = pl.BlockSpec((1, 1, nc, half), lambda b, j: (b, j, 0, 0))
    return pl.pallas_call(
        _cmp_kernel,
        grid=(bsz, g),
        in_specs=[xspec, xspec, _const_spec(pe.shape), _const_spec(w1.shape), _const_spec(w2kt.shape),
                  _const_spec(w2v.shape)],
        out_specs=[pl.BlockSpec((1, 1, LANES, nc), lambda b, j: (b, j, 0, 0)),
                   pl.BlockSpec((1, 1, nc, LANES), lambda b, j: (b, j, 0, 0))],
        out_shape=[jax.ShapeDtypeStruct((bsz, g, LANES, nc), BF16),
                   jax.ShapeDtypeStruct((bsz, g, nc, LANES), BF16)],
        compiler_params=_cparams(("arbitrary", "arbitrary")),
        name="nsa_compress",
    )(xk, xv, pe, w1, w2kt, w2v)


def _split3(x):
    hi = x.astype(BF16)
    r1 = x - hi.astype(F32)
    mid = r1.astype(BF16)
    lo = (r1 - mid.astype(F32)).astype(BF16)
    return hi, mid, lo


def _nsa_attn_kernel(qa_ref, kct_ref, vc_ref, kst_ref, vs_ref, kwt_ref, vw_ref, gt_ref, e_ref, ov_ref,
                     o_ref, tiles_ref, *, tq, tk):
    i = pl.program_id(2)
    q0 = i * tq
    rows = NSA_REP * tq
    qa = qa_ref[0]
    qs = jnp.concatenate([qa[:, r * LANES:(r + 1) * LANES] for r in range(NSA_REP)], axis=0)
    t_row = q0 + (lax.broadcasted_iota(jnp.int32, (rows, 1), 0) & (tq - 1))

    nc = kct_ref.shape[3]
    s = _dot(qs, kct_ref[0, 0])
    cend = lax.broadcasted_iota(jnp.int32, (1, nc), 1) * CMP_STRIDE + (CMP_LEN - 1)
    valid = cend <= t_row
    s = jnp.where(valid, s, NEG)
    e = jnp.where(valid, jnp.exp2(s - jnp.max(s, axis=-1, keepdims=True)), 0.0)
    l = jnp.sum(e, axis=-1, keepdims=True)
    p = e / jnp.where(l > 0.0, l, 1.0)
    o_cmp = _dot(p.astype(BF16), vc_ref[0, 0])

    p4 = p[0:tq]
    for r in range(1, NSA_REP):
        p4 = p4 + p[r * tq:(r + 1) * tq]
    ov = ov_ref[...]
    imp = sum(_dot(part, ov) for part in _split3(p4))

    wk = WIN + tq
    st = pl.multiple_of(jnp.maximum(q0 - WIN, 0), LANES)
    col = st + lax.broadcasted_iota(jnp.int32, (1, wk), 1)
    valid = (col <= t_row) & (col > t_row - WIN)
    s = jnp.where(valid, _dot(qs, kwt_ref[0, :, pl.ds(st, wk)]), NEG)
    e = jnp.exp2(s - jnp.max(s, axis=-1, keepdims=True))
    o_win = _dot(e.astype(BF16), vw_ref[0, pl.ds(st, wk), :]) / jnp.sum(e, axis=-1, keepdims=True)

    tq_row = q0 + lax.broadcasted_iota(jnp.int32, (tq, 1), 0)
    cur = tq_row >> 6
    sid = lax.broadcasted_iota(jnp.int32, (1, N_SEL_PAD), 1)
    forced = (sid == 0) | (sid == cur) | (sid == cur - 1)
    work = jnp.where(forced | (sid > cur), -3.0e38, imp)
    selb = jnp.where(forced, 0.0, NEG)
    for _ in range(SEL_TOPK - 3):
        mx = jnp.max(work, axis=-1, keepdims=True)
        first = jnp.min(jnp.where(work == mx, sid, N_SEL_PAD), axis=-1, keepdims=True)
        pick = sid == first
        selb = jnp.where(pick, 0.0, selb)
        work = jnp.where(pick, -3.0e38, work)

    jd = q0 // tk
    n_tiles = kst_ref.shape[2] // tk
    any_sel = jnp.max(jnp.where(selb == 0.0, 1.0, 0.0), axis=0, keepdims=True)
    blk_tile = (lax.broadcasted_iota(jnp.int32, (N_SEL_PAD, LANES), 0) // (tk // SEL_LEN)
                == lax.broadcasted_iota(jnp.int32, (N_SEL_PAD, LANES), 1)).astype(BF16)
    tile_cnt = _dot(jnp.broadcast_to(any_sel, (16, N_SEL_PAD)).astype(BF16), blk_tile)
    n_act = jnp.int32(0)
    for j in range(n_tiles):
        tiles_ref[n_act] = j
        n_act = n_act + ((tile_cnt[0, j] > 0.0) & (j < jd)).astype(jnp.int32)

    selb4 = jnp.concatenate([selb.astype(BF16)] * NSA_REP, axis=0)
    qsel = jnp.concatenate([qs, selb4], axis=1)

    def sel_scores(st):
        return _dot(qsel, jnp.concatenate([kst_ref[0, :, pl.ds(st, tk)], e_ref[:, pl.ds(st, tk)]], axis=0))

    def sel_body(it, carry):
        st = pl.multiple_of(tiles_ref[it] * tk, tk)
        return _online_step(sel_scores(st), *carry, vs_ref[0, pl.ds(st, tk), :])

    carry = lax.fori_loop(0, n_act, sel_body, _softmax_init(rows, LANES))
    st = pl.multiple_of(jd * tk, tk)
    col = st + lax.broadcasted_iota(jnp.int32, (1, tk), 1)
    s = jnp.where(col <= t_row, sel_scores(st), NEG)
    _, l, acc = _online_step(s, *carry, vs_ref[0, pl.ds(st, tk), :])
    o_sel = acc / l

    gs = jax.nn.sigmoid(gt_ref[0].astype(F32))
    outs = []
    for r in range(NSA_REP):
        sl = slice(r * tq, (r + 1) * tq)
        outs.append(gs[:, 3 * r:3 * r + 1] * o_cmp[sl] + gs[:, 3 * r + 1:3 * r + 2] * o_sel[sl]
                    + gs[:, 3 * r + 2:3 * r + 3] * o_win[sl])
    pairs = [outs[2 * a] + pltpu.roll(outs[2 * a + 1], NSA_DK, axis=1) for a in range(NSA_REP // 2)]
    o_ref[0] = jnp.concatenate(pairs, axis=1).astype(o_ref.dtype)


def _nsa_attention(oa, ob, kct, vcp, e_mat, ov_mat, t):
    bsz = oa.shape[0]
    tq, tk = 128, 512
    qw = NSA_REP * LANES
    vs_blk = (NSA_HEADS * LANES + 2 * NSA_KVW) // LANES
    vw_blk = vs_blk + NSA_KV
    gt_blk = vw_blk + NSA_KV
    nc = kct.shape[3]
    return pl.pallas_call(
        functools.partial(_nsa_attn_kernel, tq=tq, tk=tk),
        grid=(bsz, NSA_KV, t // tq),
        in_specs=[
            pl.BlockSpec((1, tq, qw), lambda b, g, i: (b, i, g)),
            pl.BlockSpec((1, 1, LANES, nc), lambda b, g, i: (b, g, 0, 0)),
            pl.BlockSpec((1, 1, nc, LANES), lambda b, g, i: (b, g, 0, 0)),
            pl.BlockSpec((1, LANES, t), lambda b, g, i: (b, g, 0)),
            pl.BlockSpec((1, t, LANES), lambda b, g, i: (b, 0, vs_blk + g)),
            pl.BlockSpec((1, LANES, t), lambda b, g, i: (b, NSA_KV + g, 0)),
            pl.BlockSpec((1, t, LANES), lambda b, g, i: (b, 0, vw_blk + g)),
            pl.BlockSpec((1, tq, LANES), lambda b, g, i: (b, i, gt_blk + g)),
            _const_spec(e_mat.shape),
            _const_spec(ov_mat.shape),
        ],
        out_specs=pl.BlockSpec((1, tq, NSA_REP * NSA_DK), lambda b, g, i: (b, i, g)),
        out_shape=jax.ShapeDtypeStruct((bsz, t, NSA_QW), BF16),
        scratch_shapes=[pltpu.SMEM((t // tk,), jnp.int32)],
        compiler_params=_cparams(("arbitrary", "arbitrary", "arbitrary")),
        name="nsa_attention",
    )(oa, kct, vcp, ob, oa, ob, oa, oa, e_mat, ov_mat)


def _sb_kernel(q_ref, kt_ref, v_ref, o_ref, rem_ref, acc_ref, *, tq, tk):
    i = pl.program_id(2)
    q0 = i * tq
    q = q_ref[0]
    lane = lax.broadcasted_iota(jnp.int32, (1, LANES), 1)
    zero = jnp.zeros_like(q)
    qs = jnp.concatenate([jnp.where(lane < SB_DH, q, zero), jnp.where(lane >= SB_DH, q, zero)], axis=0)
    t_row = q0 + (lax.broadcasted_iota(jnp.int32, (2 * tq, 1), 0) & (tq - 1))
    r_id = lax.broadcasted_iota(jnp.int32, (2 * tk, tk), 0) & (tk - 1)
    upper = (r_id >= lax.broadcasted_iota(jnp.int32, (2 * tk, tk), 1)).astype(BF16)
    rem_ref[...] = jnp.zeros(rem_ref.shape, F32)
    acc_ref[...] = jnp.zeros(acc_ref.shape, F32)

    def tile(j, masked):
        st = pl.multiple_of(j * tk, tk)
        z = _dot(qs, kt_ref[0, :, pl.ds(st, tk)])
        nz = -z
        log_1m = jnp.minimum(nz, 0.0) - jnp.log(1.0 + jnp.exp2(jnp.minimum(z, nz))) * LOG2E
        if masked:
            before = (st + lax.broadcasted_iota(jnp.int32, (1, tk), 1)) < t_row
            log_1m = jnp.where(before, log_1m, 0.0)
        hi = log_1m.astype(BF16)
        lo = (log_1m - hi.astype(F32)).astype(BF16)
        rem = _dot(jnp.concatenate([hi, lo], axis=1), upper)
        ex = z + rem + rem_ref[...]
        if masked:
            ex = jnp.where(before, ex, NEG)
        acc_ref[...] += _dot(jnp.exp2(ex).astype(BF16), v_ref[0, pl.ds(st, tk), :])
        rem_ref[...] += rem[:, 0:1]

    n_full = q0 // tk
    for d in range(tq // tk):
        tile(n_full + tq // tk - 1 - d, True)

    per_step = tq // tk

    def body(it, _):
        for d in range(per_step):
            tile(n_full - 1 - per_step * it - d, False)
        return 0

    lax.fori_loop(0, n_full // per_step, body, 0)
    acc = acc_ref[...]
    o_ref[0] = jnp.where(lane < SB_DH, acc[0:tq], acc[tq:2 * tq]).astype(o_ref.dtype)


def _sb_attention(oa, ob, t):
    bsz = oa.shape[0]
    tq, tk = SB_TQ, SB_TK
    npair = SB_HEADS // 2
    return pl.pallas_call(
        functools.partial(_sb_kernel, tq=tq, tk=tk),
        grid=(bsz, npair, t // tq),
        in_specs=[
            pl.BlockSpec((1, tq, LANES), lambda b, h, i: (b, i, h)),
            pl.BlockSpec((1, LANES, t), lambda b, h, i: (b, h, 0)),
            pl.BlockSpec((1, t, LANES), lambda b, h, i: (b, 0, npair + h)),
        ],
        out_specs=pl.BlockSpec((1, tq, LANES), lambda b, h, i: (b, i, h)),
        out_shape=jax.ShapeDtypeStruct((bsz, t, D_MODEL), BF16),
        scratch_shapes=[pltpu.VMEM((2 * tq, 1), F32), pltpu.VMEM((2 * tq, LANES), F32)],
        compiler_params=_cparams(("arbitrary", "arbitrary", "arbitrary")),
        name="sb_attention",
    )(oa, ob, oa)


def _diff_kernel(lam_ref, g_ref, qa_ref, kt_ref, v_ref, o_ref, *, tq, tk, lam_init):
    i = pl.program_id(2)
    q0 = i * tq
    qa = qa_ref[0]
    lane = lax.broadcasted_iota(jnp.int32, (1, 2 * LANES), 1)
    zero = jnp.zeros_like(qa)
    qs = jnp.concatenate(
        [jnp.where(((lane >= m_ * DIFF_DH) & (lane < (m_ + 1) * DIFF_DH)) | (lane >= LANES), qa, zero)
         for m_ in range(2)], axis=0)
    t_row = q0 + (lax.broadcasted_iota(jnp.int32, (2 * tq, 1), 0) & (tq - 1))

    def body(j, carry):
        st = pl.multiple_of(j * tk, tk)
        return _online_step(_dot(qs, kt_ref[0, :, pl.ds(st, tk)]), *carry, v_ref[0, pl.ds(st, tk), :])

    jd = q0 // tk
    carry = lax.fori_loop(0, jd, body, _softmax_init(2 * tq, LANES))
    st = pl.multiple_of(jd * tk, tk)
    col = st + lax.broadcasted_iota(jnp.int32, (1, tk), 1)
    s = jnp.where(col <= t_row, _dot(qs, kt_ref[0, :, pl.ds(st, tk)]), NEG)
    _, l, acc = _online_step(s, *carry, v_ref[0, pl.ds(st, tk), :])
    out = acc / l
    lam = lam_ref[...]
    lam_full = (jnp.exp(jnp.sum(lam[0:1] * lam[1:2], axis=-1, keepdims=True))
                - jnp.exp(jnp.sum(lam[2:3] * lam[3:4], axis=-1, keepdims=True)) + lam_init)
    o = out[0:tq] - lam_full * out[tq:2 * tq]
    o_ref[0] = (_rms(o, g_ref[...]) * (1.0 - lam_init)).astype(o_ref.dtype)


def _diff_attention(oa, ob, lam, subln_g, lam_init, t):
    bsz = oa.shape[0]
    tq, tk = DIFF_TQ, DIFF_TK
    v_blk = DIFF_HEADS * 2
    return pl.pallas_call(
        functools.partial(_diff_kernel, tq=tq, tk=tk, lam_init=lam_init),
        grid=(bsz, DIFF_HEADS, t // tq),
        in_specs=[
            _const_spec(lam.shape),
            _const_spec((1, 2 * DIFF_DH)),
            pl.BlockSpec((1, tq, 2 * LANES), lambda b, h, i: (b, i, h)),
            pl.BlockSpec((1, 2 * LANES, t), lambda b, h, i: (b, h, 0)),
            pl.BlockSpec((1, t, LANES), lambda b, h, i: (b, 0, v_blk + h)),
        ],
        out_specs=pl.BlockSpec((1, tq, LANES), lambda b, h, i: (b, i, h)),
        out_shape=jax.ShapeDtypeStruct((bsz, t, D_MODEL), BF16),
        compiler_params=_cparams(("arbitrary", "arbitrary", "arbitrary")),
        name="diff_attention",
    )(lam, subln_g.reshape(1, 2 * DIFF_DH), oa, ob, oa)


def _alibi_slopes(n):
    return jnp.exp2(-8.0 * jnp.arange(1, n + 1, dtype=F32) / n)


def _alibi_query_cols(slopes, width, first):
    n = slopes.shape[0]
    sl2 = slopes * LOG2E
    parts = jnp.stack([p.astype(F32) for p in _split3(sl2)], axis=1)
    b0 = jnp.zeros((n, width), F32)
    b0 = b0.at[:, first:first + 3].set(parts).at[:, first + 3:first + 6].set(parts)
    pv = jnp.zeros((n, width), F32).at[:, first + 6].set(-sl2)
    return b0.reshape(1, n * width), pv.reshape(1, n * width)


def _alibi_key_rows(nblocks, width, first):
    c = jnp.zeros((nblocks, width, 8), F32)
    c = c.at[:, first:first + 3, 0].set(1.0).at[:, first + 3:first + 6, 1].set(1.0).at[:, first + 6, 2].set(1.0)
    return c.reshape(nblocks * width, 8)


def _pad_blocks(w, nblocks, width):
    d = w.shape[0]
    k = w.shape[1] // nblocks
    return jnp.pad(w.reshape(d, nblocks, k), ((0, 0), (0, 0), (0, width - k))).reshape(d, nblocks * width)


def _nsa_weights(w_in):
    d = w_in.shape[0]
    bounds = [0, NSA_QW] + [NSA_QW + (k + 1) * NSA_KVW for k in range(6)]
    wq, wkc, wvc, wks, wvs, wkw, wvw = [w_in[:, bounds[k]:bounds[k + 1]] for k in range(7)]
    wg = w_in[:, bounds[7]:]
    wa = jnp.concatenate([
        _pad_blocks(wq, NSA_HEADS, LANES), wkc, wvc, _pad_blocks(wvs, NSA_KV, LANES),
        _pad_blocks(wvw, NSA_KV, LANES), _pad_blocks(wg, NSA_KV, LANES)], axis=1).astype(BF16)
    na = wa.shape[1]
    qcols = NSA_HEADS * LANES
    scale = NSA_DK ** -0.5 * LOG2E
    cs_q = jnp.where(jnp.arange(LANES) < NSA_DK, scale, 1.0).astype(F32)
    cs = jnp.concatenate([jnp.tile(cs_q, NSA_HEADS), jnp.ones((na - qcols,), F32)]).reshape(1, na)
    b0q, pvq = _alibi_query_cols(_alibi_slopes(NSA_HEADS), LANES, NSA_DK)
    b0 = jnp.pad(b0q, ((0, 0), (0, na - qcols)))
    pv = jnp.pad(pvq, ((0, 0), (0, na - qcols)))
    wb = jnp.concatenate([_pad_blocks(wks, NSA_KV, LANES), _pad_blocks(wkw, NSA_KV, LANES)], axis=1).T
    cb = _alibi_key_rows(2 * NSA_KV, LANES, NSA_DK)
    return wa, cs, b0, pv, wb.astype(BF16), cb


def _sb_weights(w_in):
    d = w_in.shape[0]
    wq, wk, wv = jnp.split(w_in, 3, axis=1)
    wa = jnp.concatenate([wq, wv], axis=1).astype(BF16)
    na = wa.shape[1]
    cs = jnp.concatenate([jnp.full((d,), SB_DH ** -0.5 * LOG2E, F32), jnp.ones((d,), F32)]).reshape(1, na)
    zero = jnp.zeros((1, na), F32)
    return wa, cs, zero, zero, wk.T.astype(BF16), jnp.zeros((d, 8), F32)


def _diff_weights(w_in):
    d = w_in.shape[0]
    wq, wk, wv = jnp.split(w_in, 3, axis=1)
    width = 2 * LANES
    wa = jnp.concatenate([_pad_blocks(wq, DIFF_HEADS, width), wv], axis=1).astype(BF16)
    na = wa.shape[1]
    qcols = DIFF_HEADS * width
    scale = DIFF_DH ** -0.5 * LOG2E
    cs_q = jnp.where(jnp.arange(width) < 2 * DIFF_DH, scale, 1.0).astype(F32)
    cs = jnp.concatenate([jnp.tile(cs_q, DIFF_HEADS), jnp.ones((na - qcols,), F32)]).reshape(1, na)
    b0q, pvq = _alibi_query_cols(_alibi_slopes(DIFF_HEADS), width, 2 * DIFF_DH)
    b0 = jnp.pad(b0q, ((0, 0), (0, na - qcols)))
    pv = jnp.pad(pvq, ((0, 0), (0, na - qcols)))
    wb = _pad_blocks(wk, DIFF_HEADS, width).T.astype(BF16)
    cb = _alibi_key_rows(DIFF_HEADS, width, 2 * DIFF_DH)
    return wa, cs, b0, pv, wb, cb


def _nsa_mixer(x, g, scl, shift, w_in, cmp_pe, cmp_w1, cmp_w2):
    bsz, t, _ = x.shape
    wa, cs, b0, pv, wb, cb = _nsa_weights(w_in)
    oa, ob = _inproj(x, g, scl, shift, wa, cs, b0, pv, wb, cb)
    nc = t // CMP_STRIDE
    kc0 = NSA_HEADS * LANES

    def grouped(lo):
        a = oa[:, :, lo:lo + NSA_KVW].reshape(bsz, nc, CMP_STRIDE, NSA_KV, NSA_DK)
        return jnp.moveaxis(a, 3, 1).reshape(bsz, NSA_KV, nc, CMP_STRIDE * NSA_DK)

    pe = cmp_pe.reshape(2, 2, CMP_STRIDE * NSA_DK)
    w2kt = cmp_w2[0].T.astype(BF16)
    w2v = jnp.pad(cmp_w2[1], ((0, 0), (0, LANES - NSA_DK))).astype(BF16)
    kct, vcp = _nsa_compress(grouped(kc0), grouped(kc0 + NSA_KVW), pe, cmp_w1.astype(BF16), w2kt, w2v)
    tok = jnp.arange(t)
    e_mat = (jnp.arange(N_SEL_PAD)[:, None] == (tok // SEL_LEN)[None, :]).astype(BF16)
    cstart = jnp.arange(nc) * CMP_STRIDE
    sstart = jnp.arange(N_SEL_PAD) * SEL_LEN
    ov_mat = ((cstart[:, None] < sstart[None, :] + SEL_LEN)
              & (cstart[:, None] + CMP_LEN > sstart[None, :])
              & (jnp.arange(nc)[:, None] < nc - 1)).astype(BF16)
    return _nsa_attention(oa, ob, kct, vcp, e_mat, ov_mat, t)


def _sb_mixer(x, g, scl, shift, w_in):
    t = x.shape[1]
    wa, cs, b0, pv, wb, cb = _sb_weights(w_in)
    oa, ob = _inproj(x, g, scl, shift, wa, cs, b0, pv, wb, cb)
    return _sb_attention(oa, ob, t)


def _diff_mixer(x, g, scl, shift, w_in, lam, subln_g, layer_idx):
    t = x.shape[1]
    wa, cs, b0, pv, wb, cb = _diff_weights(w_in)
    oa, ob = _inproj(x, g, scl, shift, wa, cs, b0, pv, wb, cb)
    lam_init = 0.8 - 0.6 * math.exp(-0.3 * layer_idx)
    return _diff_attention(oa, ob, lam, subln_g, lam_init, t)


def kernel(x, c, ada_w, ada_b, norm_g, ffn_w1, ffn_w2, nsa_w_in, nsa_cmp_pe, nsa_cmp_w1, nsa_cmp_w2,
           nsa_w_out, sb_w_in, sb_w_out, diff_w_in, diff_lam, diff_subln_g, diff_w_out):
    bsz, t, d = x.shape
    assert d == D_MODEL and t % ROW_TILE == 0 and t >= WIN + 128 and t <= N_SEL_PAD * SEL_LEN
    mod = _modulation(c, ada_w, ada_b)
    for i in range(DEPTH):
        def mods(sidx):
            return tuple(mod[i, :, sidx, k][:, None, :] for k in range(3))

        shift, scl, gate = mods(0)
        x = _ffn(x, norm_g[i, 0], scl, shift, gate, norm_g[i, 1],
                 ffn_w1[i, 0].astype(BF16), ffn_w2[i, 0].astype(BF16))
        shift, scl, gate = mods(1)
        kind, j = i % N_MIXERS, i // N_MIXERS
        if kind == 0:
            o = _nsa_mixer(x, norm_g[i, 2], scl, shift, nsa_w_in[j], nsa_cmp_pe[j], nsa_cmp_w1[j],
                           nsa_cmp_w2[j])
            w_out = nsa_w_out[j]
        elif kind == 1:
            o = _sb_mixer(x, norm_g[i, 2], scl, shift, sb_w_in[j])
            w_out = sb_w_out[j]
        else:
            o = _diff_mixer(x, norm_g[i, 2], scl, shift, diff_w_in[j], diff_lam[j], diff_subln_g[j], i)
            w_out = diff_w_out[j]
        x = _outproj(o, w_out.astype(BF16), x, gate, norm_g[i, 3])
        shift, scl, gate = mods(2)
        x = _ffn(x, norm_g[i, 4], scl, shift, gate, norm_g[i, 5],
                 ffn_w1[i, 1].astype(BF16), ffn_w2[i, 1].astype(BF16))
    return x
```

```python
import functools
import math

import jax
import jax.numpy as jnp
from jax import lax
from jax.experimental import pallas as pl
from jax.experimental.pallas import tpu as pltpu

F32 = jnp.float32
BF16 = jnp.bfloat16

D_MODEL = 1024
DEPTH = 4
N_MIXERS = 3
EPS = 1e-6
NEG = -1e30
FORCE = 1e4
LOG2E = 1.4426950408889634
NSA_HEADS = 16
NSA_DK = 64
NSA_KV = 4
NSA_REP = NSA_HEADS // NSA_KV
CMP_LEN = 32
CMP_STRIDE = 16
CMP_HID = 256
SEL_LEN = 64
SEL_TOPK = 16
WIN = 512
NSA_QW = NSA_HEADS * NSA_DK
NSA_KVW = NSA_KV * NSA_DK
N_SEL_PAD = 128
SB_HEADS = 16
SB_DH = D_MODEL // SB_HEADS
DIFF_HEADS = 8
DIFF_DH = D_MODEL // (2 * DIFF_HEADS)
D_FF = 2816

LANES = 128
ROW_TILE = 512
FF_CHUNK = 256
DIFF_TQ, DIFF_TK = 512, 512
SB_TQ, SB_TK = 512, 256
NSA_TQ, NSA_TK = 256, 512
VMEM_LIMIT = 56 * 1024 * 1024


def _cparams(sem):
    return pltpu.CompilerParams(dimension_semantics=sem, vmem_limit_bytes=VMEM_LIMIT)


def _const_spec(shape):
    n = len(shape)
    return pl.BlockSpec(shape, lambda *_: (0,) * n)


def _rms(x, g):
    return x * lax.rsqrt(jnp.mean(x * x, axis=-1, keepdims=True) + EPS) * g


def _dot(a, b):
    return jnp.dot(a, b, preferred_element_type=F32)


def _dot_nt(a, b):
    return lax.dot_general(a, b, (((1,), (1,)), ((), ())), preferred_element_type=F32)


def _mod_kernel(c_ref, w_ref, b_ref, o_ref):
    c = c_ref[...]
    cond = c * jax.nn.sigmoid(c)
    o_ref[0] = _dot(cond.astype(BF16), w_ref[0].astype(BF16)) + b_ref[0]


def _modulation(c, ada_w, ada_b):
    depth, d, n = ada_w.shape
    b = c.shape[0]
    rows = 8
    c_pad = jnp.zeros((rows, d), F32).at[:b].set(c)
    nc = 1152
    out = pl.pallas_call(
        _mod_kernel,
        grid=(depth, n // nc),
        in_specs=[
            pl.BlockSpec((rows, d), lambda i, j: (0, 0)),
            pl.BlockSpec((1, d, nc), lambda i, j: (i, 0, j)),
            pl.BlockSpec((1, 1, nc), lambda i, j: (i, 0, j)),
        ],
        out_specs=pl.BlockSpec((1, rows, nc), lambda i, j: (i, 0, j)),
        out_shape=jax.ShapeDtypeStruct((depth, rows, n), F32),
        compiler_params=_cparams(("arbitrary", "arbitrary")),
        name="adaln_mod",
    )(c_pad, ada_w, ada_b.reshape(depth, 1, n))
    return out[:, :b].reshape(depth, b, 3, 3, d)


def _inproj_kernel(*refs, tt, nca, ncb, has_b):
    if has_b:
        (x_ref, g_ref, scl_ref, sh_ref, wa_ref, cs_ref, b0_ref, pv_ref, wb_ref, cb_ref,
         oa_ref, ob_ref) = refs
    else:
        x_ref, g_ref, scl_ref, sh_ref, wa_ref, cs_ref, b0_ref, pv_ref, oa_ref = refs
    x = x_ref[0]
    hh = (_rms(x, g_ref[...]) * (1.0 + scl_ref[0]) + sh_ref[0]).astype(BF16)
    t0 = pl.program_id(1) * tt
    pos = (t0 + lax.broadcasted_iota(jnp.int32, (tt, 1), 0)).astype(F32)
    na = wa_ref.shape[1]
    for c in range(na // nca):
        sl = slice(c * nca, (c + 1) * nca)
        acc = _dot(hh, wa_ref[:, sl])
        y = acc * cs_ref[:, sl] + b0_ref[:, sl] + pos * pv_ref[:, sl]
        oa_ref[0, :, sl] = y.astype(oa_ref.dtype)
    if has_b:
        tpos = t0 + lax.broadcasted_iota(jnp.int32, (1, tt), 1)
        phi = ((tpos >> 7) << 7).astype(F32)
        plo = (tpos & 127).astype(F32)
        nb = wb_ref.shape[0]
        for c in range(nb // ncb):
            sl = slice(c * ncb, (c + 1) * ncb)
            acc = _dot_nt(wb_ref[sl, :], hh)
            cc = cb_ref[sl, :]
            y = acc + cc[:, 0:1] * phi + cc[:, 1:2] * plo + cc[:, 2:3]
            ob_ref[0, sl, :] = y.astype(ob_ref.dtype)


def _inproj(x, g, scl, shift, wa, cs, b0, pv, wb=None, cb=None):
    bsz, t, d = x.shape
    tt = min(ROW_TILE, t)
    na = wa.shape[1]
    has_b = wb is not None
    in_specs = [
        pl.BlockSpec((1, tt, d), lambda b, i: (b, i, 0)),
        _const_spec((1, d)),
        pl.BlockSpec((1, 1, d), lambda b, i: (b, 0, 0)),
        pl.BlockSpec((1, 1, d), lambda b, i: (b, 0, 0)),
        _const_spec((d, na)),
        _const_spec((1, na)),
        _const_spec((1, na)),
        _const_spec((1, na)),
    ]
    args = [x, g.reshape(1, d), scl, shift, wa, cs, b0, pv]
    out_specs = [pl.BlockSpec((1, tt, na), lambda b, i: (b, i, 0))]
    out_shape = [jax.ShapeDtypeStruct((bsz, t, na), BF16)]
    ncb = 512
    if has_b:
        nb = wb.shape[0]
        in_specs += [_const_spec((nb, d)), _const_spec((nb, 8))]
        args += [wb, cb]
        out_specs.append(pl.BlockSpec((1, nb, tt), lambda b, i: (b, 0, i)))
        out_shape.append(jax.ShapeDtypeStruct((bsz, nb, t), BF16))
    outs = pl.pallas_call(
        functools.partial(_inproj_kernel, tt=tt, nca=512, ncb=ncb, has_b=has_b),
        grid=(bsz, t // tt),
        in_specs=in_specs,
        out_specs=out_specs,
        out_shape=out_shape,
        compiler_params=_cparams(("arbitrary", "arbitrary")),
        name="mixer_inproj",
    )(*args)
    return outs if has_b else (outs[0], None)


def _outproj_kernel(o_ref, w_ref, x_ref, gate_ref, g2_ref, out_ref):
    y = _dot(o_ref[0], w_ref[...])
    out_ref[0] = x_ref[0] + gate_ref[0] * _rms(y, g2_ref[...])


def _outproj(o, w, x, gate, g2):
    bsz, t, d = x.shape
    tt = min(ROW_TILE, t)
    k = o.shape[2]
    return pl.pallas_call(
        _outproj_kernel,
        grid=(bsz, t // tt),
        in_specs=[
            pl.BlockSpec((1, tt, k), lambda b, i: (b, i, 0)),
            _const_spec((k, d)),
            pl.BlockSpec((1, tt, d), lambda b, i: (b, i, 0)),
            pl.BlockSpec((1, 1, d), lambda b, i: (b, 0, 0)),
            _const_spec((1, d)),
        ],
        out_specs=pl.BlockSpec((1, tt, d), lambda b, i: (b, i, 0)),
        out_shape=jax.ShapeDtypeStruct((bsz, t, d), F32),
        compiler_params=_cparams(("arbitrary", "arbitrary")),
        name="mixer_outproj",
    )(o, w, x, gate, g2.reshape(1, d))


def _ffn_kernel(x_ref, g1_ref, scl_ref, sh_ref, gate_ref, g2_ref, w1_ref, w2_ref, o_ref, acc_ref):
    x = x_ref[0]
    hh = (_rms(x, g1_ref[...]) * (1.0 + scl_ref[0]) + sh_ref[0]).astype(BF16)
    dff = w2_ref.shape[0]
    for c in range(dff // FF_CHUNK):
        lo = c * FF_CHUNK
        gt = _dot(hh, w1_ref[:, lo:lo + FF_CHUNK])
        up = _dot(hh, w1_ref[:, dff + lo:dff + lo + FF_CHUNK])
        act = (gt * jax.nn.sigmoid(gt) * up).astype(BF16)
        part = _dot(act, w2_ref[lo:lo + FF_CHUNK, :])
        if c == 0:
            acc_ref[...] = part
        else:
            acc_ref[...] += part
    o_ref[0] = x + 0.5 * gate_ref[0] * _rms(acc_ref[...], g2_ref[...])


def _ffn(x, g1, scl, shift, gate, g2, w1, w2):
    bsz, t, d = x.shape
    tt = min(ROW_TILE, t)
    vec = pl.BlockSpec((1, 1, d), lambda b, i: (b, 0, 0))
    return pl.pallas_call(
        _ffn_kernel,
        grid=(bsz, t // tt),
        in_specs=[
            pl.BlockSpec((1, tt, d), lambda b, i: (b, i, 0)),
            _const_spec((1, d)), vec, vec, vec, _const_spec((1, d)),
            _const_spec(w1.shape), _const_spec(w2.shape),
        ],
        out_specs=pl.BlockSpec((1, tt, d), lambda b, i: (b, i, 0)),
        out_shape=jax.ShapeDtypeStruct((bsz, t, d), F32),
        scratch_shapes=[pltpu.VMEM((tt, d), F32)],
        compiler_params=_cparams(("arbitrary", "arbitrary")),
        name="ffn",
    )(x, g1.reshape(1, d), scl, shift, gate, g2.reshape(1, d), w1, w2)


def _online_step(s, m, l, acc, v):
    m_new = jnp.maximum(m, jnp.max(s, axis=-1, keepdims=True))
    alpha = jnp.exp2(m - m_new)
    p = jnp.exp2(s - m_new)
    l = alpha * l + jnp.sum(p, axis=-1, keepdims=True)
    acc = alpha * acc + _dot(p.astype(BF16), v)
    return m_new, l, acc


def _softmax_init(rows, width):
    return (jnp.full((rows, 1), NEG, F32), jnp.zeros((rows, 1), F32), jnp.zeros((rows, width), F32))


def _gelu_tanh(x):
    return 0.5 * x * (1.0 + jnp.tanh(math.sqrt(2.0 / math.pi) * (x + 0.044715 * (x * x * x))))


def _cmp_kernel(xk_ref, xv_ref, pe_ref, w1_ref, w2kt_ref, w2v_ref, kt_ref, v_ref):
    nc = xk_ref.shape[2]
    half = xk_ref.shape[3]

    def hidden(x_ref, j):
        x = x_ref[0, 0].astype(F32)
        top = _dot((x + pe_ref[j, 0:1, :]).astype(BF16), w1_ref[j, 0:half, :])
        bot = _dot((x + pe_ref[j, 1:2, :]).astype(BF16), w1_ref[j, half:2 * half, :])
        return _gelu_tanh(top + pltpu.roll(bot, nc - 1, axis=0)).astype(BF16)

    hk = hidden(xk_ref, 0)
    hv = hidden(xv_ref, 1)
    kt_ref[0, 0, 0:NSA_DK, :] = _dot_nt(w2kt_ref[...], hk).astype(BF16)
    n = lax.broadcasted_iota(jnp.int32, (1, nc), 1)
    end = n * CMP_STRIDE + (CMP_LEN - 1)
    phi = ((end >> 7) << 7).astype(F32)
    plo = (end & 127).astype(F32)
    r = lax.broadcasted_iota(jnp.int32, (LANES - NSA_DK, 1), 0)
    aug = jnp.where(r < 3, phi, jnp.where(r < 6, plo, jnp.where(r == 6, 1.0, 0.0)))
    kt_ref[0, 0, NSA_DK:LANES, :] = aug.astype(BF16)
    v_ref[0, 0] = _dot(hv, w2v_ref[...]).astype(BF16)


def _nsa_compress(xk, xv, pe, w1, w2kt, w2v):
    bsz, g, nc, half = xk.shape
    xspec = pl.BlockSpec((1, 1, nc, half), lambda b, j: (b, j, 0, 0))
    return pl.pallas_call(
        _cmp_kernel,
        grid=(bsz, g),
        in_specs=[xspec, xspec, _const_spec(pe.shape), _const_spec(w1.shape), _const_spec(w2kt.shape),
                  _const_spec(w2v.shape)],
        out_specs=[pl.BlockSpec((1, 1, LANES, nc), lambda b, j: (b, j, 0, 0)),
                   pl.BlockSpec((1, 1, nc, LANES), lambda b, j: (b, j, 0, 0))],
        out_shape=[jax.ShapeDtypeStruct((bsz, g, LANES, nc), BF16),
                   jax.ShapeDtypeStruct((bsz, g, nc, LANES), BF16)],
        compiler_params=_cparams(("arbitrary", "arbitrary")),
        name="nsa_compress",
    )(xk, xv, pe, w1, w2kt, w2v)


def _split3(x):
    hi = x.astype(BF16)
    r1 = x - hi.astype(F32)
    mid = r1.astype(BF16)
    lo = (r1 - mid.astype(F32)).astype(BF16)
    return hi, mid, lo


def _nsa_attn_kernel(qa_ref, kct_ref, vc_ref, kst_ref, vs_ref, kwt_ref, vw_ref, gt_ref, e_ref, ov_ref,
                     o_ref, tiles_ref, *, tq, tk):
    i = pl.program_id(2)
    q0 = i * tq
    rows = NSA_REP * tq
    qa = qa_ref[0]
    qs = jnp.concatenate([qa[:, r * LANES:(r + 1) * LANES] for r in range(NSA_REP)], axis=0)
    t_row = q0 + (lax.broadcasted_iota(jnp.int32, (rows, 1), 0) & (tq - 1))

    nc = kct_ref.shape[3]
    s = _dot(qs, kct_ref[0, 0])
    cend = lax.broadcasted_iota(jnp.int32, (1, nc), 1) * CMP_STRIDE + (CMP_LEN - 1)
    valid = cend <= t_row
    s = jnp.where(valid, s, NEG)
    e = jnp.where(valid, jnp.exp2(s - jnp.max(s, axis=-1, keepdims=True)), 0.0)
    l = jnp.sum(e, axis=-1, keepdims=True)
    p = e / jnp.where(l > 0.0, l, 1.0)
    o_cmp = _dot(p.astype(BF16), vc_ref[0, 0])

    p4 = p[0:tq]
    for r in range(1, NSA_REP):
        p4 = p4 + p[r * tq:(r + 1) * tq]
    ov = ov_ref[...]
    imp = sum(_dot_nt(ov, part) for part in _split3(p4))

    wk = WIN + tq
    st = pl.multiple_of(jnp.maximum(q0 - WIN, 0), LANES)
    col = st + lax.broadcasted_iota(jnp.int32, (1, wk), 1)
    valid = (col <= t_row) & (col > t_row - WIN)
    s = jnp.where(valid, _dot(qs, kwt_ref[0, :, pl.ds(st, wk)]), NEG)
    e = jnp.exp2(s - jnp.max(s, axis=-1, keepdims=True))
    o_win = _dot(e.astype(BF16), vw_ref[0, pl.ds(st, wk), :]) / jnp.sum(e, axis=-1, keepdims=True)

    cur = (q0 + lax.broadcasted_iota(jnp.int32, (1, tq), 1)) >> 6
    sid = lax.broadcasted_iota(jnp.int32, (N_SEL_PAD, 1), 0)
    forced = (sid == 0) | (sid == cur) | (sid == cur - 1)
    work = jnp.where(forced | (sid > cur), -3.0e38, imp)
    selb = jnp.where(forced, 0.0, NEG)
    for _ in range(SEL_TOPK - 3):
        mx = jnp.max(work, axis=0, keepdims=True)
        first = jnp.min(jnp.where(work == mx, sid, N_SEL_PAD), axis=0, keepdims=True)
        pick = sid == first
        selb = jnp.where(pick, 0.0, selb)
        work = jnp.where(pick, -3.0e38, work)
    selb = selb.T

    jd = q0 // tk
    n_tiles = kst_ref.shape[2] // tk
    any_sel = jnp.max(jnp.where(selb == 0.0, 1.0, 0.0), axis=0, keepdims=True)
    blk_tile = (lax.broadcasted_iota(jnp.int32, (N_SEL_PAD, LANES), 0) // (tk // SEL_LEN)
                == lax.broadcasted_iota(jnp.int32, (N_SEL_PAD, LANES), 1)).astype(BF16)
    tile_cnt = _dot(jnp.broadcast_to(any_sel, (16, N_SEL_PAD)).astype(BF16), blk_tile)
    n_act = jnp.int32(0)
    for j in range(n_tiles):
        tiles_ref[n_act] = j
        n_act = n_act + ((tile_cnt[0, j] > 0.0) & (j < jd)).astype(jnp.int32)

    selb4 = jnp.concatenate([selb.astype(BF16)] * NSA_REP, axis=0)
    qsel = jnp.concatenate([qs, selb4], axis=1)

    def sel_scores(st):
        return _dot(qsel, jnp.concatenate([kst_ref[0, :, pl.ds(st, tk)], e_ref[:, pl.ds(st, tk)]], axis=0))

    def sel_body(it, carry):
        st = pl.multiple_of(tiles_ref[it] * tk, tk)
        return _online_step(sel_scores(st), *carry, vs_ref[0, pl.ds(st, tk), :])

    carry = lax.fori_loop(0, n_act, sel_body, _softmax_init(rows, LANES))
    st = pl.multiple_of(jd * tk, tk)
    col = st + lax.broadcasted_iota(jnp.int32, (1, tk), 1)
    s = jnp.where(col <= t_row, sel_scores(st), NEG)
    _, l, acc = _online_step(s, *carry, vs_ref[0, pl.ds(st, tk), :])
    o_sel = acc / l

    gs = jax.nn.sigmoid(gt_ref[0].astype(F32))
    outs = []
    for r in range(NSA_REP):
        sl = slice(r * tq, (r + 1) * tq)
        outs.append(gs[:, 3 * r:3 * r + 1] * o_cmp[sl] + gs[:, 3 * r + 1:3 * r + 2] * o_sel[sl]
                    + gs[:, 3 * r + 2:3 * r + 3] * o_win[sl])
    pairs = [outs[2 * a] + pltpu.roll(outs[2 * a + 1], NSA_DK, axis=1) for a in range(NSA_REP // 2)]
    o_ref[0] = jnp.concatenate(pairs, axis=1).astype(o_ref.dtype)


def _nsa_attention(oa, ob, kct, vcp, e_mat, ov_mat, t):
    bsz = oa.shape[0]
    tq, tk = NSA_TQ, NSA_TK
    qw = NSA_REP * LANES
    vs_blk = (NSA_HEADS * LANES + 2 * NSA_KVW) // LANES
    vw_blk = vs_blk + NSA_KV
    gt_blk = vw_blk + NSA_KV
    nc = kct.shape[3]
    return pl.pallas_call(
        functools.partial(_nsa_attn_kernel, tq=tq, tk=tk),
        grid=(bsz, NSA_KV, t // tq),
        in_specs=[
            pl.BlockSpec((1, tq, qw), lambda b, g, i: (b, i, g)),
            pl.BlockSpec((1, 1, LANES, nc), lambda b, g, i: (b, g, 0, 0)),
            pl.BlockSpec((1, 1, nc, LANES), lambda b, g, i: (b, g, 0, 0)),
            pl.BlockSpec((1, LANES, t), lambda b, g, i: (b, g, 0)),
            pl.BlockSpec((1, t, LANES), lambda b, g, i: (b, 0, vs_blk + g)),
            pl.BlockSpec((1, LANES, t), lambda b, g, i: (b, NSA_KV + g, 0)),
            pl.BlockSpec((1, t, LANES), lambda b, g, i: (b, 0, vw_blk + g)),
            pl.BlockSpec((1, tq, LANES), lambda b, g, i: (b, i, gt_blk + g)),
            _const_spec(e_mat.shape),
            _const_spec(ov_mat.shape),
        ],
        out_specs=pl.BlockSpec((1, tq, NSA_REP * NSA_DK), lambda b, g, i: (b, i, g)),
        out_shape=jax.ShapeDtypeStruct((bsz, t, NSA_QW), BF16),
        scratch_shapes=[pltpu.SMEM((t // tk,), jnp.int32)],
        compiler_params=_cparams(("arbitrary", "arbitrary", "arbitrary")),
        name="nsa_attention",
    )(oa, kct, vcp, ob, oa, ob, oa, oa, e_mat, ov_mat)


def _sb_kernel(q_ref, kt_ref, v_ref, o_ref, rem_ref, acc_ref, *, tq, tk):
    i = pl.program_id(2)
    q0 = i * tq
    q = q_ref[0]
    lane = lax.broadcasted_iota(jnp.int32, (1, LANES), 1)
    zero = jnp.zeros_like(q)
    qs = jnp.concatenate([jnp.where(lane < SB_DH, q, zero), jnp.where(lane >= SB_DH, q, zero)], axis=0)
    t_row = q0 + (lax.broadcasted_iota(jnp.int32, (2 * tq, 1), 0) & (tq - 1))
    r_id = lax.broadcasted_iota(jnp.int32, (2 * tk, tk), 0) & (tk - 1)
    upper = (r_id >= lax.broadcasted_iota(jnp.int32, (2 * tk, tk), 1)).astype(BF16)
    rem_ref[...] = jnp.zeros(rem_ref.shape, F32)
    acc_ref[...] = jnp.zeros(acc_ref.shape, F32)

    def tiles(js, masked):
        sts = [pl.multiple_of(j * tk, tk) for j in js]
        zs = [_dot(qs, kt_ref[0, :, pl.ds(st, tk)]) for st in sts]
        hilos, befores = [], []
        for st, z in zip(sts, zs):
            cost = jnp.maximum(z, 0.0) + jnp.log(1.0 + jnp.exp2(-jnp.abs(z))) * LOG2E
            before = None
            if masked:
                before = (st + lax.broadcasted_iota(jnp.int32, (1, tk), 1)) < t_row
                cost = jnp.where(before, cost, 0.0)
            hi = cost.astype(BF16)
            lo = (cost - hi.astype(F32)).astype(BF16)
            hilos.append(jnp.concatenate([hi, lo], axis=1))
            befores.append(before)
        rems = [_dot(hl, upper) for hl in hilos]
        rem_right = rem_ref[...]
        probs = []
        for z, rem, before in zip(zs, rems, befores):
            ex = z - (rem + rem_right)
            if masked:
                ex = jnp.where(before, ex, NEG)
            probs.append(jnp.exp2(ex).astype(BF16))
            rem_right = rem_right + rem[:, 0:1]
        rem_ref[...] = rem_right
        acc = acc_ref[...]
        for st, p in zip(sts, probs):
            acc = acc + _dot(p, v_ref[0, pl.ds(st, tk), :])
        acc_ref[...] = acc

    per_step = tq // tk
    n_full = q0 // tk
    tiles([n_full + per_step - 1 - d for d in range(per_step)], True)

    def body(it, _):
        tiles([n_full - 1 - per_step * it - d for d in range(per_step)], False)
        return 0

    lax.fori_loop(0, n_full // per_step, body, 0)
    acc = acc_ref[...]
    o_ref[0] = jnp.where(lane < SB_DH, acc[0:tq], acc[tq:2 * tq]).astype(o_ref.dtype)


def _sb_attention(oa, ob, t):
    bsz = oa.shape[0]
    tq, tk = SB_TQ, SB_TK
    npair = SB_HEADS // 2
    return pl.pallas_call(
        functools.partial(_sb_kernel, tq=tq, tk=tk),
        grid=(bsz, npair, t // tq),
        in_specs=[
            pl.BlockSpec((1, tq, LANES), lambda b, h, i: (b, i, h)),
            pl.BlockSpec((1, LANES, t), lambda b, h, i: (b, h, 0)),
            pl.BlockSpec((1, t, LANES), lambda b, h, i: (b, 0, npair + h)),
        ],
        out_specs=pl.BlockSpec((1, tq, LANES), lambda b, h, i: (b, i, h)),
        out_shape=jax.ShapeDtypeStruct((bsz, t, D_MODEL), BF16),
        scratch_shapes=[pltpu.VMEM((2 * tq, 1), F32), pltpu.VMEM((2 * tq, LANES), F32)],
        compiler_params=_cparams(("arbitrary", "arbitrary", "arbitrary")),
        name="sb_attention",
    )(oa, ob, oa)


def _diff_kernel(lam_ref, g_ref, qa_ref, kt_ref, v_ref, o_ref, *, tq, tk, lam_init):
    i = pl.program_id(2)
    q0 = i * tq
    qa = qa_ref[0]
    lane = lax.broadcasted_iota(jnp.int32, (1, 2 * LANES), 1)
    zero = jnp.zeros_like(qa)
    qs = jnp.concatenate(
        [jnp.where(((lane >= m_ * DIFF_DH) & (lane < (m_ + 1) * DIFF_DH)) | (lane >= LANES), qa, zero)
         for m_ in range(2)], axis=0)
    t_row = q0 + (lax.broadcasted_iota(jnp.int32, (2 * tq, 1), 0) & (tq - 1))

    def body(j, carry):
        st = pl.multiple_of(j * tk, tk)
        return _online_step(_dot(qs, kt_ref[0, :, pl.ds(st, tk)]), *carry, v_ref[0, pl.ds(st, tk), :])

    jd = q0 // tk
    carry = lax.fori_loop(0, jd, body, _softmax_init(2 * tq, LANES))
    st = pl.multiple_of(jd * tk, tk)
    col = st + lax.broadcasted_iota(jnp.int32, (1, tk), 1)
    s = jnp.where(col <= t_row, _dot(qs, kt_ref[0, :, pl.ds(st, tk)]), NEG)
    _, l, acc = _online_step(s, *carry, v_ref[0, pl.ds(st, tk), :])
    out = acc / l
    lam = lam_ref[...]
    lam_full = (jnp.exp(jnp.sum(lam[0:1] * lam[1:2], axis=-1, keepdims=True))
                - jnp.exp(jnp.sum(lam[2:3] * lam[3:4], axis=-1, keepdims=True)) + lam_init)
    o = out[0:tq] - lam_full * out[tq:2 * tq]
    o_ref[0] = (_rms(o, g_ref[...]) * (1.0 - lam_init)).astype(o_ref.dtype)


def _diff_attention(oa, ob, lam, subln_g, lam_init, t):
    bsz = oa.shape[0]
    tq, tk = DIFF_TQ, DIFF_TK
    v_blk = DIFF_HEADS * 2
    return pl.pallas_call(
        functools.partial(_diff_kernel, tq=tq, tk=tk, lam_init=lam_init),
        grid=(bsz, DIFF_HEADS, t // tq),
        in_specs=[
            _const_spec(lam.shape),
            _const_spec((1, 2 * DIFF_DH)),
            pl.BlockSpec((1, tq, 2 * LANES), lambda b, h, i: (b, i, h)),
            pl.BlockSpec((1, 2 * LANES, t), lambda b, h, i: (b, h, 0)),
            pl.BlockSpec((1, t, LANES), lambda b, h, i: (b, 0, v_blk + h)),
        ],
        out_specs=pl.BlockSpec((1, tq, LANES), lambda b, h, i: (b, i, h)),
        out_shape=jax.ShapeDtypeStruct((bsz, t, D_MODEL), BF16),
        compiler_params=_cparams(("arbitrary", "arbitrary", "arbitrary")),
        name="diff_attention",
    )(lam, subln_g.reshape(1, 2 * DIFF_DH), oa, ob, oa)


def _alibi_slopes(n):
    return jnp.exp2(-8.0 * jnp.arange(1, n + 1, dtype=F32) / n)


def _alibi_query_cols(slopes, width, first):
    n = slopes.shape[0]
    sl2 = slopes * LOG2E
    parts = jnp.stack([p.astype(F32) for p in _split3(sl2)], axis=1)
    b0 = jnp.zeros((n, width), F32)
    b0 = b0.at[:, first:first + 3].set(parts).at[:, first + 3:first + 6].set(parts)
    pv = jnp.zeros((n, width), F32).at[:, first + 6].set(-sl2)
    return b0.reshape(1, n * width), pv.reshape(1, n * width)


def _alibi_key_rows(nblocks, width, first):
    c = jnp.zeros((nblocks, width, 8), F32)
    c = c.at[:, first:first + 3, 0].set(1.0).at[:, first + 3:first + 6, 1].set(1.0).at[:, first + 6, 2].set(1.0)
    return c.reshape(nblocks * width, 8)


def _pad_blocks(w, nblocks, width):
    d = w.shape[0]
    k = w.shape[1] // nblocks
    return jnp.pad(w.reshape(d, nblocks, k), ((0, 0), (0, 0), (0, width - k))).reshape(d, nblocks * width)


def _nsa_weights(w_in):
    d = w_in.shape[0]
    bounds = [0, NSA_QW] + [NSA_QW + (k + 1) * NSA_KVW for k in range(6)]
    wq, wkc, wvc, wks, wvs, wkw, wvw = [w_in[:, bounds[k]:bounds[k + 1]] for k in range(7)]
    wg = w_in[:, bounds[7]:]
    wa = jnp.concatenate([
        _pad_blocks(wq, NSA_HEADS, LANES), wkc, wvc, _pad_blocks(wvs, NSA_KV, LANES),
        _pad_blocks(wvw, NSA_KV, LANES), _pad_blocks(wg, NSA_KV, LANES)], axis=1).astype(BF16)
    na = wa.shape[1]
    qcols = NSA_HEADS * LANES
    scale = NSA_DK ** -0.5 * LOG2E
    cs_q = jnp.where(jnp.arange(LANES) < NSA_DK, scale, 1.0).astype(F32)
    cs = jnp.concatenate([jnp.tile(cs_q, NSA_HEADS), jnp.ones((na - qcols,), F32)]).reshape(1, na)
    b0q, pvq = _alibi_query_cols(_alibi_slopes(NSA_HEADS), LANES, NSA_DK)
    b0 = jnp.pad(b0q, ((0, 0), (0, na - qcols)))
    pv = jnp.pad(pvq, ((0, 0), (0, na - qcols)))
    wb = jnp.concatenate([_pad_blocks(wks, NSA_KV, LANES), _pad_blocks(wkw, NSA_KV, LANES)], axis=1).T
    cb = _alibi_key_rows(2 * NSA_KV, LANES, NSA_DK)
    return wa, cs, b0, pv, wb.astype(BF16), cb


def _sb_weights(w_in):
    d = w_in.shape[0]
    wq, wk, wv = jnp.split(w_in, 3, axis=1)
    wa = jnp.concatenate([wq, wv], axis=1).astype(BF16)
    na = wa.shape[1]
    cs = jnp.concatenate([jnp.full((d,), SB_DH ** -0.5 * LOG2E, F32), jnp.ones((d,), F32)]).reshape(1, na)
    zero = jnp.zeros((1, na), F32)
    return wa, cs, zero, zero, wk.T.astype(BF16), jnp.zeros((d, 8), F32)


def _diff_weights(w_in):
    d = w_in.shape[0]
    wq, wk, wv = jnp.split(w_in, 3, axis=1)
    width = 2 * LANES
    wa = jnp.concatenate([_pad_blocks(wq, DIFF_HEADS, width), wv], axis=1).astype(BF16)
    na = wa.shape[1]
    qcols = DIFF_HEADS * width
    scale = DIFF_DH ** -0.5 * LOG2E
    cs_q = jnp.where(jnp.arange(width) < 2 * DIFF_DH, scale, 1.0).astype(F32)
    cs = jnp.concatenate([jnp.tile(cs_q, DIFF_HEADS), jnp.ones((na - qcols,), F32)]).reshape(1, na)
    b0q, pvq = _alibi_query_cols(_alibi_slopes(DIFF_HEADS), width, 2 * DIFF_DH)
    b0 = jnp.pad(b0q, ((0, 0), (0, na - qcols)))
    pv = jnp.pad(pvq, ((0, 0), (0, na - qcols)))
    wb = _pad_blocks(wk, DIFF_HEADS, width).T.astype(BF16)
    cb = _alibi_key_rows(DIFF_HEADS, width, 2 * DIFF_DH)
    return wa, cs, b0, pv, wb, cb


def _nsa_mixer(x, g, scl, shift, w_in, cmp_pe, cmp_w1, cmp_w2):
    bsz, t, _ = x.shape
    wa, cs, b0, pv, wb, cb = _nsa_weights(w_in)
    oa, ob = _inproj(x, g, scl, shift, wa, cs, b0, pv, wb, cb)
    nc = t // CMP_STRIDE
    kc0 = NSA_HEADS * LANES

    def grouped(lo):
        a = oa[:, :, lo:lo + NSA_KVW].reshape(bsz, nc, CMP_STRIDE, NSA_KV, NSA_DK)
        return jnp.moveaxis(a, 3, 1).reshape(bsz, NSA_KV, nc, CMP_STRIDE * NSA_DK)

    pe = cmp_pe.reshape(2, 2, CMP_STRIDE * NSA_DK)
    w2kt = cmp_w2[0].T.astype(BF16)
    w2v = jnp.pad(cmp_w2[1], ((0, 0), (0, LANES - NSA_DK))).astype(BF16)
    kct, vcp = _nsa_compress(grouped(kc0), grouped(kc0 + NSA_KVW), pe, cmp_w1.astype(BF16), w2kt, w2v)
    tok = jnp.arange(t)
    e_mat = (jnp.arange(N_SEL_PAD)[:, None] == (tok // SEL_LEN)[None, :]).astype(BF16)
    cstart = jnp.arange(nc) * CMP_STRIDE
    sstart = jnp.arange(N_SEL_PAD) * SEL_LEN
    ov_mat = ((cstart[None, :] < sstart[:, None] + SEL_LEN)
              & (cstart[None, :] + CMP_LEN > sstart[:, None])
              & (jnp.arange(nc)[None, :] < nc - 1)).astype(BF16)
    return _nsa_attention(oa, ob, kct, vcp, e_mat, ov_mat, t)


def _sb_mixer(x, g, scl, shift, w_in):
    t = x.shape[1]
    wa, cs, b0, pv, wb, cb = _sb_weights(w_in)
    oa, ob = _inproj(x, g, scl, shift, wa, cs, b0, pv, wb, cb)
    return _sb_attention(oa, ob, t)


def _diff_mixer(x, g, scl, shift, w_in, lam, subln_g, layer_idx):
    t = x.shape[1]
    wa, cs, b0, pv, wb, cb = _diff_weights(w_in)
    oa, ob = _inproj(x, g, scl, shift, wa, cs, b0, pv, wb, cb)
    lam_init = 0.8 - 0.6 * math.exp(-0.3 * layer_idx)
    return _diff_attention(oa, ob, lam, subln_g, lam_init, t)


def kernel(x, c, ada_w, ada_b, norm_g, ffn_w1, ffn_w2, nsa_w_in, nsa_cmp_pe, nsa_cmp_w1, nsa_cmp_w2,
           nsa_w_out, sb_w_in, sb_w_out, diff_w_in, diff_lam, diff_subln_g, diff_w_out):
    bsz, t, d = x.shape
    assert d == D_MODEL and t % ROW_TILE == 0 and t >= WIN + 128 and t <= N_SEL_PAD * SEL_LEN
    mod = _modulation(c, ada_w, ada_b)
    for i in range(DEPTH):
        def mods(sidx):
            return tuple(mod[i, :, sidx, k][:, None, :] for k in range(3))

        shift, scl, gate = mods(0)
        x = _ffn(x, norm_g[i, 0], scl, shift, gate, norm_g[i, 1],
                 ffn_w1[i, 0].astype(BF16), ffn_w2[i, 0].astype(BF16))
        shift, scl, gate = mods(1)
        kind, j = i % N_MIXERS, i // N_MIXERS
        if kind == 0:
            o = _nsa_mixer(x, norm_g[i, 2], scl, shift, nsa_w_in[j], nsa_cmp_pe[j], nsa_cmp_w1[j],
                           nsa_cmp_w2[j])
            w_out = nsa_w_out[j]
        elif kind == 1:
            o = _sb_mixer(x, norm_g[i, 2], scl, shift, sb_w_in[j])
            w_out = sb_w_out[j]
        else:
            o = _diff_mixer(x, norm_g[i, 2], scl, shift, diff_w_in[j], diff_lam[j], diff_subln_g[j], i)
            w_out = diff_w_out[j]
        x = _outproj(o, w_out.astype(BF16), x, gate, norm_g[i, 3])
        shift, scl, gate = mods(2)
        x = _ffn(x, norm_g[i, 4], scl, shift, gate, norm_g[i, 5],
                 ffn_w1[i, 1].astype(BF16), ffn_w2[i, 1].astype(BF16))
    return x
```

```python
import functools
import math

import jax
import jax.numpy as jnp
from jax import lax
from jax.experimental import pallas as pl
from jax.experimental.pallas import tpu as pltpu

F32 = jnp.float32
BF16 = jnp.bfloat16

D_MODEL = 1024
DEPTH = 4
N_MIXERS = 3
EPS = 1e-6
NEG = -1e30
FORCE = 1e4
LOG2E = 1.4426950408889634
NSA_HEADS = 16
NSA_DK = 64
NSA_KV = 4
NSA_REP = NSA_HEADS // NSA_KV
CMP_LEN = 32
CMP_STRIDE = 16
CMP_HID = 256
SEL_LEN = 64
SEL_TOPK = 16
WIN = 512
NSA_QW = NSA_HEADS * NSA_DK
NSA_KVW = NSA_KV * NSA_DK
N_SEL_PAD = 128
SB_HEADS = 16
SB_DH = D_MODEL // SB_HEADS
DIFF_HEADS = 8
DIFF_DH = D_MODEL // (2 * DIFF_HEADS)
D_FF = 2816

LANES = 128
ROW_TILE = 512
FF_CHUNK = 256
DIFF_TQ, DIFF_TK = 512, 512
SB_TQ, SB_TK = 512, 256
SB_LINEAR_FROM = 64.0
SB_DEAD = 192.0
NSA_TQ, NSA_TK = 256, 512
VMEM_LIMIT = 56 * 1024 * 1024


def _cparams(sem):
    return pltpu.CompilerParams(dimension_semantics=sem, vmem_limit_bytes=VMEM_LIMIT)


def _const_spec(shape):
    n = len(shape)
    return pl.BlockSpec(shape, lambda *_: (0,) * n)


def _rms(x, g):
    return x * lax.rsqrt(jnp.mean(x * x, axis=-1, keepdims=True) + EPS) * g


def _dot(a, b):
    return jnp.dot(a, b, preferred_element_type=F32)


def _dot_nt(a, b):
    return lax.dot_general(a, b, (((1,), (1,)), ((), ())), preferred_element_type=F32)


def _mod_kernel(c_ref, w_ref, b_ref, o_ref):
    c = c_ref[...]
    cond = c * jax.nn.sigmoid(c)
    o_ref[0] = _dot(cond.astype(BF16), w_ref[0].astype(BF16)) + b_ref[0]


def _modulation(c, ada_w, ada_b):
    depth, d, n = ada_w.shape
    b = c.shape[0]
    rows = 8
    c_pad = jnp.zeros((rows, d), F32).at[:b].set(c)
    nc = 1152
    out = pl.pallas_call(
        _mod_kernel,
        grid=(depth, n // nc),
        in_specs=[
            pl.BlockSpec((rows, d), lambda i, j: (0, 0)),
            pl.BlockSpec((1, d, nc), lambda i, j: (i, 0, j)),
            pl.BlockSpec((1, 1, nc), lambda i, j: (i, 0, j)),
        ],
        out_specs=pl.BlockSpec((1, rows, nc), lambda i, j: (i, 0, j)),
        out_shape=jax.ShapeDtypeStruct((depth, rows, n), F32),
        compiler_params=_cparams(("arbitrary", "arbitrary")),
        name="adaln_mod",
    )(c_pad, ada_w, ada_b.reshape(depth, 1, n))
    return out[:, :b].reshape(depth, b, 3, 3, d)


def _inproj_kernel(*refs, tt, nca, ncb, has_b):
    if has_b:
        (x_ref, g_ref, scl_ref, sh_ref, wa_ref, cs_ref, b0_ref, pv_ref, wb_ref, cb_ref,
         oa_ref, ob_ref) = refs
    else:
        x_ref, g_ref, scl_ref, sh_ref, wa_ref, cs_ref, b0_ref, pv_ref, oa_ref = refs
    x = x_ref[0]
    hh = (_rms(x, g_ref[...]) * (1.0 + scl_ref[0]) + sh_ref[0]).astype(BF16)
    t0 = pl.program_id(1) * tt
    pos = (t0 + lax.broadcasted_iota(jnp.int32, (tt, 1), 0)).astype(F32)
    na = wa_ref.shape[1]
    for c in range(na // nca):
        sl = slice(c * nca, (c + 1) * nca)
        acc = _dot(hh, wa_ref[:, sl])
        y = acc * cs_ref[:, sl] + b0_ref[:, sl] + pos * pv_ref[:, sl]
        oa_ref[0, :, sl] = y.astype(oa_ref.dtype)
    if has_b:
        tpos = t0 + lax.broadcasted_iota(jnp.int32, (1, tt), 1)
        phi = ((tpos >> 7) << 7).astype(F32)
        plo = (tpos & 127).astype(F32)
        nb = wb_ref.shape[0]
        for c in range(nb // ncb):
            sl = slice(c * ncb, (c + 1) * ncb)
            acc = _dot_nt(wb_ref[sl, :], hh)
            cc = cb_ref[sl, :]
            y = acc + cc[:, 0:1] * phi + cc[:, 1:2] * plo + cc[:, 2:3]
            ob_ref[0, sl, :] = y.astype(ob_ref.dtype)


def _inproj(x, g, scl, shift, wa, cs, b0, pv, wb=None, cb=None):
    bsz, t, d = x.shape
    tt = min(ROW_TILE, t)
    na = wa.shape[1]
    has_b = wb is not None
    in_specs = [
        pl.BlockSpec((1, tt, d), lambda b, i: (b, i, 0)),
        _const_spec((1, d)),
        pl.BlockSpec((1, 1, d), lambda b, i: (b, 0, 0)),
        pl.BlockSpec((1, 1, d), lambda b, i: (b, 0, 0)),
        _const_spec((d, na)),
        _const_spec((1, na)),
        _const_spec((1, na)),
        _const_spec((1, na)),
    ]
    args = [x, g.reshape(1, d), scl, shift, wa, cs, b0, pv]
    out_specs = [pl.BlockSpec((1, tt, na), lambda b, i: (b, i, 0))]
    out_shape = [jax.ShapeDtypeStruct((bsz, t, na), BF16)]
    ncb = 512
    if has_b:
        nb = wb.shape[0]
        in_specs += [_const_spec((nb, d)), _const_spec((nb, 8))]
        args += [wb, cb]
        out_specs.append(pl.BlockSpec((1, nb, tt), lambda b, i: (b, 0, i)))
        out_shape.append(jax.ShapeDtypeStruct((bsz, nb, t), BF16))
    outs = pl.pallas_call(
        functools.partial(_inproj_kernel, tt=tt, nca=512, ncb=ncb, has_b=has_b),
        grid=(bsz, t // tt),
        in_specs=in_specs,
        out_specs=out_specs,
        out_shape=out_shape,
        compiler_params=_cparams(("arbitrary", "arbitrary")),
        name="mixer_inproj",
    )(*args)
    return outs if has_b else (outs[0], None)


def _outproj_kernel(o_ref, w_ref, x_ref, gate_ref, g2_ref, out_ref):
    y = _dot(o_ref[0], w_ref[...])
    out_ref[0] = x_ref[0] + gate_ref[0] * _rms(y, g2_ref[...])


def _outproj(o, w, x, gate, g2):
    bsz, t, d = x.shape
    tt = min(ROW_TILE, t)
    k = o.shape[2]
    return pl.pallas_call(
        _outproj_kernel,
        grid=(bsz, t // tt),
        in_specs=[
            pl.BlockSpec((1, tt, k), lambda b, i: (b, i, 0)),
            _const_spec((k, d)),
            pl.BlockSpec((1, tt, d), lambda b, i: (b, i, 0)),
            pl.BlockSpec((1, 1, d), lambda b, i: (b, 0, 0)),
            _const_spec((1, d)),
        ],
        out_specs=pl.BlockSpec((1, tt, d), lambda b, i: (b, i, 0)),
        out_shape=jax.ShapeDtypeStruct((bsz, t, d), F32),
        compiler_params=_cparams(("arbitrary", "arbitrary")),
        name="mixer_outproj",
    )(o, w, x, gate, g2.reshape(1, d))


def _ffn_kernel(x_ref, g1_ref, scl_ref, sh_ref, gate_ref, g2_ref, w1_ref, w2_ref, o_ref, acc_ref):
    x = x_ref[0]
    hh = (_rms(x, g1_ref[...]) * (1.0 + scl_ref[0]) + sh_ref[0]).astype(BF16)
    dff = w2_ref.shape[0]
    for c in range(dff // FF_CHUNK):
        lo = c * FF_CHUNK
        gt = _dot(hh, w1_ref[:, lo:lo + FF_CHUNK])
        up = _dot(hh, w1_ref[:, dff + lo:dff + lo + FF_CHUNK])
        act = (gt * jax.nn.sigmoid(gt) * up).astype(BF16)
        part = _dot(act, w2_ref[lo:lo + FF_CHUNK, :])
        if c == 0:
            acc_ref[...] = part
        else:
            acc_ref[...] += part
    o_ref[0] = x + 0.5 * gate_ref[0] * _rms(acc_ref[...], g2_ref[...])


def _ffn(x, g1, scl, shift, gate, g2, w1, w2):
    bsz, t, d = x.shape
    tt = min(ROW_TILE, t)
    vec = pl.BlockSpec((1, 1, d), lambda b, i: (b, 0, 0))
    return pl.pallas_call(
        _ffn_kernel,
        grid=(bsz, t // tt),
        in_specs=[
            pl.BlockSpec((1, tt, d), lambda b, i: (b, i, 0)),
            _const_spec((1, d)), vec, vec, vec, _const_spec((1, d)),
            _const_spec(w1.shape), _const_spec(w2.shape),
        ],
        out_specs=pl.BlockSpec((1, tt, d), lambda b, i: (b, i, 0)),
        out_shape=jax.ShapeDtypeStruct((bsz, t, d), F32),
        scratch_shapes=[pltpu.VMEM((tt, d), F32)],
        compiler_params=_cparams(("arbitrary", "arbitrary")),
        name="ffn",
    )(x, g1.reshape(1, d), scl, shift, gate, g2.reshape(1, d), w1, w2)


def _online_step(s, m, l, acc, v):
    m_new = jnp.maximum(m, jnp.max(s, axis=-1, keepdims=True))
    alpha = jnp.exp2(m - m_new)
    p = jnp.exp2(s - m_new)
    l = alpha * l + jnp.sum(p, axis=-1, keepdims=True)
    acc = alpha * acc + _dot(p.astype(BF16), v)
    return m_new, l, acc


def _softmax_init(rows, width):
    return (jnp.full((rows, 1), NEG, F32), jnp.zeros((rows, 1), F32), jnp.zeros((rows, width), F32))


def _gelu_tanh(x):
    return 0.5 * x * (1.0 + jnp.tanh(math.sqrt(2.0 / math.pi) * (x + 0.044715 * (x * x * x))))


def _cmp_kernel(xk_ref, xv_ref, pe_ref, w1_ref, w2kt_ref, w2v_ref, kt_ref, v_ref):
    nc = xk_ref.shape[2]
    half = xk_ref.shape[3]

    def hidden(x_ref, j):
        x = x_ref[0, 0].astype(F32)
        top = _dot((x + pe_ref[j, 0:1, :]).astype(BF16), w1_ref[j, 0:half, :])
        bot = _dot((x + pe_ref[j, 1:2, :]).astype(BF16), w1_ref[j, half:2 * half, :])
        return _gelu_tanh(top + pltpu.roll(bot, nc - 1, axis=0)).astype(BF16)

    hk = hidden(xk_ref, 0)
    hv = hidden(xv_ref, 1)
    kt_ref[0, 0, 0:NSA_DK, :] = _dot_nt(w2kt_ref[...], hk).astype(BF16)
    n = lax.broadcasted_iota(jnp.int32, (1, nc), 1)
    end = n * CMP_STRIDE + (CMP_LEN - 1)
    phi = ((end >> 7) << 7).astype(F32)
    plo = (end & 127).astype(F32)
    r = lax.broadcasted_iota(jnp.int32, (LANES - NSA_DK, 1), 0)
    aug = jnp.where(r < 3, phi, jnp.where(r < 6, plo, jnp.where(r == 6, 1.0, 0.0)))
    kt_ref[0, 0, NSA_DK:LANES, :] = aug.astype(BF16)
    v_ref[0, 0] = _dot(hv, w2v_ref[...]).astype(BF16)


def _nsa_compress(xk, xv, pe, w1, w2kt, w2v):
    bsz, g, nc, half = xk.shape
    xspec = pl.BlockSpec((1, 1, nc, half), lambda b, j: (b, j, 0, 0))
    return pl.pallas_call(
        _cmp_kernel,
        grid=(bsz, g),
        in_specs=[xspec, xspec, _const_spec(pe.shape), _const_spec(w1.shape), _const_spec(w2kt.shape),
                  _const_spec(w2v.shape)],
        out_specs=[pl.BlockSpec((1, 1, LANES, nc), lambda b, j: (b, j, 0, 0)),
                   pl.BlockSpec((1, 1, nc, LANES), lambda b, j: (b, j, 0, 0))],
        out_shape=[jax.ShapeDtypeStruct((bsz, g, LANES, nc), BF16),
                   jax.ShapeDtypeStruct((bsz, g, nc, LANES), BF16)],
        compiler_params=_cparams(("arbitrary", "arbitrary")),
        name="nsa_compress",
    )(xk, xv, pe, w1, w2kt, w2v)


def _split3(x):
    hi = x.astype(BF16)
    r1 = x - hi.astype(F32)
    mid = r1.astype(BF16)
    lo = (r1 - mid.astype(F32)).astype(BF16)
    return hi, mid, lo


def _nsa_attn_kernel(qa_ref, kct_ref, vc_ref, kst_ref, vs_ref, kwt_ref, vw_ref, gt_ref, e_ref, ov_ref,
                     o_ref, tiles_ref, *, tq, tk):
    i = pl.program_id(2)
    q0 = i * tq
    rows = NSA_REP * tq
    qa = qa_ref[0]
    qs = jnp.concatenate([qa[:, r * LANES:(r + 1) * LANES] for r in range(NSA_REP)], axis=0)
    t_row = q0 + (lax.broadcasted_iota(jnp.int32, (rows, 1), 0) & (tq - 1))

    nc = kct_ref.shape[3]
    s = _dot(qs, kct_ref[0, 0])
    cend = lax.broadcasted_iota(jnp.int32, (1, nc), 1) * CMP_STRIDE + (CMP_LEN - 1)
    valid = cend <= t_row
    s = jnp.where(valid, s, NEG)
    e = jnp.where(valid, jnp.exp2(s - jnp.max(s, axis=-1, keepdims=True)), 0.0)
    l = jnp.sum(e, axis=-1, keepdims=True)
    p = e / jnp.where(l > 0.0, l, 1.0)
    o_cmp = _dot(p.astype(BF16), vc_ref[0, 0])

    p4 = p[0:tq]
    for r in range(1, NSA_REP):
        p4 = p4 + p[r * tq:(r + 1) * tq]
    ov = ov_ref[...]
    imp = sum(_dot_nt(ov, part) for part in _split3(p4))

    wk = WIN + tq
    st = pl.multiple_of(jnp.maximum(q0 - WIN, 0), LANES)
    col = st + lax.broadcasted_iota(jnp.int32, (1, wk), 1)
    valid = (col <= t_row) & (col > t_row - WIN)
    s = jnp.where(valid, _dot(qs, kwt_ref[0, :, pl.ds(st, wk)]), NEG)
    e = jnp.exp2(s - jnp.max(s, axis=-1, keepdims=True))
    o_win = _dot(e.astype(BF16), vw_ref[0, pl.ds(st, wk), :]) / jnp.sum(e, axis=-1, keepdims=True)

    cur = (q0 + lax.broadcasted_iota(jnp.int32, (1, tq), 1)) >> 6
    sid = lax.broadcasted_iota(jnp.int32, (N_SEL_PAD, 1), 0)
    forced = (sid == 0) | (sid == cur) | (sid == cur - 1)
    work = jnp.where(forced | (sid > cur), -3.0e38, imp)
    selb = jnp.where(forced, 0.0, NEG)
    for _ in range(SEL_TOPK - 3):
        mx = jnp.max(work, axis=0, keepdims=True)
        first = jnp.min(jnp.where(work == mx, sid, N_SEL_PAD), axis=0, keepdims=True)
        pick = sid == first
        selb = jnp.where(pick, 0.0, selb)
        work = jnp.where(pick, -3.0e38, work)
    selb = selb.T

    jd = q0 // tk
    n_tiles = kst_ref.shape[2] // tk
    any_sel = jnp.max(jnp.where(selb == 0.0, 1.0, 0.0), axis=0, keepdims=True)
    blk_tile = (lax.broadcasted_iota(jnp.int32, (N_SEL_PAD, LANES), 0) // (tk // SEL_LEN)
                == lax.broadcasted_iota(jnp.int32, (N_SEL_PAD, LANES), 1)).astype(BF16)
    tile_cnt = _dot(jnp.broadcast_to(any_sel, (16, N_SEL_PAD)).astype(BF16), blk_tile)
    n_act = jnp.int32(0)
    for j in range(n_tiles):
        tiles_ref[n_act] = j
        n_act = n_act + ((tile_cnt[0, j] > 0.0) & (j < jd)).astype(jnp.int32)

    selb4 = jnp.concatenate([selb.astype(BF16)] * NSA_REP, axis=0)
    qsel = jnp.concatenate([qs, selb4], axis=1)

    def sel_scores(st):
        return _dot(qsel, jnp.concatenate([kst_ref[0, :, pl.ds(st, tk)], e_ref[:, pl.ds(st, tk)]], axis=0))

    def sel_body(it, carry):
        st = pl.multiple_of(tiles_ref[it] * tk, tk)
        return _online_step(sel_scores(st), *carry, vs_ref[0, pl.ds(st, tk), :])

    carry = lax.fori_loop(0, n_act, sel_body, _softmax_init(rows, LANES))
    st = pl.multiple_of(jd * tk, tk)
    col = st + lax.broadcasted_iota(jnp.int32, (1, tk), 1)
    s = jnp.where(col <= t_row, sel_scores(st), NEG)
    _, l, acc = _online_step(s, *carry, vs_ref[0, pl.ds(st, tk), :])
    o_sel = acc / l

    gs = jax.nn.sigmoid(gt_ref[0].astype(F32))
    outs = []
    for r in range(NSA_REP):
        sl = slice(r * tq, (r + 1) * tq)
        outs.append(gs[:, 3 * r:3 * r + 1] * o_cmp[sl] + gs[:, 3 * r + 1:3 * r + 2] * o_sel[sl]
                    + gs[:, 3 * r + 2:3 * r + 3] * o_win[sl])
    pairs = [outs[2 * a] + pltpu.roll(outs[2 * a + 1], NSA_DK, axis=1) for a in range(NSA_REP // 2)]
    o_ref[0] = jnp.concatenate(pairs, axis=1).astype(o_ref.dtype)


def _nsa_attention(oa, ob, kct, vcp, e_mat, ov_mat, t):
    bsz = oa.shape[0]
    tq, tk = NSA_TQ, NSA_TK
    qw = NSA_REP * LANES
    vs_blk = (NSA_HEADS * LANES + 2 * NSA_KVW) // LANES
    vw_blk = vs_blk + NSA_KV
    gt_blk = vw_blk + NSA_KV
    nc = kct.shape[3]
    return pl.pallas_call(
        functools.partial(_nsa_attn_kernel, tq=tq, tk=tk),
        grid=(bsz, NSA_KV, t // tq),
        in_specs=[
            pl.BlockSpec((1, tq, qw), lambda b, g, i: (b, i, g)),
            pl.BlockSpec((1, 1, LANES, nc), lambda b, g, i: (b, g, 0, 0)),
            pl.BlockSpec((1, 1, nc, LANES), lambda b, g, i: (b, g, 0, 0)),
            pl.BlockSpec((1, LANES, t), lambda b, g, i: (b, g, 0)),
            pl.BlockSpec((1, t, LANES), lambda b, g, i: (b, 0, vs_blk + g)),
            pl.BlockSpec((1, LANES, t), lambda b, g, i: (b, NSA_KV + g, 0)),
            pl.BlockSpec((1, t, LANES), lambda b, g, i: (b, 0, vw_blk + g)),
            pl.BlockSpec((1, tq, LANES), lambda b, g, i: (b, i, gt_blk + g)),
            _const_spec(e_mat.shape),
            _const_spec(ov_mat.shape),
        ],
        out_specs=pl.BlockSpec((1, tq, NSA_REP * NSA_DK), lambda b, g, i: (b, i, g)),
        out_shape=jax.ShapeDtypeStruct((bsz, t, NSA_QW), BF16),
        scratch_shapes=[pltpu.SMEM((t // tk,), jnp.int32)],
        compiler_params=_cparams(("arbitrary", "arbitrary", "arbitrary")),
        name="nsa_attention",
    )(oa, kct, vcp, ob, oa, ob, oa, oa, e_mat, ov_mat)


def _sb_kernel(q_ref, kt_ref, v_ref, o_ref, rem_ref, acc_ref, *, tq, tk):
    i = pl.program_id(2)
    q0 = i * tq
    q = q_ref[0]
    lane = lax.broadcasted_iota(jnp.int32, (1, LANES), 1)
    zero = jnp.zeros_like(q)
    qs = jnp.concatenate([jnp.where(lane < SB_DH, q, zero), jnp.where(lane >= SB_DH, q, zero)], axis=0)
    t_row = q0 + (lax.broadcasted_iota(jnp.int32, (2 * tq, 1), 0) & (tq - 1))
    upper = (lax.broadcasted_iota(jnp.int32, (tk, tk), 0) > lax.broadcasted_iota(jnp.int32, (tk, tk), 1)
             ).astype(BF16)
    rem_ref[...] = jnp.zeros(rem_ref.shape, F32)
    acc_ref[...] = jnp.zeros(acc_ref.shape, F32)

    def tiles(js, masked):
        sts = [pl.multiple_of(j * tk, tk) for j in js]
        zs = [_dot(qs, kt_ref[0, :, pl.ds(st, tk)]) for st in sts]
        costs, log_bs, befores = [], [], []
        for st, z in zip(sts, zs):
            cost = jnp.where(z > SB_LINEAR_FROM, z, jnp.log(1.0 + jnp.exp2(z)) * LOG2E)
            log_bs.append(z - cost)
            before = None
            if masked:
                before = (st + lax.broadcasted_iota(jnp.int32, (1, tk), 1)) < t_row
                cost = jnp.where(before, cost, 0.0)
            costs.append(cost.astype(BF16))
            befores.append(before)
        rems = [_dot(c, upper) for c in costs]
        rem_right = rem_ref[...]
        probs = []
        for log_b, c, rem, before in zip(log_bs, costs, rems, befores):
            ex = log_b - (rem + rem_right)
            if masked:
                ex = jnp.where(before, ex, NEG)
            probs.append(jnp.exp2(ex).astype(BF16))
            rem_right = rem_right + (rem[:, 0:1] + c[:, 0:1].astype(F32))
        rem_ref[...] = rem_right
        acc = acc_ref[...]
        for st, p in zip(sts, probs):
            acc = acc + _dot(p, v_ref[0, pl.ds(st, tk), :])
        acc_ref[...] = acc
        return jnp.min(rem_right)

    per_step = tq // tk
    n_full = q0 // tk
    spent = tiles([n_full + per_step - 1 - d for d in range(per_step)], True)

    def go_on(c):
        return (c[0] < n_full // per_step) & (c[1] < SB_DEAD)

    def body(c):
        it = c[0]
        return it + 1, tiles([n_full - 1 - per_step * it - d for d in range(per_step)], False)

    lax.while_loop(go_on, body, (jnp.int32(0), spent))
    acc = acc_ref[...]
    o_ref[0] = jnp.where(lane < SB_DH, acc[0:tq], acc[tq:2 * tq]).astype(o_ref.dtype)


def _sb_attention(oa, ob, t):
    bsz = oa.shape[0]
    tq, tk = SB_TQ, SB_TK
    npair = SB_HEADS // 2
    return pl.pallas_call(
        functools.partial(_sb_kernel, tq=tq, tk=tk),
        grid=(bsz, npair, t // tq),
        in_specs=[
            pl.BlockSpec((1, tq, LANES), lambda b, h, i: (b, i, h)),
            pl.BlockSpec((1, LANES, t), lambda b, h, i: (b, h, 0)),
            pl.BlockSpec((1, t, LANES), lambda b, h, i: (b, 0, npair + h)),
        ],
        out_specs=pl.BlockSpec((1, tq, LANES), lambda b, h, i: (b, i, h)),
        out_shape=jax.ShapeDtypeStruct((bsz, t, D_MODEL), BF16),
        scratch_shapes=[pltpu.VMEM((2 * tq, 1), F32), pltpu.VMEM((2 * tq, LANES), F32)],
        compiler_params=_cparams(("arbitrary", "arbitrary", "arbitrary")),
        name="sb_attention",
    )(oa, ob, oa)


def _diff_kernel(lam_ref, g_ref, qa_ref, kt_ref, v_ref, o_ref, *, tq, tk, lam_init):
    i = pl.program_id(2)
    q0 = i * tq
    qa = qa_ref[0]
    lane = lax.broadcasted_iota(jnp.int32, (1, 2 * LANES), 1)
    zero = jnp.zeros_like(qa)
    qs = jnp.concatenate(
        [jnp.where(((lane >= m_ * DIFF_DH) & (lane < (m_ + 1) * DIFF_DH)) | (lane >= LANES), qa, zero)
         for m_ in range(2)], axis=0)
    t_row = q0 + (lax.broadcasted_iota(jnp.int32, (2 * tq, 1), 0) & (tq - 1))

    def scores(j):
        return _dot(qs, kt_ref[0, :, pl.ds(pl.multiple_of(j * tk, tk), tk)])

    def body(j, carry):
        return _online_step(scores(j), *carry, v_ref[0, pl.ds(pl.multiple_of(j * tk, tk), tk), :])

    jd = q0 // tk
    carry = lax.fori_loop(0, jd, body, _softmax_init(2 * tq, LANES))
    st = pl.multiple_of(jd * tk, tk)
    col = st + lax.broadcasted_iota(jnp.int32, (1, tk), 1)
    s = jnp.where(col <= t_row, scores(jd), NEG)
    _, l, acc = _online_step(s, *carry, v_ref[0, pl.ds(st, tk), :])
    out = acc / l
    lam = lam_ref[...]
    lam_full = (jnp.exp(jnp.sum(lam[0:1] * lam[1:2], axis=-1, keepdims=True))
                - jnp.exp(jnp.sum(lam[2:3] * lam[3:4], axis=-1, keepdims=True)) + lam_init)
    o = out[0:tq] - lam_full * out[tq:2 * tq]
    o_ref[0] = (_rms(o, g_ref[...]) * (1.0 - lam_init)).astype(o_ref.dtype)


def _diff_attention(oa, ob, lam, subln_g, lam_init, t):
    bsz = oa.shape[0]
    tq, tk = DIFF_TQ, DIFF_TK
    v_blk = DIFF_HEADS * 2
    return pl.pallas_call(
        functools.partial(_diff_kernel, tq=tq, tk=tk, lam_init=lam_init),
        grid=(bsz, DIFF_HEADS, t // tq),
        in_specs=[
            _const_spec(lam.shape),
            _const_spec((1, 2 * DIFF_DH)),
            pl.BlockSpec((1, tq, 2 * LANES), lambda b, h, i: (b, i, h)),
            pl.BlockSpec((1, 2 * LANES, t), lambda b, h, i: (b, h, 0)),
            pl.BlockSpec((1, t, LANES), lambda b, h, i: (b, 0, v_blk + h)),
        ],
        out_specs=pl.BlockSpec((1, tq, LANES), lambda b, h, i: (b, i, h)),
        out_shape=jax.ShapeDtypeStruct((bsz, t, D_MODEL), BF16),
        compiler_params=_cparams(("arbitrary", "arbitrary", "arbitrary")),
        name="diff_attention",
    )(lam, subln_g.reshape(1, 2 * DIFF_DH), oa, ob, oa)


def _alibi_slopes(n):
    return jnp.exp2(-8.0 * jnp.arange(1, n + 1, dtype=F32) / n)


def _alibi_query_cols(slopes, width, first):
    n = slopes.shape[0]
    sl2 = slopes * LOG2E
    parts = jnp.stack([p.astype(F32) for p in _split3(sl2)], axis=1)
    b0 = jnp.zeros((n, width), F32)
    b0 = b0.at[:, first:first + 3].set(parts).at[:, first + 3:first + 6].set(parts)
    pv = jnp.zeros((n, width), F32).at[:, first + 6].set(-sl2)
    return b0.reshape(1, n * width), pv.reshape(1, n * width)


def _alibi_key_rows(nblocks, width, first):
    c = jnp.zeros((nblocks, width, 8), F32)
    c = c.at[:, first:first + 3, 0].set(1.0).at[:, first + 3:first + 6, 1].set(1.0).at[:, first + 6, 2].set(1.0)
    return c.reshape(nblocks * width, 8)


def _pad_blocks(w, nblocks, width):
    d = w.shape[0]
    k = w.shape[1] // nblocks
    return jnp.pad(w.reshape(d, nblocks, k), ((0, 0), (0, 0), (0, width - k))).reshape(d, nblocks * width)


def _nsa_weights(w_in):
    d = w_in.shape[0]
    bounds = [0, NSA_QW] + [NSA_QW + (k + 1) * NSA_KVW for k in range(6)]
    wq, wkc, wvc, wks, wvs, wkw, wvw = [w_in[:, bounds[k]:bounds[k + 1]] for k in range(7)]
    wg = w_in[:, bounds[7]:]
    wa = jnp.concatenate([
        _pad_blocks(wq, NSA_HEADS, LANES), wkc, wvc, _pad_blocks(wvs, NSA_KV, LANES),
        _pad_blocks(wvw, NSA_KV, LANES), _pad_blocks(wg, NSA_KV, LANES)], axis=1).astype(BF16)
    na = wa.shape[1]
    qcols = NSA_HEADS * LANES
    scale = NSA_DK ** -0.5 * LOG2E
    cs_q = jnp.where(jnp.arange(LANES) < NSA_DK, scale, 1.0).astype(F32)
    cs = jnp.concatenate([jnp.tile(cs_q, NSA_HEADS), jnp.ones((na - qcols,), F32)]).reshape(1, na)
    b0q, pvq = _alibi_query_cols(_alibi_slopes(NSA_HEADS), LANES, NSA_DK)
    b0 = jnp.pad(b0q, ((0, 0), (0, na - qcols)))
    pv = jnp.pad(pvq, ((0, 0), (0, na - qcols)))
    wb = jnp.concatenate([_pad_blocks(wks, NSA_KV, LANES), _pad_blocks(wkw, NSA_KV, LANES)], axis=1).T
    cb = _alibi_key_rows(2 * NSA_KV, LANES, NSA_DK)
    return wa, cs, b0, pv, wb.astype(BF16), cb


def _sb_weights(w_in):
    d = w_in.shape[0]
    wq, wk, wv = jnp.split(w_in, 3, axis=1)
    wa = jnp.concatenate([wq, wv], axis=1).astype(BF16)
    na = wa.shape[1]
    cs = jnp.concatenate([jnp.full((d,), SB_DH ** -0.5 * LOG2E, F32), jnp.ones((d,), F32)]).reshape(1, na)
    zero = jnp.zeros((1, na), F32)
    return wa, cs, zero, zero, wk.T.astype(BF16), jnp.zeros((d, 8), F32)


def _diff_weights(w_in):
    d = w_in.shape[0]
    wq, wk, wv = jnp.split(w_in, 3, axis=1)
    width = 2 * LANES
    wa = jnp.concatenate([_pad_blocks(wq, DIFF_HEADS, width), wv], axis=1).astype(BF16)
    na = wa.shape[1]
    qcols = DIFF_HEADS * width
    scale = DIFF_DH ** -0.5 * LOG2E
    cs_q = jnp.where(jnp.arange(width) < 2 * DIFF_DH, scale, 1.0).astype(F32)
    cs = jnp.concatenate([jnp.tile(cs_q, DIFF_HEADS), jnp.ones((na - qcols,), F32)]).reshape(1, na)
    b0q, pvq = _alibi_query_cols(_alibi_slopes(DIFF_HEADS), width, 2 * DIFF_DH)
    b0 = jnp.pad(b0q, ((0, 0), (0, na - qcols)))
    pv = jnp.pad(pvq, ((0, 0), (0, na - qcols)))
    wb = _pad_blocks(wk, DIFF_HEADS, width).T.astype(BF16)
    cb = _alibi_key_rows(DIFF_HEADS, width, 2 * DIFF_DH)
    return wa, cs, b0, pv, wb, cb


def _nsa_mixer(x, g, scl, shift, w_in, cmp_pe, cmp_w1, cmp_w2):
    bsz, t, _ = x.shape
    wa, cs, b0, pv, wb, cb = _nsa_weights(w_in)
    oa, ob = _inproj(x, g, scl, shift, wa, cs, b0, pv, wb, cb)
    nc = t // CMP_STRIDE
    kc0 = NSA_HEADS * LANES

    def grouped(lo):
        a = oa[:, :, lo:lo + NSA_KVW].reshape(bsz, nc, CMP_STRIDE, NSA_KV, NSA_DK)
        return jnp.moveaxis(a, 3, 1).reshape(bsz, NSA_KV, nc, CMP_STRIDE * NSA_DK)

    pe = cmp_pe.reshape(2, 2, CMP_STRIDE * NSA_DK)
    w2kt = cmp_w2[0].T.astype(BF16)
    w2v = jnp.pad(cmp_w2[1], ((0, 0), (0, LANES - NSA_DK))).astype(BF16)
    kct, vcp = _nsa_compress(grouped(kc0), grouped(kc0 + NSA_KVW), pe, cmp_w1.astype(BF16), w2kt, w2v)
    tok = jnp.arange(t)
    e_mat = (jnp.arange(N_SEL_PAD)[:, None] == (tok // SEL_LEN)[None, :]).astype(BF16)
    cstart = jnp.arange(nc) * CMP_STRIDE
    sstart = jnp.arange(N_SEL_PAD) * SEL_LEN
    ov_mat = ((cstart[None, :] < sstart[:, None] + SEL_LEN)
              & (cstart[None, :] + CMP_LEN > sstart[:, None])
              & (jnp.arange(nc)[None, :] < nc - 1)).astype(BF16)
    return _nsa_attention(oa, ob, kct, vcp, e_mat, ov_mat, t)


def _sb_mixer(x, g, scl, shift, w_in):
    t = x.shape[1]
    wa, cs, b0, pv, wb, cb = _sb_weights(w_in)
    oa, ob = _inproj(x, g, scl, shift, wa, cs, b0, pv, wb, cb)
    return _sb_attention(oa, ob, t)


def _diff_mixer(x, g, scl, shift, w_in, lam, subln_g, layer_idx):
    t = x.shape[1]
    wa, cs, b0, pv, wb, cb = _diff_weights(w_in)
    oa, ob = _inproj(x, g, scl, shift, wa, cs, b0, pv, wb, cb)
    lam_init = 0.8 - 0.6 * math.exp(-0.3 * layer_idx)
    return _diff_attention(oa, ob, lam, subln_g, lam_init, t)


def kernel(x, c, ada_w, ada_b, norm_g, ffn_w1, ffn_w2, nsa_w_in, nsa_cmp_pe, nsa_cmp_w1, nsa_cmp_w2,
           nsa_w_out, sb_w_in, sb_w_out, diff_w_in, diff_lam, diff_subln_g, diff_w_out):
    bsz, t, d = x.shape
    assert d == D_MODEL and t % ROW_TILE == 0 and t >= WIN + 128 and t <= N_SEL_PAD * SEL_LEN
    mod = _modulation(c, ada_w, ada_b)
    for i in range(DEPTH):
        def mods(sidx):
            return tuple(mod[i, :, sidx, k][:, None, :] for k in range(3))

        shift, scl, gate = mods(0)
        x = _ffn(x, norm_g[i, 0], scl, shift, gate, norm_g[i, 1],
                 ffn_w1[i, 0].astype(BF16), ffn_w2[i, 0].astype(BF16))
        shift, scl, gate = mods(1)
        kind, j = i % N_MIXERS, i // N_MIXERS
        if kind == 0:
            o = _nsa_mixer(x, norm_g[i, 2], scl, shift, nsa_w_in[j], nsa_cmp_pe[j], nsa_cmp_w1[j],
                           nsa_cmp_w2[j])
            w_out = nsa_w_out[j]
        elif kind == 1:
            o = _sb_mixer(x, norm_g[i, 2], scl, shift, sb_w_in[j])
            w_out = sb_w_out[j]
        else:
            o = _diff_mixer(x, norm_g[i, 2], scl, shift, diff_w_in[j], diff_lam[j], diff_subln_g[j], i)
            w_out = diff_w_out[j]
        x = _outproj(o, w_out.astype(BF16), x, gate, norm_g[i, 3])
        shift, scl, gate = mods(2)
        x = _ffn(x, norm_g[i, 4], scl, shift, gate, norm_g[i, 5],
                 ffn_w1[i, 1].astype(BF16), ffn_w2[i, 1].astype(BF16))
    return x
```

```python
import functools
import math

import jax
import jax.numpy as jnp
from jax import lax
from jax.experimental import pallas as pl
from jax.experimental.pallas import tpu as pltpu

F32 = jnp.float32
BF16 = jnp.bfloat16

D_MODEL = 1024
DEPTH = 4
N_MIXERS = 3
EPS = 1e-6
NEG = -1e30
FORCE = 1e4
LOG2E = 1.4426950408889634
NSA_HEADS = 16
NSA_DK = 64
NSA_KV = 4
NSA_REP = NSA_HEADS // NSA_KV
CMP_LEN = 32
CMP_STRIDE = 16
CMP_HID = 256
SEL_LEN = 64
SEL_TOPK = 16
WIN = 512
NSA_QW = NSA_HEADS * NSA_DK
NSA_KVW = NSA_KV * NSA_DK
N_SEL_PAD = 128
SB_HEADS = 16
SB_DH = D_MODEL // SB_HEADS
DIFF_HEADS = 8
DIFF_DH = D_MODEL // (2 * DIFF_HEADS)
D_FF = 2816

LANES = 128
ROW_TILE = 512
FF_CHUNK = 256
DIFF_TQ, DIFF_TK = 512, 512
SB_TQ, SB_TK = 512, 256
SB_LINEAR_FROM = 64.0
SB_DEAD = 192.0
DIFF_DEAD = 192.0
ALIBI_COLS = 7
NSA_TQ, NSA_TK = 256, 512
VMEM_LIMIT = 56 * 1024 * 1024


def _cparams(sem):
    return pltpu.CompilerParams(dimension_semantics=sem, vmem_limit_bytes=VMEM_LIMIT)


def _const_spec(shape):
    n = len(shape)
    return pl.BlockSpec(shape, lambda *_: (0,) * n)


def _rms(x, g):
    return x * lax.rsqrt(jnp.mean(x * x, axis=-1, keepdims=True) + EPS) * g


def _dot(a, b):
    return jnp.dot(a, b, preferred_element_type=F32)


def _dot_nt(a, b):
    return lax.dot_general(a, b, (((1,), (1,)), ((), ())), preferred_element_type=F32)


def _mod_kernel(c_ref, w_ref, b_ref, o_ref):
    c = c_ref[...]
    cond = c * jax.nn.sigmoid(c)
    o_ref[0] = _dot(cond.astype(BF16), w_ref[0].astype(BF16)) + b_ref[0]


def _modulation(c, ada_w, ada_b):
    depth, d, n = ada_w.shape
    b = c.shape[0]
    rows = 8
    c_pad = jnp.zeros((rows, d), F32).at[:b].set(c)
    nc = 1152
    out = pl.pallas_call(
        _mod_kernel,
        grid=(depth, n // nc),
        in_specs=[
            pl.BlockSpec((rows, d), lambda i, j: (0, 0)),
            pl.BlockSpec((1, d, nc), lambda i, j: (i, 0, j)),
            pl.BlockSpec((1, 1, nc), lambda i, j: (i, 0, j)),
        ],
        out_specs=pl.BlockSpec((1, rows, nc), lambda i, j: (i, 0, j)),
        out_shape=jax.ShapeDtypeStruct((depth, rows, n), F32),
        compiler_params=_cparams(("arbitrary", "arbitrary")),
        name="adaln_mod",
    )(c_pad, ada_w, ada_b.reshape(depth, 1, n))
    return out[:, :b].reshape(depth, b, 3, 3, d)


def _inproj_kernel(*refs, tt, nca, ncb, has_b):
    if has_b:
        (x_ref, g_ref, scl_ref, sh_ref, wa_ref, cs_ref, b0_ref, pv_ref, wb_ref, cb_ref,
         oa_ref, ob_ref) = refs
    else:
        x_ref, g_ref, scl_ref, sh_ref, wa_ref, cs_ref, b0_ref, pv_ref, oa_ref = refs
    x = x_ref[0]
    hh = (_rms(x, g_ref[...]) * (1.0 + scl_ref[0]) + sh_ref[0]).astype(BF16)
    t0 = pl.program_id(1) * tt
    pos = (t0 + lax.broadcasted_iota(jnp.int32, (tt, 1), 0)).astype(F32)
    na = wa_ref.shape[1]
    for c in range(na // nca):
        sl = slice(c * nca, (c + 1) * nca)
        acc = _dot(hh, wa_ref[:, sl])
        y = acc * cs_ref[:, sl] + b0_ref[:, sl] + pos * pv_ref[:, sl]
        oa_ref[0, :, sl] = y.astype(oa_ref.dtype)
    if has_b:
        tpos = t0 + lax.broadcasted_iota(jnp.int32, (1, tt), 1)
        phi = ((tpos >> 7) << 7).astype(F32)
        plo = (tpos & 127).astype(F32)
        nb = wb_ref.shape[0]
        for c in range(nb // ncb):
            sl = slice(c * ncb, (c + 1) * ncb)
            acc = _dot_nt(wb_ref[sl, :], hh)
            cc = cb_ref[sl, :]
            y = acc + cc[:, 0:1] * phi + cc[:, 1:2] * plo + cc[:, 2:3]
            ob_ref[0, sl, :] = y.astype(ob_ref.dtype)


def _inproj(x, g, scl, shift, wa, cs, b0, pv, wb=None, cb=None):
    bsz, t, d = x.shape
    tt = min(ROW_TILE, t)
    na = wa.shape[1]
    has_b = wb is not None
    in_specs = [
        pl.BlockSpec((1, tt, d), lambda b, i: (b, i, 0)),
        _const_spec((1, d)),
        pl.BlockSpec((1, 1, d), lambda b, i: (b, 0, 0)),
        pl.BlockSpec((1, 1, d), lambda b, i: (b, 0, 0)),
        _const_spec((d, na)),
        _const_spec((1, na)),
        _const_spec((1, na)),
        _const_spec((1, na)),
    ]
    args = [x, g.reshape(1, d), scl, shift, wa, cs, b0, pv]
    out_specs = [pl.BlockSpec((1, tt, na), lambda b, i: (b, i, 0))]
    out_shape = [jax.ShapeDtypeStruct((bsz, t, na), BF16)]
    ncb = 512
    if has_b:
        nb = wb.shape[0]
        in_specs += [_const_spec((nb, d)), _const_spec((nb, 8))]
        args += [wb, cb]
        out_specs.append(pl.BlockSpec((1, nb, tt), lambda b, i: (b, 0, i)))
        out_shape.append(jax.ShapeDtypeStruct((bsz, nb, t), BF16))
    outs = pl.pallas_call(
        functools.partial(_inproj_kernel, tt=tt, nca=512, ncb=ncb, has_b=has_b),
        grid=(bsz, t // tt),
        in_specs=in_specs,
        out_specs=out_specs,
        out_shape=out_shape,
        compiler_params=_cparams(("arbitrary", "arbitrary")),
        name="mixer_inproj",
    )(*args)
    return outs if has_b else (outs[0], None)


def _outproj_kernel(o_ref, w_ref, x_ref, gate_ref, g2_ref, out_ref):
    y = _dot(o_ref[0], w_ref[...])
    out_ref[0] = x_ref[0] + gate_ref[0] * _rms(y, g2_ref[...])


def _outproj(o, w, x, gate, g2):
    bsz, t, d = x.shape
    tt = min(ROW_TILE, t)
    k = o.shape[2]
    return pl.pallas_call(
        _outproj_kernel,
        grid=(bsz, t // tt),
        in_specs=[
            pl.BlockSpec((1, tt, k), lambda b, i: (b, i, 0)),
            _const_spec((k, d)),
            pl.BlockSpec((1, tt, d), lambda b, i: (b, i, 0)),
            pl.BlockSpec((1, 1, d), lambda b, i: (b, 0, 0)),
            _const_spec((1, d)),
        ],
        out_specs=pl.BlockSpec((1, tt, d), lambda b, i: (b, i, 0)),
        out_shape=jax.ShapeDtypeStruct((bsz, t, d), F32),
        compiler_params=_cparams(("arbitrary", "arbitrary")),
        name="mixer_outproj",
    )(o, w, x, gate, g2.reshape(1, d))


def _ffn_kernel(x_ref, g1_ref, scl_ref, sh_ref, gate_ref, g2_ref, w1_ref, w2_ref, o_ref, acc_ref):
    x = x_ref[0]
    hh = (_rms(x, g1_ref[...]) * (1.0 + scl_ref[0]) + sh_ref[0]).astype(BF16)
    dff = w2_ref.shape[0]
    for c in range(dff // FF_CHUNK):
        lo = c * FF_CHUNK
        gt = _dot(hh, w1_ref[:, lo:lo + FF_CHUNK])
        up = _dot(hh, w1_ref[:, dff + lo:dff + lo + FF_CHUNK])
        act = (gt * jax.nn.sigmoid(gt) * up).astype(BF16)
        part = _dot(act, w2_ref[lo:lo + FF_CHUNK, :])
        if c == 0:
            acc_ref[...] = part
        else:
            acc_ref[...] += part
    o_ref[0] = x + 0.5 * gate_ref[0] * _rms(acc_ref[...], g2_ref[...])


def _ffn(x, g1, scl, shift, gate, g2, w1, w2):
    bsz, t, d = x.shape
    tt = min(ROW_TILE, t)
    vec = pl.BlockSpec((1, 1, d), lambda b, i: (b, 0, 0))
    return pl.pallas_call(
        _ffn_kernel,
        grid=(bsz, t // tt),
        in_specs=[
            pl.BlockSpec((1, tt, d), lambda b, i: (b, i, 0)),
            _const_spec((1, d)), vec, vec, vec, _const_spec((1, d)),
            _const_spec(w1.shape), _const_spec(w2.shape),
        ],
        out_specs=pl.BlockSpec((1, tt, d), lambda b, i: (b, i, 0)),
        out_shape=jax.ShapeDtypeStruct((bsz, t, d), F32),
        scratch_shapes=[pltpu.VMEM((tt, d), F32)],
        compiler_params=_cparams(("arbitrary", "arbitrary")),
        name="ffn",
    )(x, g1.reshape(1, d), scl, shift, gate, g2.reshape(1, d), w1, w2)


def _online_step(s, m, l, acc, v):
    m_new = jnp.maximum(m, jnp.max(s, axis=-1, keepdims=True))
    alpha = jnp.exp2(m - m_new)
    p = jnp.exp2(s - m_new)
    l = alpha * l + jnp.sum(p, axis=-1, keepdims=True)
    acc = alpha * acc + _dot(p.astype(BF16), v)
    return m_new, l, acc


def _softmax_init(rows, width):
    return (jnp.full((rows, 1), NEG, F32), jnp.zeros((rows, 1), F32), jnp.zeros((rows, width), F32))


def _gelu_tanh(x):
    return 0.5 * x * (1.0 + jnp.tanh(math.sqrt(2.0 / math.pi) * (x + 0.044715 * (x * x * x))))


def _cmp_kernel(xk_ref, xv_ref, pe_ref, w1_ref, w2kt_ref, w2v_ref, kt_ref, v_ref):
    nc = xk_ref.shape[2]
    half = xk_ref.shape[3]

    def hidden(x_ref, j):
        x = x_ref[0, 0].astype(F32)
        top = _dot((x + pe_ref[j, 0:1, :]).astype(BF16), w1_ref[j, 0:half, :])
        bot = _dot((x + pe_ref[j, 1:2, :]).astype(BF16), w1_ref[j, half:2 * half, :])
        return _gelu_tanh(top + pltpu.roll(bot, nc - 1, axis=0)).astype(BF16)

    hk = hidden(xk_ref, 0)
    hv = hidden(xv_ref, 1)
    kt_ref[0, 0, 0:NSA_DK, :] = _dot_nt(w2kt_ref[...], hk).astype(BF16)
    n = lax.broadcasted_iota(jnp.int32, (1, nc), 1)
    end = n * CMP_STRIDE + (CMP_LEN - 1)
    phi = ((end >> 7) << 7).astype(F32)
    plo = (end & 127).astype(F32)
    r = lax.broadcasted_iota(jnp.int32, (LANES - NSA_DK, 1), 0)
    aug = jnp.where(r < 3, phi, jnp.where(r < 6, plo, jnp.where(r == 6, 1.0, 0.0)))
    kt_ref[0, 0, NSA_DK:LANES, :] = aug.astype(BF16)
    v_ref[0, 0] = _dot(hv, w2v_ref[...]).astype(BF16)


def _nsa_compress(xk, xv, pe, w1, w2kt, w2v):
    bsz, g, nc, half = xk.shape
    xspec = pl.BlockSpec((1, 1, nc, half), lambda b, j: (b, j, 0, 0))
    return pl.pallas_call(
        _cmp_kernel,
        grid=(bsz, g),
        in_specs=[xspec, xspec, _const_spec(pe.shape), _const_spec(w1.shape), _const_spec(w2kt.shape),
                  _const_spec(w2v.shape)],
        out_specs=[pl.BlockSpec((1, 1, LANES, nc), lambda b, j: (b, j, 0, 0)),
                   pl.BlockSpec((1, 1, nc, LANES), lambda b, j: (b, j, 0, 0))],
        out_shape=[jax.ShapeDtypeStruct((bsz, g, LANES, nc), BF16),
                   jax.ShapeDtypeStruct((bsz, g, nc, LANES), BF16)],
        compiler_params=_cparams(("arbitrary", "arbitrary")),
        name="nsa_compress",
    )(xk, xv, pe, w1, w2kt, w2v)


def _split3(x):
    hi = x.astype(BF16)
    r1 = x - hi.astype(F32)
    mid = r1.astype(BF16)
    lo = (r1 - mid.astype(F32)).astype(BF16)
    return hi, mid, lo


def _nsa_attn_kernel(qa_ref, kct_ref, vc_ref, kst_ref, vs_ref, kwt_ref, vw_ref, gt_ref, e_ref, ov_ref,
                     o_ref, tiles_ref, *, tq, tk):
    i = pl.program_id(2)
    q0 = i * tq
    rows = NSA_REP * tq
    qa = qa_ref[0]
    qs = jnp.concatenate([qa[:, r * LANES:(r + 1) * LANES] for r in range(NSA_REP)], axis=0)
    t_row = q0 + (lax.broadcasted_iota(jnp.int32, (rows, 1), 0) & (tq - 1))

    nc = kct_ref.shape[3]
    s = _dot(qs, kct_ref[0, 0])
    cend = lax.broadcasted_iota(jnp.int32, (1, nc), 1) * CMP_STRIDE + (CMP_LEN - 1)
    valid = cend <= t_row
    s = jnp.where(valid, s, NEG)
    e = jnp.where(valid, jnp.exp2(s - jnp.max(s, axis=-1, keepdims=True)), 0.0)
    l = jnp.sum(e, axis=-1, keepdims=True)
    p = e / jnp.where(l > 0.0, l, 1.0)
    o_cmp = _dot(p.astype(BF16), vc_ref[0, 0])

    p4 = p[0:tq]
    for r in range(1, NSA_REP):
        p4 = p4 + p[r * tq:(r + 1) * tq]
    ov = ov_ref[...]
    imp = sum(_dot_nt(ov, part) for part in _split3(p4))

    wk = WIN + tq
    st = pl.multiple_of(jnp.maximum(q0 - WIN, 0), LANES)
    col = st + lax.broadcasted_iota(jnp.int32, (1, wk), 1)
    valid = (col <= t_row) & (col > t_row - WIN)
    s = jnp.where(valid, _dot(qs, kwt_ref[0, :, pl.ds(st, wk)]), NEG)
    e = jnp.exp2(s - jnp.max(s, axis=-1, keepdims=True))
    o_win = _dot(e.astype(BF16), vw_ref[0, pl.ds(st, wk), :]) / jnp.sum(e, axis=-1, keepdims=True)

    cur = (q0 + lax.broadcasted_iota(jnp.int32, (1, tq), 1)) >> 6
    sid = lax.broadcasted_iota(jnp.int32, (N_SEL_PAD, 1), 0)
    forced = (sid == 0) | (sid == cur) | (sid == cur - 1)
    work = jnp.where(forced | (sid > cur), -3.0e38, imp)
    selb = jnp.where(forced, 0.0, NEG)
    for _ in range(SEL_TOPK - 3):
        mx = jnp.max(work, axis=0, keepdims=True)
        first = jnp.min(jnp.where(work == mx, sid, N_SEL_PAD), axis=0, keepdims=True)
        pick = sid == first
        selb = jnp.where(pick, 0.0, selb)
        work = jnp.where(pick, -3.0e38, work)
    selb = selb.T

    jd = q0 // tk
    n_tiles = kst_ref.shape[2] // tk
    any_sel = jnp.max(jnp.where(selb == 0.0, 1.0, 0.0), axis=0, keepdims=True)
    blk_tile = (lax.broadcasted_iota(jnp.int32, (N_SEL_PAD, LANES), 0) // (tk // SEL_LEN)
                == lax.broadcasted_iota(jnp.int32, (N_SEL_PAD, LANES), 1)).astype(BF16)
    tile_cnt = _dot(jnp.broadcast_to(any_sel, (16, N_SEL_PAD)).astype(BF16), blk_tile)
    n_act = jnp.int32(0)
    for j in range(n_tiles):
        tiles_ref[n_act] = j
        n_act = n_act + ((tile_cnt[0, j] > 0.0) & (j < jd)).astype(jnp.int32)

    selb4 = jnp.concatenate([selb.astype(BF16)] * NSA_REP, axis=0)
    qsel = jnp.concatenate([qs, selb4], axis=1)

    def sel_scores(st):
        return _dot(qsel, jnp.concatenate([kst_ref[0, :, pl.ds(st, tk)], e_ref[:, pl.ds(st, tk)]], axis=0))

    def sel_body(it, carry):
        st = pl.multiple_of(tiles_ref[it] * tk, tk)
        return _online_step(sel_scores(st), *carry, vs_ref[0, pl.ds(st, tk), :])

    carry = lax.fori_loop(0, n_act, sel_body, _softmax_init(rows, LANES))
    st = pl.multiple_of(jd * tk, tk)
    col = st + lax.broadcasted_iota(jnp.int32, (1, tk), 1)
    s = jnp.where(col <= t_row, sel_scores(st), NEG)
    _, l, acc = _online_step(s, *carry, vs_ref[0, pl.ds(st, tk), :])
    o_sel = acc / l

    gs = jax.nn.sigmoid(gt_ref[0].astype(F32))
    outs = []
    for r in range(NSA_REP):
        sl = slice(r * tq, (r + 1) * tq)
        outs.append(gs[:, 3 * r:3 * r + 1] * o_cmp[sl] + gs[:, 3 * r + 1:3 * r + 2] * o_sel[sl]
                    + gs[:, 3 * r + 2:3 * r + 3] * o_win[sl])
    pairs = [outs[2 * a] + pltpu.roll(outs[2 * a + 1], NSA_DK, axis=1) for a in range(NSA_REP // 2)]
    o_ref[0] = jnp.concatenate(pairs, axis=1).astype(o_ref.dtype)


def _nsa_attention(oa, ob, kct, vcp, e_mat, ov_mat, t):
    bsz = oa.shape[0]
    tq, tk = NSA_TQ, NSA_TK
    qw = NSA_REP * LANES
    vs_blk = (NSA_HEADS * LANES + 2 * NSA_KVW) // LANES
    vw_blk = vs_blk + NSA_KV
    gt_blk = vw_blk + NSA_KV
    nc = kct.shape[3]
    return pl.pallas_call(
        functools.partial(_nsa_attn_kernel, tq=tq, tk=tk),
        grid=(bsz, NSA_KV, t // tq),
        in_specs=[
            pl.BlockSpec((1, tq, qw), lambda b, g, i: (b, i, g)),
            pl.BlockSpec((1, 1, LANES, nc), lambda b, g, i: (b, g, 0, 0)),
            pl.BlockSpec((1, 1, nc, LANES), lambda b, g, i: (b, g, 0, 0)),
            pl.BlockSpec((1, LANES, t), lambda b, g, i: (b, g, 0)),
            pl.BlockSpec((1, t, LANES), lambda b, g, i: (b, 0, vs_blk + g)),
            pl.BlockSpec((1, LANES, t), lambda b, g, i: (b, NSA_KV + g, 0)),
            pl.BlockSpec((1, t, LANES), lambda b, g, i: (b, 0, vw_blk + g)),
            pl.BlockSpec((1, tq, LANES), lambda b, g, i: (b, i, gt_blk + g)),
            _const_spec(e_mat.shape),
            _const_spec(ov_mat.shape),
        ],
        out_specs=pl.BlockSpec((1, tq, NSA_REP * NSA_DK), lambda b, g, i: (b, i, g)),
        out_shape=jax.ShapeDtypeStruct((bsz, t, NSA_QW), BF16),
        scratch_shapes=[pltpu.SMEM((t // tk,), jnp.int32)],
        compiler_params=_cparams(("arbitrary", "arbitrary", "arbitrary")),
        name="nsa_attention",
    )(oa, kct, vcp, ob, oa, ob, oa, oa, e_mat, ov_mat)


def _sb_kernel(q_ref, kt_ref, v_ref, o_ref, rem_ref, acc_ref, *, tq, tk):
    i = pl.program_id(2)
    q0 = i * tq
    q = q_ref[0]
    lane = lax.broadcasted_iota(jnp.int32, (1, LANES), 1)
    zero = jnp.zeros_like(q)
    qs = jnp.concatenate([jnp.where(lane < SB_DH, q, zero), jnp.where(lane >= SB_DH, q, zero)], axis=0)
    t_row = q0 + (lax.broadcasted_iota(jnp.int32, (2 * tq, 1), 0) & (tq - 1))
    upper = (lax.broadcasted_iota(jnp.int32, (tk, tk), 0) > lax.broadcasted_iota(jnp.int32, (tk, tk), 1)
             ).astype(BF16)
    rem_ref[...] = jnp.zeros(rem_ref.shape, F32)
    acc_ref[...] = jnp.zeros(acc_ref.shape, F32)

    def tiles(js, masked):
        sts = [pl.multiple_of(j * tk, tk) for j in js]
        zs = [_dot(qs, kt_ref[0, :, pl.ds(st, tk)]) for st in sts]
        costs, log_bs, befores = [], [], []
        for st, z in zip(sts, zs):
            cost = jnp.where(z > SB_LINEAR_FROM, z, jnp.log(1.0 + jnp.exp2(z)) * LOG2E)
            log_bs.append(z - cost)
            before = None
            if masked:
                before = (st + lax.broadcasted_iota(jnp.int32, (1, tk), 1)) < t_row
                cost = jnp.where(before, cost, 0.0)
            costs.append(cost.astype(BF16))
            befores.append(before)
        rems = [_dot(c, upper) for c in costs]
        rem_right = rem_ref[...]
        probs = []
        for log_b, c, rem, before in zip(log_bs, costs, rems, befores):
            ex = log_b - (rem + rem_right)
            if masked:
                ex = jnp.where(before, ex, NEG)
            probs.append(jnp.exp2(ex).astype(BF16))
            rem_right = rem_right + (rem[:, 0:1] + c[:, 0:1].astype(F32))
        rem_ref[...] = rem_right
        acc = acc_ref[...]
        for st, p in zip(sts, probs):
            acc = acc + _dot(p, v_ref[0, pl.ds(st, tk), :])
        acc_ref[...] = acc
        return jnp.min(rem_right)

    per_step = tq // tk
    n_full = q0 // tk
    spent = tiles([n_full + per_step - 1 - d for d in range(per_step)], True)

    def go_on(c):
        return (c[0] < n_full // per_step) & (c[1] < SB_DEAD)

    def body(c):
        it = c[0]
        return it + 1, tiles([n_full - 1 - per_step * it - d for d in range(per_step)], False)

    lax.while_loop(go_on, body, (jnp.int32(0), spent))
    acc = acc_ref[...]
    o_ref[0] = jnp.where(lane < SB_DH, acc[0:tq], acc[tq:2 * tq]).astype(o_ref.dtype)


def _sb_attention(oa, ob, t):
    bsz = oa.shape[0]
    tq, tk = SB_TQ, SB_TK
    npair = SB_HEADS // 2
    return pl.pallas_call(
        functools.partial(_sb_kernel, tq=tq, tk=tk),
        grid=(bsz, npair, t // tq),
        in_specs=[
            pl.BlockSpec((1, tq, LANES), lambda b, h, i: (b, i, h)),
            pl.BlockSpec((1, LANES, t), lambda b, h, i: (b, h, 0)),
            pl.BlockSpec((1, t, LANES), lambda b, h, i: (b, 0, npair + h)),
        ],
        out_specs=pl.BlockSpec((1, tq, LANES), lambda b, h, i: (b, i, h)),
        out_shape=jax.ShapeDtypeStruct((bsz, t, D_MODEL), BF16),
        scratch_shapes=[pltpu.VMEM((2 * tq, 1), F32), pltpu.VMEM((2 * tq, LANES), F32)],
        compiler_params=_cparams(("arbitrary", "arbitrary", "arbitrary")),
        name="sb_attention",
    )(oa, ob, oa)


def _diff_kernel(slope_ref, lam_ref, g_ref, qa_ref, kt_ref, v_ref, o_ref, kmax_ref, *, tq, tk, lam_init):
    h = pl.program_id(1)
    i = pl.program_id(2)
    q0 = i * tq
    n_tiles = kt_ref.shape[2] // tk

    @pl.when(i == 0)
    def _():
        run = jnp.float32(0.0)
        for j in range(n_tiles):
            kf = kt_ref[0, 0:LANES, j * tk:(j + 1) * tk].astype(F32)
            sq = kf * kf
            norms = jnp.sqrt(jnp.maximum(jnp.sum(sq[0:DIFF_DH], axis=0, keepdims=True),
                                         jnp.sum(sq[DIFF_DH:LANES], axis=0, keepdims=True)))
            run = jnp.maximum(run, jnp.max(norms))
            kmax_ref[j] = run

    qa = qa_ref[0]
    lane = lax.broadcasted_iota(jnp.int32, (1, 2 * LANES), 1)
    zero = jnp.zeros_like(qa)
    qs = jnp.concatenate(
        [jnp.where(((lane >= m_ * DIFF_DH) & (lane < (m_ + 1) * DIFF_DH)) | (lane >= LANES), qa, zero)
         for m_ in range(2)], axis=0)
    t_row = q0 + (lax.broadcasted_iota(jnp.int32, (2 * tq, 1), 0) & (tq - 1))

    def scores(j):
        return _dot(qs, kt_ref[0, :, pl.ds(pl.multiple_of(j * tk, tk), tk)])

    def values(j):
        return v_ref[0, pl.ds(pl.multiple_of(j * tk, tk), tk), :]

    jd = q0 // tk
    col = jd * tk + lax.broadcasted_iota(jnp.int32, (1, tk), 1)
    state = _online_step(jnp.where(col <= t_row, scores(jd), NEG), *_softmax_init(2 * tq, LANES), values(jd))
    qf = jnp.where(lane < LANES, qs, jnp.zeros_like(qs)).astype(F32)
    q_norm = jnp.max(jnp.sqrt(jnp.sum(qf * qf, axis=-1, keepdims=True)))
    row_col = 2 * DIFF_DH + ALIBI_COLS - 1
    row_const = qs[:, row_col:row_col + 1].astype(F32)
    slope = slope_ref[h]

    def live(c):
        j = jnp.maximum(c[0], 0)
        bound = q_norm * kmax_ref[j] * 1.001 + slope * ((j + 1) * tk - 1).astype(F32)
        return (c[0] >= 0) & (bound - c[4] > -DIFF_DEAD)

    def body(c):
        m, l, acc = _online_step(scores(c[0]), c[1], c[2], c[3], values(c[0]))
        return c[0] - 1, m, l, acc, jnp.min(m - row_const)

    _, _, l, acc, _ = lax.while_loop(live, body, (jd - 1, *state, jnp.min(state[0] - row_const)))
    out = acc / l
    lam = lam_ref[...]
    lam_full = (jnp.exp(jnp.sum(lam[0:1] * lam[1:2], axis=-1, keepdims=True))
                - jnp.exp(jnp.sum(lam[2:3] * lam[3:4], axis=-1, keepdims=True)) + lam_init)
    o = out[0:tq] - lam_full * out[tq:2 * tq]
    o_ref[0] = (_rms(o, g_ref[...]) * (1.0 - lam_init)).astype(o_ref.dtype)


def _diff_attention(oa, ob, lam, subln_g, lam_init, t):
    bsz = oa.shape[0]
    tq, tk = DIFF_TQ, DIFF_TK
    v_blk = DIFF_HEADS * 2
    slopes = _alibi_slopes(DIFF_HEADS) * LOG2E
    return pl.pallas_call(
        functools.partial(_diff_kernel, tq=tq, tk=tk, lam_init=lam_init),
        grid=(bsz, DIFF_HEADS, t // tq),
        in_specs=[
            pl.BlockSpec(memory_space=pltpu.SMEM),
            _const_spec(lam.shape),
            _const_spec((1, 2 * DIFF_DH)),
            pl.BlockSpec((1, tq, 2 * LANES), lambda b, h, i: (b, i, h)),
            pl.BlockSpec((1, 2 * LANES, t), lambda b, h, i: (b, h, 0)),
            pl.BlockSpec((1, t, LANES), lambda b, h, i: (b, 0, v_blk + h)),
        ],
        out_specs=pl.BlockSpec((1, tq, LANES), lambda b, h, i: (b, i, h)),
        out_shape=jax.ShapeDtypeStruct((bsz, t, D_MODEL), BF16),
        scratch_shapes=[pltpu.SMEM((t // tk,), F32)],
        compiler_params=_cparams(("arbitrary", "arbitrary", "arbitrary")),
        name="diff_attention",
    )(slopes, lam, subln_g.reshape(1, 2 * DIFF_DH), oa, ob, oa)


def _alibi_slopes(n):
    return jnp.exp2(-8.0 * jnp.arange(1, n + 1, dtype=F32) / n)


def _alibi_query_cols(slopes, width, first):
    n = slopes.shape[0]
    sl2 = slopes * LOG2E
    parts = jnp.stack([p.astype(F32) for p in _split3(sl2)], axis=1)
    b0 = jnp.zeros((n, width), F32)
    b0 = b0.at[:, first:first + 3].set(parts).at[:, first + 3:first + 6].set(parts)
    pv = jnp.zeros((n, width), F32).at[:, first + 6].set(-sl2)
    return b0.reshape(1, n * width), pv.reshape(1, n * width)


def _alibi_key_rows(nblocks, width, first):
    c = jnp.zeros((nblocks, width, 8), F32)
    c = c.at[:, first:first + 3, 0].set(1.0).at[:, first + 3:first + 6, 1].set(1.0).at[:, first + 6, 2].set(1.0)
    return c.reshape(nblocks * width, 8)


def _pad_blocks(w, nblocks, width):
    d = w.shape[0]
    k = w.shape[1] // nblocks
    return jnp.pad(w.reshape(d, nblocks, k), ((0, 0), (0, 0), (0, width - k))).reshape(d, nblocks * width)


def _nsa_weights(w_in):
    d = w_in.shape[0]
    bounds = [0, NSA_QW] + [NSA_QW + (k + 1) * NSA_KVW for k in range(6)]
    wq, wkc, wvc, wks, wvs, wkw, wvw = [w_in[:, bounds[k]:bounds[k + 1]] for k in range(7)]
    wg = w_in[:, bounds[7]:]
    wa = jnp.concatenate([
        _pad_blocks(wq, NSA_HEADS, LANES), wkc, wvc, _pad_blocks(wvs, NSA_KV, LANES),
        _pad_blocks(wvw, NSA_KV, LANES), _pad_blocks(wg, NSA_KV, LANES)], axis=1).astype(BF16)
    na = wa.shape[1]
    qcols = NSA_HEADS * LANES
    scale = NSA_DK ** -0.5 * LOG2E
    cs_q = jnp.where(jnp.arange(LANES) < NSA_DK, scale, 1.0).astype(F32)
    cs = jnp.concatenate([jnp.tile(cs_q, NSA_HEADS), jnp.ones((na - qcols,), F32)]).reshape(1, na)
    b0q, pvq = _alibi_query_cols(_alibi_slopes(NSA_HEADS), LANES, NSA_DK)
    b0 = jnp.pad(b0q, ((0, 0), (0, na - qcols)))
    pv = jnp.pad(pvq, ((0, 0), (0, na - qcols)))
    wb = jnp.concatenate([_pad_blocks(wks, NSA_KV, LANES), _pad_blocks(wkw, NSA_KV, LANES)], axis=1).T
    cb = _alibi_key_rows(2 * NSA_KV, LANES, NSA_DK)
    return wa, cs, b0, pv, wb.astype(BF16), cb


def _sb_weights(w_in):
    d = w_in.shape[0]
    wq, wk, wv = jnp.split(w_in, 3, axis=1)
    wa = jnp.concatenate([wq, wv], axis=1).astype(BF16)
    na = wa.shape[1]
    cs = jnp.concatenate([jnp.full((d,), SB_DH ** -0.5 * LOG2E, F32), jnp.ones((d,), F32)]).reshape(1, na)
    zero = jnp.zeros((1, na), F32)
    return wa, cs, zero, zero, wk.T.astype(BF16), jnp.zeros((d, 8), F32)


def _diff_weights(w_in):
    d = w_in.shape[0]
    wq, wk, wv = jnp.split(w_in, 3, axis=1)
    width = 2 * LANES
    wa = jnp.concatenate([_pad_blocks(wq, DIFF_HEADS, width), wv], axis=1).astype(BF16)
    na = wa.shape[1]
    qcols = DIFF_HEADS * width
    scale = DIFF_DH ** -0.5 * LOG2E
    cs_q = jnp.where(jnp.arange(width) < 2 * DIFF_DH, scale, 1.0).astype(F32)
    cs = jnp.concatenate([jnp.tile(cs_q, DIFF_HEADS), jnp.ones((na - qcols,), F32)]).reshape(1, na)
    b0q, pvq = _alibi_query_cols(_alibi_slopes(DIFF_HEADS), width, 2 * DIFF_DH)
    b0 = jnp.pad(b0q, ((0, 0), (0, na - qcols)))
    pv = jnp.pad(pvq, ((0, 0), (0, na - qcols)))
    wb = _pad_blocks(wk, DIFF_HEADS, width).T.astype(BF16)
    cb = _alibi_key_rows(DIFF_HEADS, width, 2 * DIFF_DH)
    return wa, cs, b0, pv, wb, cb


def _nsa_mixer(x, g, scl, shift, w_in, cmp_pe, cmp_w1, cmp_w2):
    bsz, t, _ = x.shape
    wa, cs, b0, pv, wb, cb = _nsa_weights(w_in)
    oa, ob = _inproj(x, g, scl, shift, wa, cs, b0, pv, wb, cb)
    nc = t // CMP_STRIDE
    kc0 = NSA_HEADS * LANES

    def grouped(lo):
        a = oa[:, :, lo:lo + NSA_KVW].reshape(bsz, nc, CMP_STRIDE, NSA_KV, NSA_DK)
        return jnp.moveaxis(a, 3, 1).reshape(bsz, NSA_KV, nc, CMP_STRIDE * NSA_DK)

    pe = cmp_pe.reshape(2, 2, CMP_STRIDE * NSA_DK)
    w2kt = cmp_w2[0].T.astype(BF16)
    w2v = jnp.pad(cmp_w2[1], ((0, 0), (0, LANES - NSA_DK))).astype(BF16)
    kct, vcp = _nsa_compress(grouped(kc0), grouped(kc0 + NSA_KVW), pe, cmp_w1.astype(BF16), w2kt, w2v)
    tok = jnp.arange(t)
    e_mat = (jnp.arange(N_SEL_PAD)[:, None] == (tok // SEL_LEN)[None, :]).astype(BF16)
    cstart = jnp.arange(nc) * CMP_STRIDE
    sstart = jnp.arange(N_SEL_PAD) * SEL_LEN
    ov_mat = ((cstart[None, :] < sstart[:, None] + SEL_LEN)
              & (cstart[None, :] + CMP_LEN > sstart[:, None])
              & (jnp.arange(nc)[None, :] < nc - 1)).astype(BF16)
    return _nsa_attention(oa, ob, kct, vcp, e_mat, ov_mat, t)


def _sb_mixer(x, g, scl, shift, w_in):
    t = x.shape[1]
    wa, cs, b0, pv, wb, cb = _sb_weights(w_in)
    oa, ob = _inproj(x, g, scl, shift, wa, cs, b0, pv, wb, cb)
    return _sb_attention(oa, ob, t)


def _diff_mixer(x, g, scl, shift, w_in, lam, subln_g, layer_idx):
    t = x.shape[1]
    wa, cs, b0, pv, wb, cb = _diff_weights(w_in)
    oa, ob = _inproj(x, g, scl, shift, wa, cs, b0, pv, wb, cb)
    lam_init = 0.8 - 0.6 * math.exp(-0.3 * layer_idx)
    return _diff_attention(oa, ob, lam, subln_g, lam_init, t)


def kernel(x, c, ada_w, ada_b, norm_g, ffn_w1, ffn_w2, nsa_w_in, nsa_cmp_pe, nsa_cmp_w1, nsa_cmp_w2,
           nsa_w_out, sb_w_in, sb_w_out, diff_w_in, diff_lam, diff_subln_g, diff_w_out):
    bsz, t, d = x.shape
    assert d == D_MODEL and t % ROW_TILE == 0 and t >= WIN + 128 and t <= N_SEL_PAD * SEL_LEN
    mod = _modulation(c, ada_w, ada_b)
    for i in range(DEPTH):
        def mods(sidx):
            return tuple(mod[i, :, sidx, k][:, None, :] for k in range(3))

        shift, scl, gate = mods(0)
        x = _ffn(x, norm_g[i, 0], scl, shift, gate, norm_g[i, 1],
                 ffn_w1[i, 0].astype(BF16), ffn_w2[i, 0].astype(BF16))
        shift, scl, gate = mods(1)
        kind, j = i % N_MIXERS, i // N_MIXERS
        if kind == 0:
            o = _nsa_mixer(x, norm_g[i, 2], scl, shift, nsa_w_in[j], nsa_cmp_pe[j], nsa_cmp_w1[j],
                           nsa_cmp_w2[j])
            w_out = nsa_w_out[j]
        elif kind == 1:
            o = _sb_mixer(x, norm_g[i, 2], scl, shift, sb_w_in[j])
            w_out = sb_w_out[j]
        else:
            o = _diff_mixer(x, norm_g[i, 2], scl, shift, diff_w_in[j], diff_lam[j], diff_subln_g[j], i)
            w_out = diff_w_out[j]
        x = _outproj(o, w_out.astype(BF16), x, gate, norm_g[i, 3])
        shift, scl, gate = mods(2)
        x = _ffn(x, norm_g[i, 4], scl, shift, gate, norm_g[i, 5],
                 ffn_w1[i, 1].astype(BF16), ffn_w2[i, 1].astype(BF16))
    return x
```

```python
import functools
import math

import jax
import jax.numpy as jnp
import numpy as np
from jax import lax
from jax.experimental import pallas as pl
from jax.experimental.pallas import tpu as pltpu

F32 = jnp.float32
BF16 = jnp.bfloat16

D_MODEL = 1024
DEPTH = 4
N_MIXERS = 3
EPS = 1e-6
NEG = -1e30
FORCE = 1e4
LOG2E = 1.4426950408889634
NSA_HEADS = 16
NSA_DK = 64
NSA_KV = 4
NSA_REP = NSA_HEADS // NSA_KV
CMP_LEN = 32
CMP_STRIDE = 16
CMP_HID = 256
SEL_LEN = 64
SEL_TOPK = 16
WIN = 512
NSA_QW = NSA_HEADS * NSA_DK
NSA_KVW = NSA_KV * NSA_DK
N_SEL_PAD = 128
SB_HEADS = 16
SB_DH = D_MODEL // SB_HEADS
DIFF_HEADS = 8
DIFF_DH = D_MODEL // (2 * DIFF_HEADS)
D_FF = 2816

LANES = 128
MOD_ROWS = 8
ROW_TILE = 512
FF_CHUNK = 256
DIFF_TQ, DIFF_TK = 512, 512
SB_TQ, SB_TK = 512, 256
SB_LINEAR_FROM = 64.0
SB_DEAD = 192.0
DIFF_DEAD = 192.0
ALIBI_COLS = 7
NSA_TQ, NSA_TK = 256, 512
CMP_CHUNK = 128
VMEM_LIMIT = 56 * 1024 * 1024


def _cparams(sem):
    return pltpu.CompilerParams(dimension_semantics=sem, vmem_limit_bytes=VMEM_LIMIT)


def _const_spec(shape):
    n = len(shape)
    return pl.BlockSpec(shape, lambda *_: (0,) * n)


def _rms(x, g):
    return x * lax.rsqrt(jnp.mean(x * x, axis=-1, keepdims=True) + EPS) * g


def _dot(a, b):
    return jnp.dot(a, b, preferred_element_type=F32)


def _dot_nt(a, b):
    return lax.dot_general(a, b, (((1,), (1,)), ((), ())), preferred_element_type=F32)


def _mod_kernel(c_ref, w_ref, b_ref, o_ref):
    c = c_ref[...]
    cond = c * jax.nn.sigmoid(c)
    o_ref[0] = _dot(cond.astype(BF16), w_ref[0].astype(BF16)) + b_ref[0]


def _modulation(c, ada_w, ada_b):
    depth, d, n = ada_w.shape
    b = c.shape[0]
    rows = MOD_ROWS
    c_pad = jnp.zeros((rows, d), F32).at[:b].set(c)
    nc = 1152
    out = pl.pallas_call(
        _mod_kernel,
        grid=(depth, n // nc),
        in_specs=[
            pl.BlockSpec((rows, d), lambda i, j: (0, 0)),
            pl.BlockSpec((1, d, nc), lambda i, j: (i, 0, j)),
            pl.BlockSpec((1, 1, nc), lambda i, j: (i, 0, j)),
        ],
        out_specs=pl.BlockSpec((1, rows, nc), lambda i, j: (i, 0, j)),
        out_shape=jax.ShapeDtypeStruct((depth, rows, n), F32),
        compiler_params=_cparams(("arbitrary", "arbitrary")),
        name="adaln_mod",
    )(c_pad, ada_w, ada_b.reshape(depth, 1, n))
    return out


def _mod_spec(sel):
    layer, blk = sel
    return pl.BlockSpec((1, MOD_ROWS, D_MODEL), lambda b, i: (layer, 0, blk))


def _mod_row(ref):
    return ref[0, pl.ds(pl.program_id(0), 1), :]


def _inproj_kernel(*refs, tt, nca, ncb, has_b):
    if has_b:
        (x_ref, g_ref, scl_ref, sh_ref, wa_ref, cs_ref, b0_ref, pv_ref, wb_ref, cb_ref,
         oa_ref, ob_ref) = refs
    else:
        x_ref, g_ref, scl_ref, sh_ref, wa_ref, cs_ref, b0_ref, pv_ref, oa_ref = refs
    x = x_ref[0]
    hh = (_rms(x, g_ref[...]) * (1.0 + _mod_row(scl_ref)) + _mod_row(sh_ref)).astype(BF16)
    t0 = pl.program_id(1) * tt
    pos = (t0 + lax.broadcasted_iota(jnp.int32, (tt, 1), 0)).astype(F32)
    na = wa_ref.shape[1]
    for c in range(na // nca):
        sl = slice(c * nca, (c + 1) * nca)
        acc = _dot(hh, wa_ref[:, sl])
        y = acc * cs_ref[:, sl] + b0_ref[:, sl] + pos * pv_ref[:, sl]
        oa_ref[0, :, sl] = y.astype(oa_ref.dtype)
    if has_b:
        tpos = t0 + lax.broadcasted_iota(jnp.int32, (1, tt), 1)
        phi = ((tpos >> 7) << 7).astype(F32)
        plo = (tpos & 127).astype(F32)
        nb = wb_ref.shape[0]
        for c in range(nb // ncb):
            sl = slice(c * ncb, (c + 1) * ncb)
            acc = _dot_nt(wb_ref[sl, :], hh)
            cc = cb_ref[sl, :]
            y = acc + cc[:, 0:1] * phi + cc[:, 1:2] * plo + cc[:, 2:3]
            ob_ref[0, sl, :] = y.astype(ob_ref.dtype)


def _inproj(x, g, mod, scl, shift, wa, cs, b0, pv, wb=None, cb=None):
    bsz, t, d = x.shape
    tt = min(ROW_TILE, t)
    na = wa.shape[1]
    has_b = wb is not None
    in_specs = [
        pl.BlockSpec((1, tt, d), lambda b, i: (b, i, 0)),
        _const_spec((1, d)),
        _mod_spec(scl),
        _mod_spec(shift),
        _const_spec((d, na)),
        _const_spec((1, na)),
        _const_spec((1, na)),
        _const_spec((1, na)),
    ]
    args = [x, g.reshape(1, d), mod, mod, wa, cs, b0, pv]
    out_specs = [pl.BlockSpec((1, tt, na), lambda b, i: (b, i, 0))]
    out_shape = [jax.ShapeDtypeStruct((bsz, t, na), BF16)]
    ncb = 512
    if has_b:
        nb = wb.shape[0]
        in_specs += [_const_spec((nb, d)), _const_spec((nb, 8))]
        args += [wb, cb]
        out_specs.append(pl.BlockSpec((1, nb, tt), lambda b, i: (b, 0, i)))
        out_shape.append(jax.ShapeDtypeStruct((bsz, nb, t), BF16))
    outs = pl.pallas_call(
        functools.partial(_inproj_kernel, tt=tt, nca=512, ncb=ncb, has_b=has_b),
        grid=(bsz, t // tt),
        in_specs=in_specs,
        out_specs=out_specs,
        out_shape=out_shape,
        compiler_params=_cparams(("arbitrary", "arbitrary")),
        name="mixer_inproj",
    )(*args)
    return outs if has_b else (outs[0], None)


def _outproj_kernel(o_ref, w_ref, x_ref, gate_ref, g2_ref, out_ref):
    y = _dot(o_ref[0], w_ref[...])
    out_ref[0] = x_ref[0] + _mod_row(gate_ref) * _rms(y, g2_ref[...])


def _outproj(o, w, x, mod, gate, g2):
    bsz, t, d = x.shape
    tt = min(ROW_TILE, t)
    k = o.shape[2]
    return pl.pallas_call(
        _outproj_kernel,
        grid=(bsz, t // tt),
        in_specs=[
            pl.BlockSpec((1, tt, k), lambda b, i: (b, i, 0)),
            _const_spec((k, d)),
            pl.BlockSpec((1, tt, d), lambda b, i: (b, i, 0)),
            _mod_spec(gate),
            _const_spec((1, d)),
        ],
        out_specs=pl.BlockSpec((1, tt, d), lambda b, i: (b, i, 0)),
        out_shape=jax.ShapeDtypeStruct((bsz, t, d), F32),
        compiler_params=_cparams(("arbitrary", "arbitrary")),
        name="mixer_outproj",
    )(o, w, x, mod, g2.reshape(1, d))


def _ffn_kernel(x_ref, g1_ref, scl_ref, sh_ref, gate_ref, g2_ref, w1_ref, w2_ref, o_ref, acc_ref):
    x = x_ref[0]
    hh = (_rms(x, g1_ref[...]) * (1.0 + _mod_row(scl_ref)) + _mod_row(sh_ref)).astype(BF16)
    dff = w2_ref.shape[0]
    for c in range(dff // FF_CHUNK):
        lo = c * FF_CHUNK
        gt = _dot(hh, w1_ref[:, lo:lo + FF_CHUNK])
        up = _dot(hh, w1_ref[:, dff + lo:dff + lo + FF_CHUNK])
        act = (gt * jax.nn.sigmoid(gt) * up).astype(BF16)
        part = _dot(act, w2_ref[lo:lo + FF_CHUNK, :])
        if c == 0:
            acc_ref[...] = part
        else:
            acc_ref[...] += part
    o_ref[0] = x + 0.5 * _mod_row(gate_ref) * _rms(acc_ref[...], g2_ref[...])


def _ffn(x, g1, mod, scl, shift, gate, g2, w1, w2):
    bsz, t, d = x.shape
    tt = min(ROW_TILE, t)
    return pl.pallas_call(
        _ffn_kernel,
        grid=(bsz, t // tt),
        in_specs=[
            pl.BlockSpec((1, tt, d), lambda b, i: (b, i, 0)),
            _const_spec((1, d)), _mod_spec(scl), _mod_spec(shift), _mod_spec(gate), _const_spec((1, d)),
            _const_spec(w1.shape), _const_spec(w2.shape),
        ],
        out_specs=pl.BlockSpec((1, tt, d), lambda b, i: (b, i, 0)),
        out_shape=jax.ShapeDtypeStruct((bsz, t, d), F32),
        scratch_shapes=[pltpu.VMEM((tt, d), F32)],
        compiler_params=_cparams(("arbitrary", "arbitrary")),
        name="ffn",
    )(x, g1.reshape(1, d), mod, mod, mod, g2.reshape(1, d), w1, w2)


def _online_step(s, m, l, acc, v):
    m_new = jnp.maximum(m, jnp.max(s, axis=-1, keepdims=True))
    alpha = jnp.exp2(m - m_new)
    p = jnp.exp2(s - m_new)
    l = alpha * l + jnp.sum(p, axis=-1, keepdims=True)
    acc = alpha * acc + _dot(p.astype(BF16), v)
    return m_new, l, acc


def _softmax_init(rows, width):
    return (jnp.full((rows, 1), NEG, F32), jnp.zeros((rows, 1), F32), jnp.zeros((rows, width), F32))


def _gelu_tanh(x):
    return 0.5 * x * (1.0 + jnp.tanh(math.sqrt(2.0 / math.pi) * (x + 0.044715 * (x * x * x))))


def _cmp_kernel(xk_ref, xv_ref, pe_ref, w1_ref, w2kt_ref, w2v_ref, kt_ref, v_ref):
    nc = xk_ref.shape[2]
    half = xk_ref.shape[3]

    def hidden(x_ref, j):
        x = x_ref[0, 0].astype(F32)
        top = _dot((x + pe_ref[j, 0:1, :]).astype(BF16), w1_ref[j, 0:half, :])
        bot = _dot((x + pe_ref[j, 1:2, :]).astype(BF16), w1_ref[j, half:2 * half, :])
        return _gelu_tanh(top + pltpu.roll(bot, nc - 1, axis=0)).astype(BF16)

    hk = hidden(xk_ref, 0)
    hv = hidden(xv_ref, 1)
    kt_ref[0, 0, 0:NSA_DK, :] = _dot_nt(w2kt_ref[...], hk).astype(BF16)
    n = lax.broadcasted_iota(jnp.int32, (1, nc), 1)
    end = n * CMP_STRIDE + (CMP_LEN - 1)
    phi = ((end >> 7) << 7).astype(F32)
    plo = (end & 127).astype(F32)
    r = lax.broadcasted_iota(jnp.int32, (LANES - NSA_DK, 1), 0)
    aug = jnp.where(r < 3, phi, jnp.where(r < 6, plo, jnp.where(r == 6, 1.0, 0.0)))
    kt_ref[0, 0, NSA_DK:LANES, :] = aug.astype(BF16)
    v_ref[0, 0] = _dot(hv, w2v_ref[...]).astype(BF16)


def _nsa_compress(xk, xv, pe, w1, w2kt, w2v):
    bsz, g, nc, half = xk.shape
    xspec = pl.BlockSpec((1, 1, nc, half), lambda b, j: (b, j, 0, 0))
    return pl.pallas_call(
        _cmp_kernel,
        grid=(bsz, g),
        in_specs=[xspec, xspec, _const_spec(pe.shape), _const_spec(w1.shape), _const_spec(w2kt.shape),
                  _const_spec(w2v.shape)],
        out_specs=[pl.BlockSpec((1, 1, LANES, nc), lambda b, j: (b, j, 0, 0)),
                   pl.BlockSpec((1, 1, nc, LANES), lambda b, j: (b, j, 0, 0))],
        out_shape=[jax.ShapeDtypeStruct((bsz, g, LANES, nc), BF16),
                   jax.ShapeDtypeStruct((bsz, g, nc, LANES), BF16)],
        compiler_params=_cparams(("arbitrary", "arbitrary")),
        name="nsa_compress",
    )(xk, xv, pe, w1, w2kt, w2v)


def _split3(x):
    hi = x.astype(BF16)
    r1 = x - hi.astype(F32)
    mid = r1.astype(BF16)
    lo = (r1 - mid.astype(F32)).astype(BF16)
    return hi, mid, lo


def _nsa_attn_kernel(qa_ref, kct_ref, vc_ref, kst_ref, vs_ref, kwt_ref, vw_ref, gt_ref, e_ref, ov_ref,
                     o_ref, tiles_ref, *, tq, tk):
    i = pl.program_id(2)
    q0 = i * tq
    rows = NSA_REP * tq
    qa = qa_ref[0]
    qs = jnp.concatenate([qa[:, r * LANES:(r + 1) * LANES] for r in range(NSA_REP)], axis=0)
    t_row = q0 + (lax.broadcasted_iota(jnp.int32, (rows, 1), 0) & (tq - 1))

    nc = kct_ref.shape[3]

    def compressed(width):
        s = _dot(qs, kct_ref[0, 0, :, 0:width])
        cend = lax.broadcasted_iota(jnp.int32, (1, width), 1) * CMP_STRIDE + (CMP_LEN - 1)
        valid = cend <= t_row
        s = jnp.where(valid, s, NEG)
        e = jnp.where(valid, jnp.exp2(s - jnp.max(s, axis=-1, keepdims=True)), 0.0)
        l = jnp.sum(e, axis=-1, keepdims=True)
        p = e / jnp.where(l > 0.0, l, 1.0)
        o = _dot(p.astype(BF16), vc_ref[0, 0, 0:width, :])
        p4 = p[0:tq]
        for r in range(1, NSA_REP):
            p4 = p4 + p[r * tq:(r + 1) * tq]
        ov = ov_ref[:, 0:width]
        return o, sum(_dot_nt(ov, part) for part in _split3(p4))

    chunk = min(CMP_CHUNK, nc)
    widths = list(range(chunk, nc + 1, chunk))
    n_ended = jnp.maximum(q0 + tq - CMP_LEN, 0) // CMP_STRIDE + 1
    o_cmp, imp = lax.switch((n_ended - 1) // chunk, [functools.partial(compressed, w) for w in widths])

    wk = WIN + tq
    st = pl.multiple_of(jnp.maximum(q0 - WIN, 0), LANES)
    col = st + lax.broadcasted_iota(jnp.int32, (1, wk), 1)
    valid = (col <= t_row) & (col > t_row - WIN)
    s = jnp.where(valid, _dot(qs, kwt_ref[0, :, pl.ds(st, wk)]), NEG)
    e = jnp.exp2(s - jnp.max(s, axis=-1, keepdims=True))
    o_win = _dot(e.astype(BF16), vw_ref[0, pl.ds(st, wk), :]) / jnp.sum(e, axis=-1, keepdims=True)

    cur = (q0 + lax.broadcasted_iota(jnp.int32, (1, tq), 1)) >> 6
    sid = lax.broadcasted_iota(jnp.int32, (N_SEL_PAD, 1), 0)
    forced = (sid == 0) | (sid == cur) | (sid == cur - 1)
    work = jnp.where(forced | (sid > cur), -3.0e38, imp)
    selb = jnp.where(forced, 0.0, NEG)
    for _ in range(SEL_TOPK - 3):
        mx = jnp.max(work, axis=0, keepdims=True)
        first = jnp.min(jnp.where(work == mx, sid, N_SEL_PAD), axis=0, keepdims=True)
        pick = sid == first
        selb = jnp.where(pick, 0.0, selb)
        work = jnp.where(pick, -3.0e38, work)
    selb = selb.T

    jd = q0 // tk
    n_tiles = kst_ref.shape[2] // tk
    any_sel = jnp.max(jnp.where(selb == 0.0, 1.0, 0.0), axis=0, keepdims=True)
    blk_tile = (lax.broadcasted_iota(jnp.int32, (N_SEL_PAD, LANES), 0) // (tk // SEL_LEN)
                == lax.broadcasted_iota(jnp.int32, (N_SEL_PAD, LANES), 1)).astype(BF16)
    tile_cnt = _dot(jnp.broadcast_to(any_sel, (16, N_SEL_PAD)).astype(BF16), blk_tile)
    n_act = jnp.int32(0)
    for j in range(n_tiles):
        tiles_ref[n_act] = j
        n_act = n_act + ((tile_cnt[0, j] > 0.0) & (j < jd)).astype(jnp.int32)

    selb4 = jnp.concatenate([selb.astype(BF16)] * NSA_REP, axis=0)
    qsel = jnp.concatenate([qs, selb4], axis=1)

    def sel_scores(st, width):
        return _dot(qsel, jnp.concatenate([kst_ref[0, :, pl.ds(st, width)], e_ref[:, pl.ds(st, width)]], axis=0))

    def sel_body(it, carry):
        st = pl.multiple_of(tiles_ref[it] * tk, tk)
        return _online_step(sel_scores(st, tk), *carry, vs_ref[0, pl.ds(st, tk), :])

    carry = lax.fori_loop(0, n_act, sel_body, _softmax_init(rows, LANES))
    st = pl.multiple_of(jd * tk, tk)

    def diagonal(width):
        col = st + lax.broadcasted_iota(jnp.int32, (1, width), 1)
        s = jnp.where(col <= t_row, sel_scores(st, width), NEG)
        _, l, acc = _online_step(s, *carry, vs_ref[0, pl.ds(st, width), :])
        return acc / l

    o_sel = lax.switch((q0 - st) // tq, [functools.partial(diagonal, w) for w in range(tq, tk + 1, tq)])

    gs = jax.nn.sigmoid(gt_ref[0].astype(F32))
    outs = []
    for r in range(NSA_REP):
        sl = slice(r * tq, (r + 1) * tq)
        outs.append(gs[:, 3 * r:3 * r + 1] * o_cmp[sl] + gs[:, 3 * r + 1:3 * r + 2] * o_sel[sl]
                    + gs[:, 3 * r + 2:3 * r + 3] * o_win[sl])
    pairs = [outs[2 * a] + pltpu.roll(outs[2 * a + 1], NSA_DK, axis=1) for a in range(NSA_REP // 2)]
    o_ref[0] = jnp.concatenate(pairs, axis=1).astype(o_ref.dtype)


def _nsa_attention(oa, ob, kct, vcp, e_mat, ov_mat, t):
    bsz = oa.shape[0]
    tq, tk = NSA_TQ, NSA_TK
    qw = NSA_REP * LANES
    vs_blk = (NSA_HEADS * LANES + 2 * NSA_KVW) // LANES
    vw_blk = vs_blk + NSA_KV
    gt_blk = vw_blk + NSA_KV
    nc = kct.shape[3]
    return pl.pallas_call(
        functools.partial(_nsa_attn_kernel, tq=tq, tk=tk),
        grid=(bsz, NSA_KV, t // tq),
        in_specs=[
            pl.BlockSpec((1, tq, qw), lambda b, g, i: (b, i, g)),
            pl.BlockSpec((1, 1, LANES, nc), lambda b, g, i: (b, g, 0, 0)),
            pl.BlockSpec((1, 1, nc, LANES), lambda b, g, i: (b, g, 0, 0)),
            pl.BlockSpec((1, LANES, t), lambda b, g, i: (b, g, 0)),
            pl.BlockSpec((1, t, LANES), lambda b, g, i: (b, 0, vs_blk + g)),
            pl.BlockSpec((1, LANES, t), lambda b, g, i: (b, NSA_KV + g, 0)),
            pl.BlockSpec((1, t, LANES), lambda b, g, i: (b, 0, vw_blk + g)),
            pl.BlockSpec((1, tq, LANES), lambda b, g, i: (b, i, gt_blk + g)),
            _const_spec(e_mat.shape),
            _const_spec(ov_mat.shape),
        ],
        out_specs=pl.BlockSpec((1, tq, NSA_REP * NSA_DK), lambda b, g, i: (b, i, g)),
        out_shape=jax.ShapeDtypeStruct((bsz, t, NSA_QW), BF16),
        scratch_shapes=[pltpu.SMEM((t // tk,), jnp.int32)],
        compiler_params=_cparams(("arbitrary", "arbitrary", "arbitrary")),
        name="nsa_attention",
    )(oa, kct, vcp, ob, oa, ob, oa, oa, e_mat, ov_mat)


def _sb_kernel(q_ref, kt_ref, v_ref, o_ref, rem_ref, acc_ref, *, tq, tk):
    i = pl.program_id(2)
    q0 = i * tq
    q = q_ref[0]
    lane = lax.broadcasted_iota(jnp.int32, (1, LANES), 1)
    zero = jnp.zeros_like(q)
    qs = jnp.concatenate([jnp.where(lane < SB_DH, q, zero), jnp.where(lane >= SB_DH, q, zero)], axis=0)
    t_row = q0 + (lax.broadcasted_iota(jnp.int32, (2 * tq, 1), 0) & (tq - 1))
    upper = (lax.broadcasted_iota(jnp.int32, (tk, tk), 0) > lax.broadcasted_iota(jnp.int32, (tk, tk), 1)
             ).astype(BF16)
    rem_ref[...] = jnp.zeros(rem_ref.shape, F32)
    acc_ref[...] = jnp.zeros(acc_ref.shape, F32)

    def tiles(js, masked):
        sts = [pl.multiple_of(j * tk, tk) for j in js]
        zs = [_dot(qs, kt_ref[0, :, pl.ds(st, tk)]) for st in sts]
        costs, log_bs, befores = [], [], []
        for st, z in zip(sts, zs):
            cost = jnp.where(z > SB_LINEAR_FROM, z, jnp.log(1.0 + jnp.exp2(z)) * LOG2E)
            log_bs.append(z - cost)
            before = None
            if masked:
                before = (st + lax.broadcasted_iota(jnp.int32, (1, tk), 1)) < t_row
                cost = jnp.where(before, cost, 0.0)
            costs.append(cost.astype(BF16))
            befores.append(before)
        rems = [_dot(c, upper) for c in costs]
        rem_right = rem_ref[...]
        probs = []
        for log_b, c, rem, before in zip(log_bs, costs, rems, befores):
            ex = log_b - (rem + rem_right)
            if masked:
                ex = jnp.where(before, ex, NEG)
            probs.append(jnp.exp2(ex).astype(BF16))
            rem_right = rem_right + (rem[:, 0:1] + c[:, 0:1].astype(F32))
        rem_ref[...] = rem_right
        acc = acc_ref[...]
        for st, p in zip(sts, probs):
            acc = acc + _dot(p, v_ref[0, pl.ds(st, tk), :])
        acc_ref[...] = acc
        return jnp.min(rem_right)

    per_step = tq // tk
    n_full = q0 // tk
    spent = tiles([n_full + per_step - 1 - d for d in range(per_step)], True)

    def go_on(c):
        return (c[0] < n_full // per_step) & (c[1] < SB_DEAD)

    def body(c):
        it = c[0]
        return it + 1, tiles([n_full - 1 - per_step * it - d for d in range(per_step)], False)

    lax.while_loop(go_on, body, (jnp.int32(0), spent))
    acc = acc_ref[...]
    o_ref[0] = jnp.where(lane < SB_DH, acc[0:tq], acc[tq:2 * tq]).astype(o_ref.dtype)


def _sb_attention(oa, ob, t):
    bsz = oa.shape[0]
    tq, tk = SB_TQ, SB_TK
    npair = SB_HEADS // 2
    return pl.pallas_call(
        functools.partial(_sb_kernel, tq=tq, tk=tk),
        grid=(bsz, npair, t // tq),
        in_specs=[
            pl.BlockSpec((1, tq, LANES), lambda b, h, i: (b, i, h)),
            pl.BlockSpec((1, LANES, t), lambda b, h, i: (b, h, 0)),
            pl.BlockSpec((1, t, LANES), lambda b, h, i: (b, 0, npair + h)),
        ],
        out_specs=pl.BlockSpec((1, tq, LANES), lambda b, h, i: (b, i, h)),
        out_shape=jax.ShapeDtypeStruct((bsz, t, D_MODEL), BF16),
        scratch_shapes=[pltpu.VMEM((2 * tq, 1), F32), pltpu.VMEM((2 * tq, LANES), F32)],
        compiler_params=_cparams(("arbitrary", "arbitrary", "arbitrary")),
        name="sb_attention",
    )(oa, ob, oa)


def _diff_kernel(slope_ref, lam_ref, g_ref, qa_ref, kt_ref, v_ref, o_ref, kmax_ref, *, tq, tk, lam_init):
    h = pl.program_id(1)
    i = pl.program_id(2)
    q0 = i * tq
    n_tiles = kt_ref.shape[2] // tk

    @pl.when(i == 0)
    def _():
        run = jnp.float32(0.0)
        for j in range(n_tiles):
            kf = kt_ref[0, 0:LANES, j * tk:(j + 1) * tk].astype(F32)
            sq = kf * kf
            norms = jnp.sqrt(jnp.maximum(jnp.sum(sq[0:DIFF_DH], axis=0, keepdims=True),
                                         jnp.sum(sq[DIFF_DH:LANES], axis=0, keepdims=True)))
            run = jnp.maximum(run, jnp.max(norms))
            kmax_ref[j] = run

    qa = qa_ref[0]
    lane = lax.broadcasted_iota(jnp.int32, (1, 2 * LANES), 1)
    zero = jnp.zeros_like(qa)
    qs = jnp.concatenate(
        [jnp.where(((lane >= m_ * DIFF_DH) & (lane < (m_ + 1) * DIFF_DH)) | (lane >= LANES), qa, zero)
         for m_ in range(2)], axis=0)
    t_row = q0 + (lax.broadcasted_iota(jnp.int32, (2 * tq, 1), 0) & (tq - 1))

    def scores(j):
        return _dot(qs, kt_ref[0, :, pl.ds(pl.multiple_of(j * tk, tk), tk)])

    def values(j):
        return v_ref[0, pl.ds(pl.multiple_of(j * tk, tk), tk), :]

    jd = q0 // tk
    col = jd * tk + lax.broadcasted_iota(jnp.int32, (1, tk), 1)
    state = _online_step(jnp.where(col <= t_row, scores(jd), NEG), *_softmax_init(2 * tq, LANES), values(jd))
    qf = jnp.where(lane < LANES, qs, jnp.zeros_like(qs)).astype(F32)
    q_norm = jnp.max(jnp.sqrt(jnp.sum(qf * qf, axis=-1, keepdims=True)))
    row_col = 2 * DIFF_DH + ALIBI_COLS - 1
    row_const = qs[:, row_col:row_col + 1].astype(F32)
    slope = slope_ref[h]

    def live(c):
        j = jnp.maximum(c[0], 0)
        bound = q_norm * kmax_ref[j] * 1.001 + slope * ((j + 1) * tk - 1).astype(F32)
        return (c[0] >= 0) & (bound - c[4] > -DIFF_DEAD)

    def body(c):
        m, l, acc = _online_step(scores(c[0]), c[1], c[2], c[3], values(c[0]))
        return c[0] - 1, m, l, acc, jnp.min(m - row_const)

    _, _, l, acc, _ = lax.while_loop(live, body, (jd - 1, *state, jnp.min(state[0] - row_const)))
    out = acc / l
    lam = lam_ref[...]
    lam_full = (jnp.exp(jnp.sum(lam[0:1] * lam[1:2], axis=-1, keepdims=True))
                - jnp.exp(jnp.sum(lam[2:3] * lam[3:4], axis=-1, keepdims=True)) + lam_init)
    o = out[0:tq] - lam_full * out[tq:2 * tq]
    o_ref[0] = (_rms(o, g_ref[...]) * (1.0 - lam_init)).astype(o_ref.dtype)


def _diff_attention(oa, ob, lam, subln_g, lam_init, t):
    bsz = oa.shape[0]
    tq, tk = DIFF_TQ, DIFF_TK
    v_blk = DIFF_HEADS * 2
    slopes = _alibi_slopes(DIFF_HEADS) * LOG2E
    return pl.pallas_call(
        functools.partial(_diff_kernel, tq=tq, tk=tk, lam_init=lam_init),
        grid=(bsz, DIFF_HEADS, t // tq),
        in_specs=[
            pl.BlockSpec(memory_space=pltpu.SMEM),
            _const_spec(lam.shape),
            _const_spec((1, 2 * DIFF_DH)),
            pl.BlockSpec((1, tq, 2 * LANES), lambda b, h, i: (b, i, h)),
            pl.BlockSpec((1, 2 * LANES, t), lambda b, h, i: (b, h, 0)),
            pl.BlockSpec((1, t, LANES), lambda b, h, i: (b, 0, v_blk + h)),
        ],
        out_specs=pl.BlockSpec((1, tq, LANES), lambda b, h, i: (b, i, h)),
        out_shape=jax.ShapeDtypeStruct((bsz, t, D_MODEL), BF16),
        scratch_shapes=[pltpu.SMEM((t // tk,), F32)],
        compiler_params=_cparams(("arbitrary", "arbitrary", "arbitrary")),
        name="diff_attention",
    )(slopes, lam, subln_g.reshape(1, 2 * DIFF_DH), oa, ob, oa)


def _alibi_slopes(n):
    return jnp.exp2(-8.0 * jnp.arange(1, n + 1, dtype=F32) / n)


def _alibi_query_cols(slopes, width, first):
    n = slopes.shape[0]
    sl2 = slopes * LOG2E
    parts = jnp.stack([p.astype(F32) for p in _split3(sl2)], axis=1)
    b0 = jnp.pad(jnp.concatenate([parts, parts], axis=1), ((0, 0), (first, width - first - 6)))
    pv = jnp.pad(-sl2[:, None], ((0, 0), (first + 6, width - first - 7)))
    return b0.reshape(1, n * width), pv.reshape(1, n * width)


def _alibi_key_rows(nblocks, width, first):
    c = np.zeros((nblocks, width, 8), np.float32)
    c[:, first:first + 3, 0] = 1.0
    c[:, first + 3:first + 6, 1] = 1.0
    c[:, first + 6, 2] = 1.0
    return jnp.asarray(c.reshape(nblocks * width, 8))


def _pad_blocks(w, nblocks, width):
    d = w.shape[0]
    k = w.shape[1] // nblocks
    return jnp.pad(w.reshape(d, nblocks, k), ((0, 0), (0, 0), (0, width - k))).reshape(d, nblocks * width)


def _nsa_weights(w_in):
    d = w_in.shape[0]
    bounds = [0, NSA_QW] + [NSA_QW + (k + 1) * NSA_KVW for k in range(6)]
    wq, wkc, wvc, wks, wvs, wkw, wvw = [w_in[:, bounds[k]:bounds[k + 1]] for k in range(7)]
    wg = w_in[:, bounds[7]:]
    wa = jnp.concatenate([
        _pad_blocks(wq, NSA_HEADS, LANES), wkc, wvc, _pad_blocks(wvs, NSA_KV, LANES),
        _pad_blocks(wvw, NSA_KV, LANES), _pad_blocks(wg, NSA_KV, LANES)], axis=1).astype(BF16)
    na = wa.shape[1]
    qcols = NSA_HEADS * LANES
    scale = NSA_DK ** -0.5 * LOG2E
    cs_q = np.where(np.arange(LANES) < NSA_DK, scale, 1.0).astype(np.float32)
    cs = jnp.asarray(np.concatenate([np.tile(cs_q, NSA_HEADS), np.ones((na - qcols,), np.float32)]).reshape(1, na))
    b0q, pvq = _alibi_query_cols(_alibi_slopes(NSA_HEADS), LANES, NSA_DK)
    b0 = jnp.pad(b0q, ((0, 0), (0, na - qcols)))
    pv = jnp.pad(pvq, ((0, 0), (0, na - qcols)))
    wb = jnp.concatenate([_pad_blocks(wks, NSA_KV, LANES), _pad_blocks(wkw, NSA_KV, LANES)], axis=1).T
    cb = _alibi_key_rows(2 * NSA_KV, LANES, NSA_DK)
    return wa, cs, b0, pv, wb.astype(BF16), cb


def _sb_weights(w_in):
    d = w_in.shape[0]
    wq, wk, wv = jnp.split(w_in, 3, axis=1)
    wa = jnp.concatenate([wq, wv], axis=1).astype(BF16)
    na = wa.shape[1]
    cs = jnp.asarray(np.concatenate([np.full((d,), SB_DH ** -0.5 * LOG2E, np.float32),
                                     np.ones((d,), np.float32)]).reshape(1, na))
    zero = jnp.asarray(np.zeros((1, na), np.float32))
    return wa, cs, zero, zero, wk.T.astype(BF16), jnp.asarray(np.zeros((d, 8), np.float32))


def _diff_weights(w_in):
    d = w_in.shape[0]
    wq, wk, wv = jnp.split(w_in, 3, axis=1)
    width = 2 * LANES
    wa = jnp.concatenate([_pad_blocks(wq, DIFF_HEADS, width), wv], axis=1).astype(BF16)
    na = wa.shape[1]
    qcols = DIFF_HEADS * width
    scale = DIFF_DH ** -0.5 * LOG2E
    cs_q = np.where(np.arange(width) < 2 * DIFF_DH, scale, 1.0).astype(np.float32)
    cs = jnp.asarray(np.concatenate([np.tile(cs_q, DIFF_HEADS), np.ones((na - qcols,), np.float32)]).reshape(1, na))
    b0q, pvq = _alibi_query_cols(_alibi_slopes(DIFF_HEADS), width, 2 * DIFF_DH)
    b0 = jnp.pad(b0q, ((0, 0), (0, na - qcols)))
    pv = jnp.pad(pvq, ((0, 0), (0, na - qcols)))
    wb = _pad_blocks(wk, DIFF_HEADS, width).T.astype(BF16)
    cb = _alibi_key_rows(DIFF_HEADS, width, 2 * DIFF_DH)
    return wa, cs, b0, pv, wb, cb


def _nsa_mixer(x, g, mod, scl, shift, w_in, cmp_pe, cmp_w1, cmp_w2):
    bsz, t, _ = x.shape
    wa, cs, b0, pv, wb, cb = _nsa_weights(w_in)
    oa, ob = _inproj(x, g, mod, scl, shift, wa, cs, b0, pv, wb, cb)
    nc = t // CMP_STRIDE
    kc0 = NSA_HEADS * LANES

    def grouped(lo):
        a = oa[:, :, lo:lo + NSA_KVW].reshape(bsz, nc, CMP_STRIDE, NSA_KV, NSA_DK)
        return jnp.moveaxis(a, 3, 1).reshape(bsz, NSA_KV, nc, CMP_STRIDE * NSA_DK)

    pe = cmp_pe.reshape(2, 2, CMP_STRIDE * NSA_DK)
    w2kt = cmp_w2[0].T.astype(BF16)
    w2v = jnp.pad(cmp_w2[1], ((0, 0), (0, LANES - NSA_DK))).astype(BF16)
    kct, vcp = _nsa_compress(grouped(kc0), grouped(kc0 + NSA_KVW), pe, cmp_w1.astype(BF16), w2kt, w2v)
    tok = np.arange(t)
    e_mat = jnp.asarray(np.arange(N_SEL_PAD)[:, None] == (tok // SEL_LEN)[None, :], BF16)
    cstart = np.arange(nc) * CMP_STRIDE
    sstart = np.arange(N_SEL_PAD) * SEL_LEN
    ov_mat = jnp.asarray((cstart[None, :] < sstart[:, None] + SEL_LEN)
                         & (cstart[None, :] + CMP_LEN > sstart[:, None])
                         & (np.arange(nc)[None, :] < nc - 1), BF16)
    return _nsa_attention(oa, ob, kct, vcp, e_mat, ov_mat, t)


def _sb_mixer(x, g, mod, scl, shift, w_in):
    t = x.shape[1]
    wa, cs, b0, pv, wb, cb = _sb_weights(w_in)
    oa, ob = _inproj(x, g, mod, scl, shift, wa, cs, b0, pv, wb, cb)
    return _sb_attention(oa, ob, t)


def _diff_mixer(x, g, mod, scl, shift, w_in, lam, subln_g, layer_idx):
    t = x.shape[1]
    wa, cs, b0, pv, wb, cb = _diff_weights(w_in)
    oa, ob = _inproj(x, g, mod, scl, shift, wa, cs, b0, pv, wb, cb)
    lam_init = 0.8 - 0.6 * math.exp(-0.3 * layer_idx)
    return _diff_attention(oa, ob, lam, subln_g, lam_init, t)


def kernel(x, c, ada_w, ada_b, norm_g, ffn_w1, ffn_w2, nsa_w_in, nsa_cmp_pe, nsa_cmp_w1, nsa_cmp_w2,
           nsa_w_out, sb_w_in, sb_w_out, diff_w_in, diff_lam, diff_subln_g, diff_w_out):
    bsz, t, d = x.shape
    assert d == D_MODEL and t % ROW_TILE == 0 and t >= WIN + 128 and t <= N_SEL_PAD * SEL_LEN
    mod = _modulation(c, ada_w, ada_b)
    for i in range(DEPTH):
        def mods(sidx):
            return tuple((i, 3 * sidx + k) for k in range(3))

        shift, scl, gate = mods(0)
        x = _ffn(x, norm_g[i, 0], mod, scl, shift, gate, norm_g[i, 1],
                 ffn_w1[i, 0].astype(BF16), ffn_w2[i, 0].astype(BF16))
        shift, scl, gate = mods(1)
        kind, j = i % N_MIXERS, i // N_MIXERS
        if kind == 0:
            o = _nsa_mixer(x, norm_g[i, 2], mod, scl, shift, nsa_w_in[j], nsa_cmp_pe[j], nsa_cmp_w1[j],
                           nsa_cmp_w2[j])
            w_out = nsa_w_out[j]
        elif kind == 1:
            o = _sb_mixer(x, norm_g[i, 2], mod, scl, shift, sb_w_in[j])
            w_out = sb_w_out[j]
        else:
            o = _diff_mixer(x, norm_g[i, 2], mod, scl, shift, diff_w_in[j], diff_lam[j], diff_subln_g[j], i)
            w_out = diff_w_out[j]
        x = _outproj(o, w_out.astype(BF16), x, mod, gate, norm_g[i, 3])
        shift, scl, gate = mods(2)
        x = _ffn(x, norm_g[i, 4], mod, scl, shift, gate, norm_g[i, 5],
                 ffn_w1[i, 1].astype(BF16), ffn_w2[i, 1].astype(BF16))
    return x
```

```python
import functools
import math

import jax
import jax.numpy as jnp
import numpy as np
from jax import lax
from jax.experimental import pallas as pl
from jax.experimental.pallas import tpu as pltpu

F32 = jnp.float32
BF16 = jnp.bfloat16

D_MODEL = 1024
DEPTH = 4
N_MIXERS = 3
EPS = 1e-6
NEG = -1e30
FORCE = 1e4
LOG2E = 1.4426950408889634
NSA_HEADS = 16
NSA_DK = 64
NSA_KV = 4
NSA_REP = NSA_HEADS // NSA_KV
CMP_LEN = 32
CMP_STRIDE = 16
CMP_HID = 256
SEL_LEN = 64
SEL_TOPK = 16
WIN = 512
NSA_QW = NSA_HEADS * NSA_DK
NSA_KVW = NSA_KV * NSA_DK
N_SEL_PAD = 128
SB_HEADS = 16
SB_DH = D_MODEL // SB_HEADS
DIFF_HEADS = 8
DIFF_DH = D_MODEL // (2 * DIFF_HEADS)
D_FF = 2816

LANES = 128
MOD_ROWS = 8
ROW_TILE = 512
FF_CHUNK = 256
DIFF_TQ, DIFF_TK = 512, 512
SB_TQ, SB_TK = 512, 256
SB_LINEAR_FROM = 64.0
SB_DEAD = 192.0
DIFF_DEAD = 192.0
ALIBI_COLS = 7
GATE_COLS = NSA_REP * 3
NSA_TQ, NSA_TK = 256, 512
CMP_CHUNK = 128
VMEM_LIMIT = 56 * 1024 * 1024


def _cparams(sem):
    return pltpu.CompilerParams(dimension_semantics=sem, vmem_limit_bytes=VMEM_LIMIT)


def _const_spec(shape):
    n = len(shape)
    return pl.BlockSpec(shape, lambda *_: (0,) * n)


def _rms(x, g):
    return x * lax.rsqrt(jnp.mean(x * x, axis=-1, keepdims=True) + EPS) * g


def _dot(a, b):
    return jnp.dot(a, b, preferred_element_type=F32)


def _dot_nt(a, b):
    return lax.dot_general(a, b, (((1,), (1,)), ((), ())), preferred_element_type=F32)


def _mod_kernel(c_ref, w_ref, b_ref, o_ref):
    c = c_ref[...]
    cond = c * jax.nn.sigmoid(c)
    o_ref[0] = _dot(cond.astype(BF16), w_ref[0].astype(BF16)) + b_ref[0]


def _modulation(c, ada_w, ada_b):
    depth, d, n = ada_w.shape
    b = c.shape[0]
    rows = MOD_ROWS
    c_pad = jnp.zeros((rows, d), F32).at[:b].set(c)
    nc = 1152
    out = pl.pallas_call(
        _mod_kernel,
        grid=(depth, n // nc),
        in_specs=[
            pl.BlockSpec((rows, d), lambda i, j: (0, 0)),
            pl.BlockSpec((1, d, nc), lambda i, j: (i, 0, j)),
            pl.BlockSpec((1, 1, nc), lambda i, j: (i, 0, j)),
        ],
        out_specs=pl.BlockSpec((1, rows, nc), lambda i, j: (i, 0, j)),
        out_shape=jax.ShapeDtypeStruct((depth, rows, n), F32),
        compiler_params=_cparams(("arbitrary", "arbitrary")),
        name="adaln_mod",
    )(c_pad, ada_w, ada_b.reshape(depth, 1, n))
    return out


def _mod_spec(sel):
    layer, blk = sel
    return pl.BlockSpec((1, MOD_ROWS, D_MODEL), lambda b, i: (layer, 0, blk))


def _mod_row(ref):
    return ref[0, pl.ds(pl.program_id(0), 1), :]


def _inproj_kernel(*refs, tt, a_chunks, aug_blocks, b_chunks, b_aug, has_kv):
    if has_kv:
        (x_ref, g_ref, scl_ref, sh_ref, wa_ref, b0_ref, pv_ref, wb_ref, cb_ref,
         oa_ref, ob_ref, oc_ref, kv_ref) = refs
    else:
        x_ref, g_ref, scl_ref, sh_ref, wa_ref, b0_ref, pv_ref, wb_ref, cb_ref, oa_ref, ob_ref = refs
    x = x_ref[0]
    hh = (_rms(x, g_ref[...]) * (1.0 + _mod_row(scl_ref)) + _mod_row(sh_ref)).astype(BF16)
    t0 = pl.program_id(1) * tt
    pos = (t0 + lax.broadcasted_iota(jnp.int32, (tt, 1), 0)).astype(F32)
    lane = lax.broadcasted_iota(jnp.int32, (1, LANES), 1)

    def aug(k):
        return b0_ref[k:k + 1, :] + pos * pv_ref[k:k + 1, :]

    def put(dst, val):
        oa_ref[0, :, dst * LANES:(dst + 1) * LANES] = val.astype(oa_ref.dtype)

    for lo, width, kind, entries in a_chunks:
        acc = _dot(hh, wa_ref[:, lo:lo + width])
        if kind == "kv":
            for k in range(width // LANES):
                kv_ref[k] = acc[:, k * LANES:(k + 1) * LANES]
                for l in range(CMP_STRIDE):
                    oc_ref[0, l, :, k * LANES:(k + 1) * LANES] = kv_ref[
                        k, pl.ds(l, tt // CMP_STRIDE, stride=CMP_STRIDE), :].astype(oc_ref.dtype)
            continue
        for e in entries:
            blk = acc[:, e[1] * LANES:(e[1] + 1) * LANES]
            if e[0] == "mm":
                put(e[2], blk * e[3] if e[3] != 1.0 else blk)
            elif e[0] == "half":
                if e[3] != 1.0:
                    blk = blk * e[3]
                if e[4]:
                    blk = pltpu.roll(blk, NSA_DK, axis=1)
                put(e[2], jnp.where(lane < NSA_DK, blk, aug(e[5]) if e[5] >= 0 else 0.0))
            else:
                if e[3]:
                    blk = pltpu.roll(blk, LANES - e[3], axis=1)
                put(e[2], jnp.where(lane < GATE_COLS, blk, 0.0))
    for dst, k in aug_blocks:
        put(dst, aug(k))

    tpos = t0 + lax.broadcasted_iota(jnp.int32, (1, tt), 1)
    phi = ((tpos >> 7) << 7).astype(F32)
    plo = (tpos & 127).astype(F32)
    if b_aug:
        cc = cb_ref[...]
        key_rows = (cc[:, 0:1] * phi + cc[:, 1:2] * plo + cc[:, 2:3]).astype(ob_ref.dtype)
        for dst, nrows in b_aug:
            ob_ref[0, dst:dst + nrows, :] = key_rows[0:nrows]
    for lo, rows, pieces in b_chunks:
        acc = _dot_nt(wb_ref[lo:lo + rows, :], hh)
        for src, nrows, dst in pieces:
            ob_ref[0, dst:dst + nrows, :] = acc[src:src + nrows].astype(ob_ref.dtype)


def _inproj(x, g, mod, scl, shift, wa, b0, pv, wb, cb, plan):
    bsz, t, d = x.shape
    tt = min(ROW_TILE, t)
    a_chunks, aug_blocks, n_blocks, b_chunks, b_aug, n_rows, has_kv = plan
    na = n_blocks * LANES
    in_specs = [
        pl.BlockSpec((1, tt, d), lambda b, i: (b, i, 0)),
        _const_spec((1, d)),
        _mod_spec(scl),
        _mod_spec(shift),
        _const_spec(wa.shape),
        _const_spec(b0.shape),
        _const_spec(pv.shape),
        _const_spec(wb.shape),
        _const_spec(cb.shape),
    ]
    out_specs = [pl.BlockSpec((1, tt, na), lambda b, i: (b, i, 0)),
                 pl.BlockSpec((1, n_rows, tt), lambda b, i: (b, 0, i))]
    out_shape = [jax.ShapeDtypeStruct((bsz, t, na), BF16), jax.ShapeDtypeStruct((bsz, n_rows, t), BF16)]
    scratch = []
    if has_kv:
        kvw = 2 * NSA_KVW
        out_specs.append(pl.BlockSpec((1, CMP_STRIDE, tt // CMP_STRIDE, kvw), lambda b, i: (b, 0, i, 0)))
        out_shape.append(jax.ShapeDtypeStruct((bsz, CMP_STRIDE, t // CMP_STRIDE, kvw), BF16))
        scratch.append(pltpu.VMEM((kvw // LANES, tt, LANES), F32))
    return pl.pallas_call(
        functools.partial(_inproj_kernel, tt=tt, a_chunks=a_chunks, aug_blocks=aug_blocks, b_chunks=b_chunks,
                          b_aug=b_aug, has_kv=has_kv),
        grid=(bsz, t // tt),
        in_specs=in_specs,
        out_specs=out_specs,
        out_shape=out_shape,
        scratch_shapes=scratch,
        compiler_params=_cparams(("arbitrary", "arbitrary")),
        name="mixer_inproj",
    )(x, g.reshape(1, d), mod, mod, wa, b0, pv, wb, cb)


def _ffn_kernel(*refs, after_mixer):
    if after_mixer:
        mo_ref, wo_ref, mgate_ref, gm_ref = refs[:4]
        refs = refs[4:]
    x_ref, g1_ref, scl_ref, sh_ref, gate_ref, g2_ref, w1_ref, w2_ref, o_ref, acc_ref = refs
    x = x_ref[0]
    if after_mixer:
        x = x + _mod_row(mgate_ref) * _rms(_dot(mo_ref[0], wo_ref[...]), gm_ref[...])
    hh = (_rms(x, g1_ref[...]) * (1.0 + _mod_row(scl_ref)) + _mod_row(sh_ref)).astype(BF16)
    dff = w2_ref.shape[0]
    for c in range(dff // FF_CHUNK):
        lo = c * FF_CHUNK
        gt = _dot(hh, w1_ref[:, lo:lo + FF_CHUNK])
        up = _dot(hh, w1_ref[:, dff + lo:dff + lo + FF_CHUNK])
        act = (gt * jax.nn.sigmoid(gt) * up).astype(BF16)
        part = _dot(act, w2_ref[lo:lo + FF_CHUNK, :])
        if c == 0:
            acc_ref[...] = part
        else:
            acc_ref[...] += part
    o_ref[0] = x + 0.5 * _mod_row(gate_ref) * _rms(acc_ref[...], g2_ref[...])


def _ffn(x, g1, mod, scl, shift, gate, g2, w1, w2, mixer=None):
    bsz, t, d = x.shape
    tt = min(ROW_TILE, t)
    in_specs = [
        pl.BlockSpec((1, tt, d), lambda b, i: (b, i, 0)),
        _const_spec((1, d)), _mod_spec(scl), _mod_spec(shift), _mod_spec(gate), _const_spec((1, d)),
        _const_spec(w1.shape), _const_spec(w2.shape),
    ]
    args = [x, g1.reshape(1, d), mod, mod, mod, g2.reshape(1, d), w1, w2]
    if mixer is not None:
        mo, wo, mgate, gm = mixer
        in_specs = [pl.BlockSpec((1, tt, mo.shape[2]), lambda b, i: (b, i, 0)), _const_spec(wo.shape),
                    _mod_spec(mgate), _const_spec((1, d))] + in_specs
        args = [mo, wo, mod, gm.reshape(1, d)] + args
    return pl.pallas_call(
        functools.partial(_ffn_kernel, after_mixer=mixer is not None),
        grid=(bsz, t // tt),
        in_specs=in_specs,
        out_specs=pl.BlockSpec((1, tt, d), lambda b, i: (b, i, 0)),
        out_shape=jax.ShapeDtypeStruct((bsz, t, d), F32),
        scratch_shapes=[pltpu.VMEM((tt, d), F32)],
        compiler_params=_cparams(("arbitrary", "arbitrary")),
        name="ffn",
    )(*args)


def _online_step(s, m, l, acc, v):
    m_new = jnp.maximum(m, jnp.max(s, axis=-1, keepdims=True))
    alpha = jnp.exp2(m - m_new)
    p = jnp.exp2(s - m_new)
    l = alpha * l + jnp.sum(p, axis=-1, keepdims=True)
    acc = alpha * acc + _dot(p.astype(BF16), v)
    return m_new, l, acc


def _softmax_init(rows, width):
    return (jnp.full((rows, 1), NEG, F32), jnp.zeros((rows, 1), F32), jnp.zeros((rows, width), F32))


def _gelu_tanh(x):
    return 0.5 * x * (1.0 + jnp.tanh(math.sqrt(2.0 / math.pi) * (x + 0.044715 * (x * x * x))))


def _cmp_kernel(kv_ref, pe_ref, w1_ref, w2kt_ref, w2v_ref, kt_ref, v_ref):
    nc = kv_ref.shape[2]

    def hidden(j, g):
        lo = j * NSA_KVW + g * NSA_DK
        top = bot = None
        for l in range(CMP_STRIDE):
            x = kv_ref[0, l, :, lo:lo + NSA_DK].astype(F32)
            hi = CMP_STRIDE + l
            a = _dot((x + pe_ref[j, l:l + 1, :]).astype(BF16), w1_ref[j, l * NSA_DK:(l + 1) * NSA_DK, :])
            b = _dot((x + pe_ref[j, hi:hi + 1, :]).astype(BF16), w1_ref[j, hi * NSA_DK:(hi + 1) * NSA_DK, :])
            top = a if top is None else top + a
            bot = b if bot is None else bot + b
        return _gelu_tanh(top + pltpu.roll(bot, nc - 1, axis=0)).astype(BF16)

    n = lax.broadcasted_iota(jnp.int32, (1, nc), 1)
    end = n * CMP_STRIDE + (CMP_LEN - 1)
    phi = ((end >> 7) << 7).astype(F32)
    plo = (end & 127).astype(F32)
    r = lax.broadcasted_iota(jnp.int32, (LANES - NSA_DK, 1), 0)
    aug = jnp.where(r < 3, phi, jnp.where(r < 6, plo, jnp.where(r == 6, 1.0, 0.0))).astype(BF16)
    for g in range(NSA_KV):
        kt_ref[0, g, 0:NSA_DK, :] = _dot_nt(w2kt_ref[...], hidden(0, g)).astype(BF16)
        kt_ref[0, g, NSA_DK:LANES, :] = aug
        v_ref[0, g] = _dot(hidden(1, g), w2v_ref[...]).astype(BF16)


def _nsa_compress(kv, pe, w1, w2kt, w2v):
    bsz, _, nc, kvw = kv.shape
    return pl.pallas_call(
        _cmp_kernel,
        grid=(bsz,),
        in_specs=[pl.BlockSpec((1, CMP_STRIDE, nc, kvw), lambda b: (b, 0, 0, 0)), _const_spec(pe.shape),
                  _const_spec(w1.shape), _const_spec(w2kt.shape), _const_spec(w2v.shape)],
        out_specs=[pl.BlockSpec((1, NSA_KV, LANES, nc), lambda b: (b, 0, 0, 0)),
                   pl.BlockSpec((1, NSA_KV, nc, LANES), lambda b: (b, 0, 0, 0))],
        out_shape=[jax.ShapeDtypeStruct((bsz, NSA_KV, LANES, nc), BF16),
                   jax.ShapeDtypeStruct((bsz, NSA_KV, nc, LANES), BF16)],
        compiler_params=_cparams(("arbitrary",)),
        name="nsa_compress",
    )(kv, pe, w1, w2kt, w2v)


def _split3(x):
    hi = x.astype(BF16)
    r1 = x - hi.astype(F32)
    mid = r1.astype(BF16)
    lo = (r1 - mid.astype(F32)).astype(BF16)
    return hi, mid, lo


def _nsa_attn_kernel(qa_ref, kct_ref, vc_ref, kst_ref, vs_ref, kwt_ref, vw_ref, gt_ref, e_ref, ov_ref,
                     o_ref, tiles_ref, *, tq, tk):
    i = pl.program_id(2)
    q0 = i * tq
    rows = NSA_REP * tq
    qa = qa_ref[0]
    qs = jnp.concatenate([qa[:, r * LANES:(r + 1) * LANES] for r in range(NSA_REP)], axis=0)
    t_row = q0 + (lax.broadcasted_iota(jnp.int32, (rows, 1), 0) & (tq - 1))

    nc = kct_ref.shape[3]

    def compressed(width):
        s = _dot(qs, kct_ref[0, 0, :, 0:width])
        cend = lax.broadcasted_iota(jnp.int32, (1, width), 1) * CMP_STRIDE + (CMP_LEN - 1)
        valid = cend <= t_row
        s = jnp.where(valid, s, NEG)
        e = jnp.where(valid, jnp.exp2(s - jnp.max(s, axis=-1, keepdims=True)), 0.0)
        l = jnp.sum(e, axis=-1, keepdims=True)
        p = e / jnp.where(l > 0.0, l, 1.0)
        o = _dot(p.astype(BF16), vc_ref[0, 0, 0:width, :])
        p4 = p[0:tq]
        for r in range(1, NSA_REP):
            p4 = p4 + p[r * tq:(r + 1) * tq]
        ov = ov_ref[:, 0:width]
        return o, sum(_dot_nt(ov, part) for part in _split3(p4))

    chunk = min(CMP_CHUNK, nc)
    widths = list(range(chunk, nc + 1, chunk))
    n_ended = jnp.maximum(q0 + tq - CMP_LEN, 0) // CMP_STRIDE + 1
    o_cmp, imp = lax.switch((n_ended - 1) // chunk, [functools.partial(compressed, w) for w in widths])

    wk = WIN + tq
    st = pl.multiple_of(jnp.maximum(q0 - WIN, 0), LANES)
    col = st + lax.broadcasted_iota(jnp.int32, (1, wk), 1)
    valid = (col <= t_row) & (col > t_row - WIN)
    s = jnp.where(valid, _dot(qs, kwt_ref[0, :, pl.ds(st, wk)]), NEG)
    e = jnp.exp2(s - jnp.max(s, axis=-1, keepdims=True))
    o_win = _dot(e.astype(BF16), vw_ref[0, pl.ds(st, wk), :]) / jnp.sum(e, axis=-1, keepdims=True)

    cur = (q0 + lax.broadcasted_iota(jnp.int32, (1, tq), 1)) >> 6
    sid = lax.broadcasted_iota(jnp.int32, (N_SEL_PAD, 1), 0)
    forced = (sid == 0) | (sid == cur) | (sid == cur - 1)
    work = jnp.where(forced | (sid > cur), -3.0e38, imp)
    selb = jnp.where(forced, 0.0, NEG)
    for _ in range(SEL_TOPK - 3):
        mx = jnp.max(work, axis=0, keepdims=True)
        first = jnp.min(jnp.where(work == mx, sid, N_SEL_PAD), axis=0, keepdims=True)
        pick = sid == first
        selb = jnp.where(pick, 0.0, selb)
        work = jnp.where(pick, -3.0e38, work)
    selb = selb.T

    jd = q0 // tk
    n_tiles = kst_ref.shape[2] // tk
    any_sel = jnp.max(jnp.where(selb == 0.0, 1.0, 0.0), axis=0, keepdims=True)
    blk_tile = (lax.broadcasted_iota(jnp.int32, (N_SEL_PAD, LANES), 0) // (tk // SEL_LEN)
                == lax.broadcasted_iota(jnp.int32, (N_SEL_PAD, LANES), 1)).astype(BF16)
    tile_cnt = _dot(jnp.broadcast_to(any_sel, (16, N_SEL_PAD)).astype(BF16), blk_tile)
    n_act = jnp.int32(0)
    for j in range(n_tiles):
        tiles_ref[n_act] = j
        n_act = n_act + ((tile_cnt[0, j] > 0.0) & (j < jd)).astype(jnp.int32)

    selb4 = jnp.concatenate([selb.astype(BF16)] * NSA_REP, axis=0)
    qsel = jnp.concatenate([qs, selb4], axis=1)

    def sel_scores(st, width):
        return _dot(qsel, jnp.concatenate([kst_ref[0, :, pl.ds(st, width)], e_ref[:, pl.ds(st, width)]], axis=0))

    def sel_body(it, carry):
        st = pl.multiple_of(tiles_ref[it] * tk, tk)
        return _online_step(sel_scores(st, tk), *carry, vs_ref[0, pl.ds(st, tk), :])

    carry = lax.fori_loop(0, n_act, sel_body, _softmax_init(rows, LANES))
    st = pl.multiple_of(jd * tk, tk)

    def diagonal(width):
        col = st + lax.broadcasted_iota(jnp.int32, (1, width), 1)
        s = jnp.where(col <= t_row, sel_scores(st, width), NEG)
        _, l, acc = _online_step(s, *carry, vs_ref[0, pl.ds(st, width), :])
        return acc / l

    o_sel = lax.switch((q0 - st) // tq, [functools.partial(diagonal, w) for w in range(tq, tk + 1, tq)])

    gs = jax.nn.sigmoid(gt_ref[0].astype(F32))
    outs = []
    for r in range(NSA_REP):
        sl = slice(r * tq, (r + 1) * tq)
        outs.append(gs[:, 3 * r:3 * r + 1] * o_cmp[sl] + gs[:, 3 * r + 1:3 * r + 2] * o_sel[sl]
                    + gs[:, 3 * r + 2:3 * r + 3] * o_win[sl])
    pairs = [outs[2 * a] + pltpu.roll(outs[2 * a + 1], NSA_DK, axis=1) for a in range(NSA_REP // 2)]
    o_ref[0] = jnp.concatenate(pairs, axis=1).astype(o_ref.dtype)


def _nsa_attention(oa, ob, kct, vcp, e_mat, ov_mat, t):
    bsz = oa.shape[0]
    tq, tk = NSA_TQ, NSA_TK
    qw = NSA_REP * LANES
    vs_blk = NSA_HEADS
    vw_blk = vs_blk + NSA_KV
    gt_blk = vw_blk + NSA_KV
    nc = kct.shape[3]
    return pl.pallas_call(
        functools.partial(_nsa_attn_kernel, tq=tq, tk=tk),
        grid=(bsz, NSA_KV, t // tq),
        in_specs=[
            pl.BlockSpec((1, tq, qw), lambda b, g, i: (b, i, g)),
            pl.BlockSpec((1, 1, LANES, nc), lambda b, g, i: (b, g, 0, 0)),
            pl.BlockSpec((1, 1, nc, LANES), lambda b, g, i: (b, g, 0, 0)),
            pl.BlockSpec((1, LANES, t), lambda b, g, i: (b, g, 0)),
            pl.BlockSpec((1, t, LANES), lambda b, g, i: (b, 0, vs_blk + g)),
            pl.BlockSpec((1, LANES, t), lambda b, g, i: (b, NSA_KV + g, 0)),
            pl.BlockSpec((1, t, LANES), lambda b, g, i: (b, 0, vw_blk + g)),
            pl.BlockSpec((1, tq, LANES), lambda b, g, i: (b, i, gt_blk + g)),
            _const_spec(e_mat.shape),
            _const_spec(ov_mat.shape),
        ],
        out_specs=pl.BlockSpec((1, tq, NSA_REP * NSA_DK), lambda b, g, i: (b, i, g)),
        out_shape=jax.ShapeDtypeStruct((bsz, t, NSA_QW), BF16),
        scratch_shapes=[pltpu.SMEM((t // tk,), jnp.int32)],
        compiler_params=_cparams(("arbitrary", "arbitrary", "arbitrary")),
        name="nsa_attention",
    )(oa, kct, vcp, ob, oa, ob, oa, oa, e_mat, ov_mat)


def _sb_kernel(q_ref, kt_ref, v_ref, o_ref, rem_ref, acc_ref, *, tq, tk):
    i = pl.program_id(2)
    q0 = i * tq
    q = q_ref[0]
    lane = lax.broadcasted_iota(jnp.int32, (1, LANES), 1)
    zero = jnp.zeros_like(q)
    qs = jnp.concatenate([jnp.where(lane < SB_DH, q, zero), jnp.where(lane >= SB_DH, q, zero)], axis=0)
    t_row = q0 + (lax.broadcasted_iota(jnp.int32, (2 * tq, 1), 0) & (tq - 1))
    upper = (lax.broadcasted_iota(jnp.int32, (tk, tk), 0) > lax.broadcasted_iota(jnp.int32, (tk, tk), 1)
             ).astype(BF16)
    rem_ref[...] = jnp.zeros(rem_ref.shape, F32)
    acc_ref[...] = jnp.zeros(acc_ref.shape, F32)

    def tiles(js, masked):
        sts = [pl.multiple_of(j * tk, tk) for j in js]
        zs = [_dot(qs, kt_ref[0, :, pl.ds(st, tk)]) for st in sts]
        costs, log_bs, befores = [], [], []
        for st, z in zip(sts, zs):
            cost = jnp.where(z > SB_LINEAR_FROM, z, jnp.log(1.0 + jnp.exp2(z)) * LOG2E)
            log_bs.append(z - cost)
            before = None
            if masked:
                before = (st + lax.broadcasted_iota(jnp.int32, (1, tk), 1)) < t_row
                cost = jnp.where(before, cost, 0.0)
            costs.append(cost.astype(BF16))
            befores.append(before)
        rems = [_dot(c, upper) for c in costs]
        rem_right = rem_ref[...]
        probs = []
        for log_b, c, rem, before in zip(log_bs, costs, rems, befores):
            ex = log_b - (rem + rem_right)
            if masked:
                ex = jnp.where(before, ex, NEG)
            probs.append(jnp.exp2(ex).astype(BF16))
            rem_right = rem_right + (rem[:, 0:1] + c[:, 0:1].astype(F32))
        rem_ref[...] = rem_right
        acc = acc_ref[...]
        for st, p in zip(sts, probs):
            acc = acc + _dot(p, v_ref[0, pl.ds(st, tk), :])
        acc_ref[...] = acc
        return jnp.min(rem_right)

    per_step = tq // tk
    n_full = q0 // tk
    spent = tiles([n_full + per_step - 1 - d for d in range(per_step)], True)

    def go_on(c):
        return (c[0] < n_full // per_step) & (c[1] < SB_DEAD)

    def body(c):
        it = c[0]
        return it + 1, tiles([n_full - 1 - per_step * it - d for d in range(per_step)], False)

    lax.while_loop(go_on, body, (jnp.int32(0), spent))
    acc = acc_ref[...]
    o_ref[0] = jnp.where(lane < SB_DH, acc[0:tq], acc[tq:2 * tq]).astype(o_ref.dtype)


def _sb_attention(oa, ob, t):
    bsz = oa.shape[0]
    tq, tk = SB_TQ, SB_TK
    npair = SB_HEADS // 2
    return pl.pallas_call(
        functools.partial(_sb_kernel, tq=tq, tk=tk),
        grid=(bsz, npair, t // tq),
        in_specs=[
            pl.BlockSpec((1, tq, LANES), lambda b, h, i: (b, i, h)),
            pl.BlockSpec((1, LANES, t), lambda b, h, i: (b, h, 0)),
            pl.BlockSpec((1, t, LANES), lambda b, h, i: (b, 0, npair + h)),
        ],
        out_specs=pl.BlockSpec((1, tq, LANES), lambda b, h, i: (b, i, h)),
        out_shape=jax.ShapeDtypeStruct((bsz, t, D_MODEL), BF16),
        scratch_shapes=[pltpu.VMEM((2 * tq, 1), F32), pltpu.VMEM((2 * tq, LANES), F32)],
        compiler_params=_cparams(("arbitrary", "arbitrary", "arbitrary")),
        name="sb_attention",
    )(oa, ob, oa)


def _diff_kernel(slope_ref, lam_ref, g_ref, qa_ref, kt_ref, v_ref, o_ref, kmax_ref, *, tq, tk, lam_init):
    h = pl.program_id(1)
    i = pl.program_id(2)
    q0 = i * tq
    n_tiles = kt_ref.shape[2] // tk

    @pl.when(i == 0)
    def _():
        run = jnp.float32(0.0)
        for j in range(n_tiles):
            kf = kt_ref[0, 0:LANES, j * tk:(j + 1) * tk].astype(F32)
            sq = kf * kf
            norms = jnp.sqrt(jnp.maximum(jnp.sum(sq[0:DIFF_DH], axis=0, keepdims=True),
                                         jnp.sum(sq[DIFF_DH:LANES], axis=0, keepdims=True)))
            run = jnp.maximum(run, jnp.max(norms))
            kmax_ref[j] = run

    qa = qa_ref[0]
    lane = lax.broadcasted_iota(jnp.int32, (1, 2 * LANES), 1)
    zero = jnp.zeros_like(qa)
    qs = jnp.concatenate(
        [jnp.where(((lane >= m_ * DIFF_DH) & (lane < (m_ + 1) * DIFF_DH)) | (lane >= LANES), qa, zero)
         for m_ in range(2)], axis=0)
    t_row = q0 + (lax.broadcasted_iota(jnp.int32, (2 * tq, 1), 0) & (tq - 1))

    def scores(j):
        return _dot(qs, kt_ref[0, :, pl.ds(pl.multiple_of(j * tk, tk), tk)])

    def values(j):
        return v_ref[0, pl.ds(pl.multiple_of(j * tk, tk), tk), :]

    jd = q0 // tk
    col = jd * tk + lax.broadcasted_iota(jnp.int32, (1, tk), 1)
    state = _online_step(jnp.where(col <= t_row, scores(jd), NEG), *_softmax_init(2 * tq, LANES), values(jd))
    qf = jnp.where(lane < LANES, qs, jnp.zeros_like(qs)).astype(F32)
    q_norm = jnp.max(jnp.sqrt(jnp.sum(qf * qf, axis=-1, keepdims=True)))
    row_col = 2 * DIFF_DH + ALIBI_COLS - 1
    row_const = qs[:, row_col:row_col + 1].astype(F32)
    slope = slope_ref[h]

    def live(c):
        j = jnp.maximum(c[0], 0)
        bound = q_norm * kmax_ref[j] * 1.001 + slope * ((j + 1) * tk - 1).astype(F32)
        return (c[0] >= 0) & (bound - c[4] > -DIFF_DEAD)

    def body(c):
        m, l, acc = _online_step(scores(c[0]), c[1], c[2], c[3], values(c[0]))
        return c[0] - 1, m, l, acc, jnp.min(m - row_const)

    _, _, l, acc, _ = lax.while_loop(live, body, (jd - 1, *state, jnp.min(state[0] - row_const)))
    out = acc / l
    lam = lam_ref[...]
    lam_full = (jnp.exp(jnp.sum(lam[0:1] * lam[1:2], axis=-1, keepdims=True))
                - jnp.exp(jnp.sum(lam[2:3] * lam[3:4], axis=-1, keepdims=True)) + lam_init)
    o = out[0:tq] - lam_full * out[tq:2 * tq]
    o_ref[0] = (_rms(o, g_ref[...]) * (1.0 - lam_init)).astype(o_ref.dtype)


def _diff_attention(oa, ob, lam, subln_g, lam_init, t):
    bsz = oa.shape[0]
    tq, tk = DIFF_TQ, DIFF_TK
    v_blk = DIFF_HEADS * 2
    slopes = _alibi_slopes(DIFF_HEADS) * LOG2E
    return pl.pallas_call(
        functools.partial(_diff_kernel, tq=tq, tk=tk, lam_init=lam_init),
        grid=(bsz, DIFF_HEADS, t // tq),
        in_specs=[
            pl.BlockSpec(memory_space=pltpu.SMEM),
            _const_spec(lam.shape),
            _const_spec((1, 2 * DIFF_DH)),
            pl.BlockSpec((1, tq, 2 * LANES), lambda b, h, i: (b, i, h)),
            pl.BlockSpec((1, 2 * LANES, t), lambda b, h, i: (b, h, 0)),
            pl.BlockSpec((1, t, LANES), lambda b, h, i: (b, 0, v_blk + h)),
        ],
        out_specs=pl.BlockSpec((1, tq, LANES), lambda b, h, i: (b, i, h)),
        out_shape=jax.ShapeDtypeStruct((bsz, t, D_MODEL), BF16),
        scratch_shapes=[pltpu.SMEM((t // tk,), F32)],
        compiler_params=_cparams(("arbitrary", "arbitrary", "arbitrary")),
        name="diff_attention",
    )(slopes, lam, subln_g.reshape(1, 2 * DIFF_DH), oa, ob, oa)


def _alibi_slopes(n):
    return jnp.exp2(-8.0 * jnp.arange(1, n + 1, dtype=F32) / n)


def _alibi_query_cols(slopes, first):
    sl2 = slopes * LOG2E
    parts = jnp.stack([p.astype(F32) for p in _split3(sl2)], axis=1)
    b0 = jnp.pad(jnp.concatenate([parts, parts], axis=1), ((0, 0), (first, LANES - first - 6)))
    pv = jnp.pad(-sl2[:, None], ((0, 0), (first + 6, LANES - first - ALIBI_COLS)))
    return b0, pv


def _alibi_key_rows():
    c = np.zeros((LANES, 8), np.float32)
    c[0:3, 0] = 1.0
    c[3:6, 1] = 1.0
    c[6, 2] = 1.0
    return jnp.asarray(c)


def _nsa_weights(w_in):
    bounds = [0, NSA_QW] + [NSA_QW + (k + 1) * NSA_KVW for k in range(6)]
    wq, wkc, wvc, wks, wvs, wkw, wvw = [w_in[:, bounds[k]:bounds[k + 1]] for k in range(7)]
    wg = jnp.pad(w_in[:, bounds[7]:], ((0, 0), (0, LANES - NSA_KV * GATE_COLS)))
    wa = jnp.concatenate([wq, wkc, wvc, wvs, wvw, wg], axis=1).astype(BF16)
    scale = NSA_DK ** -0.5 * LOG2E
    per = ROW_TILE // LANES
    qblocks = NSA_QW // LANES
    a_chunks = []
    for c in range(qblocks // per):
        ents = tuple(("half", b, 2 * (c * per + b) + hi, scale, hi, 2 * (c * per + b) + hi)
                     for b in range(per) for hi in range(2))
        a_chunks.append((c * ROW_TILE, ROW_TILE, "mm", ents))
    a_chunks.append((NSA_QW, 2 * NSA_KVW, "kv", ()))
    v0 = NSA_QW + 2 * NSA_KVW
    vs_blk, vw_blk, gt_blk = NSA_HEADS, NSA_HEADS + NSA_KV, NSA_HEADS + 2 * NSA_KV
    ents = tuple(("half", g // 2, vs_blk + g, 1.0, g % 2, -1) for g in range(NSA_KV)) + tuple(
        ("half", NSA_KV // 2 + g // 2, vw_blk + g, 1.0, g % 2, -1) for g in range(NSA_KV))
    a_chunks.append((v0, 2 * NSA_KVW, "mm", ents))
    a_chunks.append((v0 + 2 * NSA_KVW, LANES, "mm",
                     tuple(("gate", 0, gt_blk + g, GATE_COLS * g) for g in range(NSA_KV))))
    b0, pv = _alibi_query_cols(_alibi_slopes(NSA_HEADS), NSA_DK)
    wb = jnp.concatenate([wks, wkw], axis=1).T.astype(BF16)
    pieces = tuple((NSA_DK * g, NSA_DK, LANES * g) for g in range(2 * NSA_KV))
    b_aug = tuple((LANES * g + NSA_DK, LANES - NSA_DK) for g in range(2 * NSA_KV))
    plan = (tuple(a_chunks), (), gt_blk + NSA_KV, ((0, 2 * NSA_KVW, pieces),), b_aug, 2 * NSA_KV * LANES, True)
    return wa, b0, pv, wb, _alibi_key_rows(), plan


def _sb_weights(w_in):
    d = w_in.shape[0]
    wq, wk, wv = jnp.split(w_in, 3, axis=1)
    wa = jnp.concatenate([wq, wv], axis=1).astype(BF16)
    per = ROW_TILE // LANES
    nq = d // ROW_TILE
    a_chunks = tuple((c * ROW_TILE, ROW_TILE, "mm",
                      tuple(("mm", b, c * per + b, SB_DH ** -0.5 * LOG2E if c < nq else 1.0) for b in range(per)))
                     for c in range(2 * nq))
    b_chunks = tuple((c * ROW_TILE, ROW_TILE, ((0, ROW_TILE, c * ROW_TILE),)) for c in range(nq))
    zero = jnp.asarray(np.zeros((1, LANES), np.float32))
    plan = (a_chunks, (), 2 * d // LANES, b_chunks, (), d, False)
    return wa, zero, zero, wk.T.astype(BF16), jnp.asarray(np.zeros((LANES, 8), np.float32)), plan


def _diff_weights(w_in):
    d = w_in.shape[0]
    wq, wk, wv = jnp.split(w_in, 3, axis=1)
    wa = jnp.concatenate([wq, wv], axis=1).astype(BF16)
    scale = DIFF_DH ** -0.5 * LOG2E
    per = ROW_TILE // LANES
    nq = d // ROW_TILE
    a_chunks = tuple(
        (c * ROW_TILE, ROW_TILE, "mm",
         tuple(("mm", b, 2 * (c * per + b), scale) if c < nq else ("mm", b, 2 * DIFF_HEADS + (c - nq) * per + b, 1.0)
               for b in range(per)))
        for c in range(2 * nq))
    aug_blocks = tuple((2 * h + 1, h) for h in range(DIFF_HEADS))
    b0, pv = _alibi_query_cols(_alibi_slopes(DIFF_HEADS), 0)
    b_chunks = tuple((c * ROW_TILE, ROW_TILE,
                      tuple((LANES * k, LANES, 2 * LANES * (c * per + k)) for k in range(per)))
                     for c in range(nq))
    b_aug = tuple((2 * LANES * h + LANES, LANES) for h in range(DIFF_HEADS))
    plan = (a_chunks, aug_blocks, 3 * DIFF_HEADS, b_chunks, b_aug, 2 * DIFF_HEADS * LANES, False)
    return wa, b0, pv, wk.T.astype(BF16), _alibi_key_rows(), plan


def _nsa_mixer(x, g, mod, scl, shift, w_in, cmp_pe, cmp_w1, cmp_w2):
    t = x.shape[1]
    wa, b0, pv, wb, cb, plan = _nsa_weights(w_in)
    oa, ob, kv = _inproj(x, g, mod, scl, shift, wa, b0, pv, wb, cb, plan)
    nc = t // CMP_STRIDE
    w2kt = cmp_w2[0].T.astype(BF16)
    w2v = jnp.pad(cmp_w2[1], ((0, 0), (0, LANES - NSA_DK))).astype(BF16)
    kct, vcp = _nsa_compress(kv, cmp_pe, cmp_w1.astype(BF16), w2kt, w2v)
    tok = np.arange(t)
    e_mat = jnp.asarray(np.arange(N_SEL_PAD)[:, None] == (tok // SEL_LEN)[None, :], BF16)
    cstart = np.arange(nc) * CMP_STRIDE
    sstart = np.arange(N_SEL_PAD) * SEL_LEN
    ov_mat = jnp.asarray((cstart[None, :] < sstart[:, None] + SEL_LEN)
                         & (cstart[None, :] + CMP_LEN > sstart[:, None])
                         & (np.arange(nc)[None, :] < nc - 1), BF16)
    return _nsa_attention(oa, ob, kct, vcp, e_mat, ov_mat, t)


def _sb_mixer(x, g, mod, scl, shift, w_in):
    t = x.shape[1]
    wa, b0, pv, wb, cb, plan = _sb_weights(w_in)
    oa, ob = _inproj(x, g, mod, scl, shift, wa, b0, pv, wb, cb, plan)
    return _sb_attention(oa, ob, t)


def _diff_mixer(x, g, mod, scl, shift, w_in, lam, subln_g, layer_idx):
    t = x.shape[1]
    wa, b0, pv, wb, cb, plan = _diff_weights(w_in)
    oa, ob = _inproj(x, g, mod, scl, shift, wa, b0, pv, wb, cb, plan)
    lam_init = 0.8 - 0.6 * math.exp(-0.3 * layer_idx)
    return _diff_attention(oa, ob, lam, subln_g, lam_init, t)


def kernel(x, c, ada_w, ada_b, norm_g, ffn_w1, ffn_w2, nsa_w_in, nsa_cmp_pe, nsa_cmp_w1, nsa_cmp_w2,
           nsa_w_out, sb_w_in, sb_w_out, diff_w_in, diff_lam, diff_subln_g, diff_w_out):
    bsz, t, d = x.shape
    assert d == D_MODEL and t % ROW_TILE == 0 and t >= WIN + 128 and t <= N_SEL_PAD * SEL_LEN
    mod = _modulation(c, ada_w, ada_b)
    for i in range(DEPTH):
        def mods(sidx):
            return tuple((i, 3 * sidx + k) for k in range(3))

        shift, scl, gate = mods(0)
        x = _ffn(x, norm_g[i, 0], mod, scl, shift, gate, norm_g[i, 1],
                 ffn_w1[i, 0].astype(BF16), ffn_w2[i, 0].astype(BF16))
        shift, scl, gate = mods(1)
        kind, j = i % N_MIXERS, i // N_MIXERS
        if kind == 0:
            o = _nsa_mixer(x, norm_g[i, 2], mod, scl, shift, nsa_w_in[j], nsa_cmp_pe[j], nsa_cmp_w1[j],
                           nsa_cmp_w2[j])
            w_out = nsa_w_out[j]
        elif kind == 1:
            o = _sb_mixer(x, norm_g[i, 2], mod, scl, shift, sb_w_in[j])
            w_out = sb_w_out[j]
        else:
            o = _diff_mixer(x, norm_g[i, 2], mod, scl, shift, diff_w_in[j], diff_lam[j], diff_subln_g[j], i)
            w_out = diff_w_out[j]
        mixer = (o, w_out.astype(BF16), gate, norm_g[i, 3])
        shift, scl, gate = mods(2)
        x = _ffn(x, norm_g[i, 4], mod, scl, shift, gate, norm_g[i, 5],
                 ffn_w1[i, 1].astype(BF16), ffn_w2[i, 1].astype(BF16), mixer=mixer)
    return x
```

```python
import functools
import math

import jax
import jax.numpy as jnp
import numpy as np
from jax import lax
from jax.experimental import pallas as pl
from jax.experimental.pallas import tpu as pltpu

F32 = jnp.float32
BF16 = jnp.bfloat16

D_MODEL = 1024
DEPTH = 4
N_MIXERS = 3
EPS = 1e-6
NEG = -1e30
FORCE = 1e4
LOG2E = 1.4426950408889634
NSA_HEADS = 16
NSA_DK = 64
NSA_KV = 4
NSA_REP = NSA_HEADS // NSA_KV
CMP_LEN = 32
CMP_STRIDE = 16
CMP_HID = 256
SEL_LEN = 64
SEL_TOPK = 16
WIN = 512
NSA_QW = NSA_HEADS * NSA_DK
NSA_KVW = NSA_KV * NSA_DK
N_SEL_PAD = 128
SB_HEADS = 16
SB_DH = D_MODEL // SB_HEADS
DIFF_HEADS = 8
DIFF_DH = D_MODEL // (2 * DIFF_HEADS)
D_FF = 2816

LANES = 128
MOD_ROWS = 8
ROW_TILE = 512
FF_CHUNK = 256
DIFF_TQ, DIFF_TK = 512, 512
SB_TQ, SB_TK = 512, 256
SB_LINEAR_FROM = 64.0
SB_DEAD = 192.0
DIFF_DEAD = 192.0
ALIBI_COLS = 7
GATE_COLS = NSA_REP * 3
NSA_TQ, NSA_TK = 256, 512
CMP_CHUNK = 128
VMEM_LIMIT = 56 * 1024 * 1024


def _cparams(sem):
    return pltpu.CompilerParams(dimension_semantics=sem, vmem_limit_bytes=VMEM_LIMIT)


def _const_spec(shape):
    n = len(shape)
    return pl.BlockSpec(shape, lambda *_: (0,) * n)


def _rms(x, g):
    return x * lax.rsqrt(jnp.mean(x * x, axis=-1, keepdims=True) + EPS) * g


def _dot(a, b):
    return jnp.dot(a, b, preferred_element_type=F32)


def _dot_nt(a, b):
    return lax.dot_general(a, b, (((1,), (1,)), ((), ())), preferred_element_type=F32)


def _mod_kernel(c_ref, w_ref, b_ref, o_ref):
    c = c_ref[...]
    cond = c * jax.nn.sigmoid(c)
    o_ref[0] = _dot(cond.astype(BF16), w_ref[0].astype(BF16)) + b_ref[0]


def _modulation(c, ada_w, ada_b):
    depth, d, n = ada_w.shape
    b = c.shape[0]
    rows = MOD_ROWS
    c_pad = jnp.zeros((rows, d), F32).at[:b].set(c)
    nc = 1152
    out = pl.pallas_call(
        _mod_kernel,
        grid=(depth, n // nc),
        in_specs=[
            pl.BlockSpec((rows, d), lambda i, j: (0, 0)),
            pl.BlockSpec((1, d, nc), lambda i, j: (i, 0, j)),
            pl.BlockSpec((1, 1, nc), lambda i, j: (i, 0, j)),
        ],
        out_specs=pl.BlockSpec((1, rows, nc), lambda i, j: (i, 0, j)),
        out_shape=jax.ShapeDtypeStruct((depth, rows, n), F32),
        compiler_params=_cparams(("arbitrary", "arbitrary")),
        name="adaln_mod",
    )(c_pad, ada_w, ada_b.reshape(depth, 1, n))
    return out


def _mod_spec(sel):
    layer, blk = sel
    return pl.BlockSpec((1, MOD_ROWS, D_MODEL), lambda b, i: (layer, 0, blk))


def _mod_row(ref):
    return ref[0, pl.ds(pl.program_id(0), 1), :]


def _inproj_kernel(*refs, tt, a_chunks, aug_blocks, b_chunks, b_aug, has_kv):
    if has_kv:
        (x_ref, g_ref, scl_ref, sh_ref, wa_ref, b0_ref, pv_ref, wb_ref, cb_ref,
         oa_ref, ob_ref, oc_ref, kv_ref) = refs
    else:
        x_ref, g_ref, scl_ref, sh_ref, wa_ref, b0_ref, pv_ref, wb_ref, cb_ref, oa_ref, ob_ref = refs
    x = x_ref[0]
    hh = (_rms(x, g_ref[...]) * (1.0 + _mod_row(scl_ref)) + _mod_row(sh_ref)).astype(BF16)
    t0 = pl.program_id(1) * tt
    pos = (t0 + lax.broadcasted_iota(jnp.int32, (tt, 1), 0)).astype(F32)
    lane = lax.broadcasted_iota(jnp.int32, (1, LANES), 1)

    def aug(k):
        return b0_ref[k:k + 1, :] + pos * pv_ref[k:k + 1, :]

    def put(dst, val):
        oa_ref[0, :, dst * LANES:(dst + 1) * LANES] = val.astype(oa_ref.dtype)

    for lo, width, kind, entries in a_chunks:
        acc = _dot(hh, wa_ref[:, lo:lo + width])
        if kind == "kv":
            for k in range(width // LANES):
                kv_ref[k] = acc[:, k * LANES:(k + 1) * LANES]
                for l in range(CMP_STRIDE):
                    oc_ref[0, l, :, k * LANES:(k + 1) * LANES] = kv_ref[
                        k, pl.ds(l, tt // CMP_STRIDE, stride=CMP_STRIDE), :].astype(oc_ref.dtype)
            continue
        for e in entries:
            blk = acc[:, e[1] * LANES:(e[1] + 1) * LANES]
            if e[0] == "mm":
                put(e[2], blk * e[3] if e[3] != 1.0 else blk)
            elif e[0] == "half":
                if e[3] != 1.0:
                    blk = blk * e[3]
                if e[4]:
                    blk = pltpu.roll(blk, NSA_DK, axis=1)
                put(e[2], jnp.where(lane < NSA_DK, blk, aug(e[5]) if e[5] >= 0 else 0.0))
            else:
                if e[3]:
                    blk = pltpu.roll(blk, LANES - e[3], axis=1)
                put(e[2], jnp.where(lane < GATE_COLS, blk, 0.0))
    for dst, k in aug_blocks:
        put(dst, aug(k))

    tpos = t0 + lax.broadcasted_iota(jnp.int32, (1, tt), 1)
    phi = ((tpos >> 7) << 7).astype(F32)
    plo = (tpos & 127).astype(F32)
    if b_aug:
        cc = cb_ref[...]
        key_rows = (cc[:, 0:1] * phi + cc[:, 1:2] * plo + cc[:, 2:3]).astype(ob_ref.dtype)
        for dst, nrows in b_aug:
            ob_ref[0, dst:dst + nrows, :] = key_rows[0:nrows]
    for lo, rows, pieces in b_chunks:
        acc = _dot_nt(wb_ref[lo:lo + rows, :], hh)
        for src, nrows, dst in pieces:
            ob_ref[0, dst:dst + nrows, :] = acc[src:src + nrows].astype(ob_ref.dtype)


def _inproj(x, g, mod, scl, shift, wa, b0, pv, wb, cb, plan):
    bsz, t, d = x.shape
    tt = min(ROW_TILE, t)
    a_chunks, aug_blocks, n_blocks, b_chunks, b_aug, n_rows, has_kv = plan
    na = n_blocks * LANES
    in_specs = [
        pl.BlockSpec((1, tt, d), lambda b, i: (b, i, 0)),
        _const_spec((1, d)),
        _mod_spec(scl),
        _mod_spec(shift),
        _const_spec(wa.shape),
        _const_spec(b0.shape),
        _const_spec(pv.shape),
        _const_spec(wb.shape),
        _const_spec(cb.shape),
    ]
    out_specs = [pl.BlockSpec((1, tt, na), lambda b, i: (b, i, 0)),
                 pl.BlockSpec((1, n_rows, tt), lambda b, i: (b, 0, i))]
    out_shape = [jax.ShapeDtypeStruct((bsz, t, na), BF16), jax.ShapeDtypeStruct((bsz, n_rows, t), BF16)]
    scratch = []
    if has_kv:
        kvw = 2 * NSA_KVW
        out_specs.append(pl.BlockSpec((1, CMP_STRIDE, tt // CMP_STRIDE, kvw), lambda b, i: (b, 0, i, 0)))
        out_shape.append(jax.ShapeDtypeStruct((bsz, CMP_STRIDE, t // CMP_STRIDE, kvw), BF16))
        scratch.append(pltpu.VMEM((kvw // LANES, tt, LANES), F32))
    return pl.pallas_call(
        functools.partial(_inproj_kernel, tt=tt, a_chunks=a_chunks, aug_blocks=aug_blocks, b_chunks=b_chunks,
                          b_aug=b_aug, has_kv=has_kv),
        grid=(bsz, t // tt),
        in_specs=in_specs,
        out_specs=out_specs,
        out_shape=out_shape,
        scratch_shapes=scratch,
        compiler_params=_cparams(("arbitrary", "arbitrary")),
        name="mixer_inproj",
    )(x, g.reshape(1, d), mod, mod, wa, b0, pv, wb, cb)


def _ffn_kernel(*refs, after_mixer):
    if after_mixer:
        mo_ref, wo_ref, mgate_ref, gm_ref = refs[:4]
        refs = refs[4:]
    x_ref, g1_ref, scl_ref, sh_ref, gate_ref, g2_ref, w1_ref, w2_ref, o_ref, acc_ref = refs
    x = x_ref[0]
    if after_mixer:
        x = x + _mod_row(mgate_ref) * _rms(_dot(mo_ref[0], wo_ref[...]), gm_ref[...])
    hh = (_rms(x, g1_ref[...]) * (1.0 + _mod_row(scl_ref)) + _mod_row(sh_ref)).astype(BF16)
    dff = w2_ref.shape[0]
    for c in range(dff // FF_CHUNK):
        lo = c * FF_CHUNK
        gt = _dot(hh, w1_ref[:, lo:lo + FF_CHUNK])
        up = _dot(hh, w1_ref[:, dff + lo:dff + lo + FF_CHUNK])
        act = (gt * jax.nn.sigmoid(gt) * up).astype(BF16)
        part = _dot(act, w2_ref[lo:lo + FF_CHUNK, :])
        if c == 0:
            acc_ref[...] = part
        else:
            acc_ref[...] += part
    o_ref[0] = x + 0.5 * _mod_row(gate_ref) * _rms(acc_ref[...], g2_ref[...])


def _ffn(x, g1, mod, scl, shift, gate, g2, w1, w2, mixer=None):
    bsz, t, d = x.shape
    tt = min(ROW_TILE, t)
    in_specs = [
        pl.BlockSpec((1, tt, d), lambda b, i: (b, i, 0)),
        _const_spec((1, d)), _mod_spec(scl), _mod_spec(shift), _mod_spec(gate), _const_spec((1, d)),
        _const_spec(w1.shape), _const_spec(w2.shape),
    ]
    args = [x, g1.reshape(1, d), mod, mod, mod, g2.reshape(1, d), w1, w2]
    if mixer is not None:
        mo, wo, mgate, gm = mixer
        in_specs = [pl.BlockSpec((1, tt, mo.shape[2]), lambda b, i: (b, i, 0)), _const_spec(wo.shape),
                    _mod_spec(mgate), _const_spec((1, d))] + in_specs
        args = [mo, wo, mod, gm.reshape(1, d)] + args
    return pl.pallas_call(
        functools.partial(_ffn_kernel, after_mixer=mixer is not None),
        grid=(bsz, t // tt),
        in_specs=in_specs,
        out_specs=pl.BlockSpec((1, tt, d), lambda b, i: (b, i, 0)),
        out_shape=jax.ShapeDtypeStruct((bsz, t, d), F32),
        scratch_shapes=[pltpu.VMEM((tt, d), F32)],
        compiler_params=_cparams(("arbitrary", "arbitrary")),
        name="ffn",
    )(*args)


def _online_step(s, m, l, acc, v):
    m_new = jnp.maximum(m, jnp.max(s, axis=-1, keepdims=True))
    alpha = jnp.exp2(m - m_new)
    p = jnp.exp2(s - m_new)
    l = alpha * l + jnp.sum(p, axis=-1, keepdims=True)
    acc = alpha * acc + _dot(p.astype(BF16), v)
    return m_new, l, acc


def _softmax_init(rows, width):
    return (jnp.full((rows, 1), NEG, F32), jnp.zeros((rows, 1), F32), jnp.zeros((rows, width), F32))


def _gelu_tanh(x):
    return 0.5 * x * (1.0 + jnp.tanh(math.sqrt(2.0 / math.pi) * (x + 0.044715 * (x * x * x))))


def _cmp_kernel(kv_ref, pe_ref, w1_ref, w2kt_ref, w2v_ref, kt_ref, v_ref):
    nc = kv_ref.shape[2]

    def hidden(j, g):
        lo = j * NSA_KVW + g * NSA_DK
        top = bot = None
        for l in range(CMP_STRIDE):
            x = kv_ref[0, l, :, lo:lo + NSA_DK].astype(F32)
            hi = CMP_STRIDE + l
            a = _dot((x + pe_ref[j, l:l + 1, :]).astype(BF16), w1_ref[j, l * NSA_DK:(l + 1) * NSA_DK, :])
            b = _dot((x + pe_ref[j, hi:hi + 1, :]).astype(BF16), w1_ref[j, hi * NSA_DK:(hi + 1) * NSA_DK, :])
            top = a if top is None else top + a
            bot = b if bot is None else bot + b
        return _gelu_tanh(top + pltpu.roll(bot, nc - 1, axis=0)).astype(BF16)

    n = lax.broadcasted_iota(jnp.int32, (1, nc), 1)
    end = n * CMP_STRIDE + (CMP_LEN - 1)
    phi = ((end >> 7) << 7).astype(F32)
    plo = (end & 127).astype(F32)
    r = lax.broadcasted_iota(jnp.int32, (LANES - NSA_DK, 1), 0)
    aug = jnp.where(r < 3, phi, jnp.where(r < 6, plo, jnp.where(r == 6, 1.0, 0.0))).astype(BF16)
    for g in range(NSA_KV):
        kt_ref[0, g, 0:NSA_DK, :] = _dot_nt(w2kt_ref[...], hidden(0, g)).astype(BF16)
        kt_ref[0, g, NSA_DK:LANES, :] = aug
        v_ref[0, g] = _dot(hidden(1, g), w2v_ref[...]).astype(BF16)


def _nsa_compress(kv, pe, w1, w2kt, w2v):
    bsz, _, nc, kvw = kv.shape
    return pl.pallas_call(
        _cmp_kernel,
        grid=(bsz,),
        in_specs=[pl.BlockSpec((1, CMP_STRIDE, nc, kvw), lambda b: (b, 0, 0, 0)), _const_spec(pe.shape),
                  _const_spec(w1.shape), _const_spec(w2kt.shape), _const_spec(w2v.shape)],
        out_specs=[pl.BlockSpec((1, NSA_KV, LANES, nc), lambda b: (b, 0, 0, 0)),
                   pl.BlockSpec((1, NSA_KV, nc, LANES), lambda b: (b, 0, 0, 0))],
        out_shape=[jax.ShapeDtypeStruct((bsz, NSA_KV, LANES, nc), BF16),
                   jax.ShapeDtypeStruct((bsz, NSA_KV, nc, LANES), BF16)],
        compiler_params=_cparams(("arbitrary",)),
        name="nsa_compress",
    )(kv, pe, w1, w2kt, w2v)


def _split3(x):
    hi = x.astype(BF16)
    r1 = x - hi.astype(F32)
    mid = r1.astype(BF16)
    lo = (r1 - mid.astype(F32)).astype(BF16)
    return hi, mid, lo


def _nsa_attn_kernel(qa_ref, kct_ref, vc_ref, kst_ref, vs_ref, kwt_ref, vw_ref, gt_ref, e_ref, ov_ref,
                     o_ref, tiles_ref, *, tq, tk):
    i = pl.program_id(2)
    q0 = i * tq
    rows = NSA_REP * tq
    qa = qa_ref[0]
    qs = jnp.concatenate([qa[:, r * LANES:(r + 1) * LANES] for r in range(NSA_REP)], axis=0)
    t_row = q0 + (lax.broadcasted_iota(jnp.int32, (rows, 1), 0) & (tq - 1))

    nc = kct_ref.shape[3]

    def compressed(width):
        s = _dot(qs, kct_ref[0, 0, :, 0:width])
        cend = lax.broadcasted_iota(jnp.int32, (1, width), 1) * CMP_STRIDE + (CMP_LEN - 1)
        valid = cend <= t_row
        s = jnp.where(valid, s, NEG)
        e = jnp.where(valid, jnp.exp2(s - jnp.max(s, axis=-1, keepdims=True)), 0.0)
        l = jnp.sum(e, axis=-1, keepdims=True)
        p = e / jnp.where(l > 0.0, l, 1.0)
        o = _dot(p.astype(BF16), vc_ref[0, 0, 0:width, :])
        p4 = p[0:tq]
        for r in range(1, NSA_REP):
            p4 = p4 + p[r * tq:(r + 1) * tq]
        ov = ov_ref[:, 0:width]
        return o, sum(_dot_nt(ov, part) for part in _split3(p4))

    chunk = min(CMP_CHUNK, nc)
    widths = list(range(chunk, nc + 1, chunk))
    n_ended = jnp.maximum(q0 + tq - CMP_LEN, 0) // CMP_STRIDE + 1
    o_cmp, imp = lax.switch((n_ended - 1) // chunk, [functools.partial(compressed, w) for w in widths])

    wk = WIN + tq
    st = pl.multiple_of(jnp.maximum(q0 - WIN, 0), LANES)
    col = st + lax.broadcasted_iota(jnp.int32, (1, wk), 1)
    valid = (t_row - col).astype(jnp.uint32) < WIN
    s = jnp.where(valid, _dot(qs, kwt_ref[0, :, pl.ds(st, wk)]), NEG)
    e = jnp.exp2(s - jnp.max(s, axis=-1, keepdims=True))
    o_win = _dot(e.astype(BF16), vw_ref[0, pl.ds(st, wk), :]) / jnp.sum(e, axis=-1, keepdims=True)

    cur = (q0 + lax.broadcasted_iota(jnp.int32, (1, tq), 1)) >> 6
    sid = lax.broadcasted_iota(jnp.int32, (N_SEL_PAD, 1), 0)
    forced = (sid == 0) | (sid == cur) | (sid == cur - 1)
    work = jnp.where(forced | (sid > cur), -3.0e38, imp)
    selb = jnp.where(forced, 0.0, NEG)
    for _ in range(SEL_TOPK - 3):
        mx = jnp.max(work, axis=0, keepdims=True)
        first = jnp.min(jnp.where(work == mx, sid, N_SEL_PAD), axis=0, keepdims=True)
        pick = sid == first
        selb = jnp.where(pick, 0.0, selb)
        work = jnp.where(pick, -3.0e38, work)
    selb = selb.T

    jd = q0 // tk
    n_tiles = kst_ref.shape[2] // tk
    any_sel = jnp.max(jnp.where(selb == 0.0, 1.0, 0.0), axis=0, keepdims=True)
    blk_tile = (lax.broadcasted_iota(jnp.int32, (N_SEL_PAD, LANES), 0) // (tk // SEL_LEN)
                == lax.broadcasted_iota(jnp.int32, (N_SEL_PAD, LANES), 1)).astype(BF16)
    tile_cnt = _dot(jnp.broadcast_to(any_sel, (16, N_SEL_PAD)).astype(BF16), blk_tile)
    n_act = jnp.int32(0)
    for j in range(n_tiles):
        tiles_ref[n_act] = j
        n_act = n_act + ((tile_cnt[0, j] > 0.0) & (j < jd)).astype(jnp.int32)

    selb4 = jnp.concatenate([selb.astype(BF16)] * NSA_REP, axis=0)
    qsel = jnp.concatenate([qs, selb4], axis=1)

    def sel_scores(st, width):
        return _dot(qsel, jnp.concatenate([kst_ref[0, :, pl.ds(st, width)], e_ref[:, pl.ds(st, width)]], axis=0))

    def sel_body(it, carry):
        st = pl.multiple_of(tiles_ref[it] * tk, tk)
        return _online_step(sel_scores(st, tk), *carry, vs_ref[0, pl.ds(st, tk), :])

    carry = lax.fori_loop(0, n_act, sel_body, _softmax_init(rows, LANES))
    st = pl.multiple_of(jd * tk, tk)

    def diagonal(width):
        col = st + lax.broadcasted_iota(jnp.int32, (1, width), 1)
        s = jnp.where(col <= t_row, sel_scores(st, width), NEG)
        _, l, acc = _online_step(s, *carry, vs_ref[0, pl.ds(st, width), :])
        return acc / l

    o_sel = lax.switch((q0 - st) // tq, [functools.partial(diagonal, w) for w in range(tq, tk + 1, tq)])

    gs = jax.nn.sigmoid(gt_ref[0].astype(F32))
    outs = []
    for r in range(NSA_REP):
        sl = slice(r * tq, (r + 1) * tq)
        outs.append(gs[:, 3 * r:3 * r + 1] * o_cmp[sl] + gs[:, 3 * r + 1:3 * r + 2] * o_sel[sl]
                    + gs[:, 3 * r + 2:3 * r + 3] * o_win[sl])
    pairs = [outs[2 * a] + pltpu.roll(outs[2 * a + 1], NSA_DK, axis=1) for a in range(NSA_REP // 2)]
    o_ref[0] = jnp.concatenate(pairs, axis=1).astype(o_ref.dtype)


def _nsa_attention(oa, ob, kct, vcp, e_mat, ov_mat, t):
    bsz = oa.shape[0]
    tq, tk = NSA_TQ, NSA_TK
    qw = NSA_REP * LANES
    vs_blk = NSA_HEADS
    vw_blk = vs_blk + NSA_KV
    gt_blk = vw_blk + NSA_KV
    nc = kct.shape[3]
    return pl.pallas_call(
        functools.partial(_nsa_attn_kernel, tq=tq, tk=tk),
        grid=(bsz, NSA_KV, t // tq),
        in_specs=[
            pl.BlockSpec((1, tq, qw), lambda b, g, i: (b, i, g)),
            pl.BlockSpec((1, 1, LANES, nc), lambda b, g, i: (b, g, 0, 0)),
            pl.BlockSpec((1, 1, nc, LANES), lambda b, g, i: (b, g, 0, 0)),
            pl.BlockSpec((1, LANES, t), lambda b, g, i: (b, g, 0)),
            pl.BlockSpec((1, t, LANES), lambda b, g, i: (b, 0, vs_blk + g)),
            pl.BlockSpec((1, LANES, t), lambda b, g, i: (b, NSA_KV + g, 0)),
            pl.BlockSpec((1, t, LANES), lambda b, g, i: (b, 0, vw_blk + g)),
            pl.BlockSpec((1, tq, LANES), lambda b, g, i: (b, i, gt_blk + g)),
            _const_spec(e_mat.shape),
            _const_spec(ov_mat.shape),
        ],
        out_specs=pl.BlockSpec((1, tq, NSA_REP * NSA_DK), lambda b, g, i: (b, i, g)),
        out_shape=jax.ShapeDtypeStruct((bsz, t, NSA_QW), BF16),
        scratch_shapes=[pltpu.SMEM((t // tk,), jnp.int32)],
        compiler_params=_cparams(("arbitrary", "arbitrary", "arbitrary")),
        name="nsa_attention",
    )(oa, kct, vcp, ob, oa, ob, oa, oa, e_mat, ov_mat)


def _sb_kernel(q_ref, kt_ref, v_ref, o_ref, rem_ref, acc_ref, *, tq, tk):
    i = pl.program_id(2)
    q0 = i * tq
    q = q_ref[0]
    lane = lax.broadcasted_iota(jnp.int32, (1, LANES), 1)
    zero = jnp.zeros_like(q)
    qs = jnp.concatenate([jnp.where(lane < SB_DH, q, zero), jnp.where(lane >= SB_DH, q, zero)], axis=0)
    t_row = q0 + (lax.broadcasted_iota(jnp.int32, (2 * tq, 1), 0) & (tq - 1))
    upper = (lax.broadcasted_iota(jnp.int32, (tk, tk), 0) > lax.broadcasted_iota(jnp.int32, (tk, tk), 1)
             ).astype(BF16)
    rem_ref[...] = jnp.zeros(rem_ref.shape, F32)
    acc_ref[...] = jnp.zeros(acc_ref.shape, F32)

    def tiles(js, masked):
        sts = [pl.multiple_of(j * tk, tk) for j in js]
        zs = [_dot(qs, kt_ref[0, :, pl.ds(st, tk)]) for st in sts]
        costs, log_bs, befores = [], [], []
        for st, z in zip(sts, zs):
            cost = jnp.where(z > SB_LINEAR_FROM, z, jnp.log(1.0 + jnp.exp2(z)) * LOG2E)
            log_bs.append(z - cost)
            before = None
            if masked:
                before = (st + lax.broadcasted_iota(jnp.int32, (1, tk), 1)) < t_row
                cost = jnp.where(before, cost, 0.0)
            costs.append(cost.astype(BF16))
            befores.append(before)
        rems = [_dot(c, upper) for c in costs]
        rem_right = rem_ref[...]
        probs = []
        for log_b, c, rem, before in zip(log_bs, costs, rems, befores):
            ex = log_b - (rem + rem_right)
            if masked:
                ex = jnp.where(before, ex, NEG)
            probs.append(jnp.exp2(ex).astype(BF16))
            rem_right = rem_right + (rem[:, 0:1] + c[:, 0:1].astype(F32))
        rem_ref[...] = rem_right
        acc = acc_ref[...]
        for st, p in zip(sts, probs):
            acc = acc + _dot(p, v_ref[0, pl.ds(st, tk), :])
        acc_ref[...] = acc
        return jnp.min(rem_right)

    per_step = tq // tk
    n_full = q0 // tk
    spent = tiles([n_full + per_step - 1 - d for d in range(per_step)], True)

    def go_on(c):
        return (c[0] < n_full // per_step) & (c[1] < SB_DEAD)

    def body(c):
        it = c[0]
        return it + 1, tiles([n_full - 1 - per_step * it - d for d in range(per_step)], False)

    lax.while_loop(go_on, body, (jnp.int32(0), spent))
    acc = acc_ref[...]
    o_ref[0] = jnp.where(lane < SB_DH, acc[0:tq], acc[tq:2 * tq]).astype(o_ref.dtype)


def _sb_attention(oa, ob, t):
    bsz = oa.shape[0]
    tq, tk = SB_TQ, SB_TK
    npair = SB_HEADS // 2
    return pl.pallas_call(
        functools.partial(_sb_kernel, tq=tq, tk=tk),
        grid=(bsz, npair, t // tq),
        in_specs=[
            pl.BlockSpec((1, tq, LANES), lambda b, h, i: (b, i, h)),
            pl.BlockSpec((1, LANES, t), lambda b, h, i: (b, h, 0)),
            pl.BlockSpec((1, t, LANES), lambda b, h, i: (b, 0, npair + h)),
        ],
        out_specs=pl.BlockSpec((1, tq, LANES), lambda b, h, i: (b, i, h)),
        out_shape=jax.ShapeDtypeStruct((bsz, t, D_MODEL), BF16),
        scratch_shapes=[pltpu.VMEM((2 * tq, 1), F32), pltpu.VMEM((2 * tq, LANES), F32)],
        compiler_params=_cparams(("arbitrary", "arbitrary", "arbitrary")),
        name="sb_attention",
    )(oa, ob, oa)


def _diff_kernel(slope_ref, lam_ref, g_ref, qa_ref, kt_ref, v_ref, o_ref, kmax_ref, *, tq, tk, lam_init):
    h = pl.program_id(1)
    i = pl.program_id(2)
    q0 = i * tq
    n_tiles = kt_ref.shape[2] // tk

    @pl.when(i == 0)
    def _():
        run = jnp.float32(0.0)
        for j in range(n_tiles):
            kf = kt_ref[0, 0:LANES, j * tk:(j + 1) * tk].astype(F32)
            sq = kf * kf
            norms = jnp.sqrt(jnp.maximum(jnp.sum(sq[0:DIFF_DH], axis=0, keepdims=True),
                                         jnp.sum(sq[DIFF_DH:LANES], axis=0, keepdims=True)))
            run = jnp.maximum(run, jnp.max(norms))
            kmax_ref[j] = run

    qa = qa_ref[0]
    lane = lax.broadcasted_iota(jnp.int32, (1, 2 * LANES), 1)
    zero = jnp.zeros_like(qa)
    qs = jnp.concatenate(
        [jnp.where(((lane >= m_ * DIFF_DH) & (lane < (m_ + 1) * DIFF_DH)) | (lane >= LANES), qa, zero)
         for m_ in range(2)], axis=0)
    t_row = q0 + (lax.broadcasted_iota(jnp.int32, (2 * tq, 1), 0) & (tq - 1))

    def scores(j):
        return _dot(qs, kt_ref[0, :, pl.ds(pl.multiple_of(j * tk, tk), tk)])

    def values(j):
        return v_ref[0, pl.ds(pl.multiple_of(j * tk, tk), tk), :]

    jd = q0 // tk
    col = jd * tk + lax.broadcasted_iota(jnp.int32, (1, tk), 1)
    state = _online_step(jnp.where(col <= t_row, scores(jd), NEG), *_softmax_init(2 * tq, LANES), values(jd))
    qf = jnp.where(lane < LANES, qs, jnp.zeros_like(qs)).astype(F32)
    q_norm = jnp.max(jnp.sqrt(jnp.sum(qf * qf, axis=-1, keepdims=True)))
    row_col = 2 * DIFF_DH + ALIBI_COLS - 1
    row_const = qs[:, row_col:row_col + 1].astype(F32)
    slope = slope_ref[h]

    def live(c):
        j = jnp.maximum(c[0], 0)
        bound = q_norm * kmax_ref[j] * 1.001 + slope * ((j + 1) * tk - 1).astype(F32)
        return (c[0] >= 0) & (bound - c[4] > -DIFF_DEAD)

    def body(c):
        m, l, acc = _online_step(scores(c[0]), c[1], c[2], c[3], values(c[0]))
        return c[0] - 1, m, l, acc, jnp.min(m - row_const)

    _, _, l, acc, _ = lax.while_loop(live, body, (jd - 1, *state, jnp.min(state[0] - row_const)))
    out = acc / l
    lam = lam_ref[...]
    lam_full = (jnp.exp(jnp.sum(lam[0:1] * lam[1:2], axis=-1, keepdims=True))
                - jnp.exp(jnp.sum(lam[2:3] * lam[3:4], axis=-1, keepdims=True)) + lam_init)
    o = out[0:tq] - lam_full * out[tq:2 * tq]
    o_ref[0] = (_rms(o, g_ref[...]) * (1.0 - lam_init)).astype(o_ref.dtype)


def _diff_attention(oa, ob, lam, subln_g, lam_init, t):
    bsz = oa.shape[0]
    tq, tk = DIFF_TQ, DIFF_TK
    v_blk = DIFF_HEADS * 2
    slopes = _alibi_slopes(DIFF_HEADS) * LOG2E
    return pl.pallas_call(
        functools.partial(_diff_kernel, tq=tq, tk=tk, lam_init=lam_init),
        grid=(bsz, DIFF_HEADS, t // tq),
        in_specs=[
            pl.BlockSpec(memory_space=pltpu.SMEM),
            _const_spec(lam.shape),
            _const_spec((1, 2 * DIFF_DH)),
            pl.BlockSpec((1, tq, 2 * LANES), lambda b, h, i: (b, i, h)),
            pl.BlockSpec((1, 2 * LANES, t), lambda b, h, i: (b, h, 0)),
            pl.BlockSpec((1, t, LANES), lambda b, h, i: (b, 0, v_blk + h)),
        ],
        out_specs=pl.BlockSpec((1, tq, LANES), lambda b, h, i: (b, i, h)),
        out_shape=jax.ShapeDtypeStruct((bsz, t, D_MODEL), BF16),
        scratch_shapes=[pltpu.SMEM((t // tk,), F32)],
        compiler_params=_cparams(("arbitrary", "arbitrary", "arbitrary")),
        name="diff_attention",
    )(slopes, lam, subln_g.reshape(1, 2 * DIFF_DH), oa, ob, oa)


def _alibi_slopes(n):
    return jnp.exp2(-8.0 * jnp.arange(1, n + 1, dtype=F32) / n)


def _alibi_query_cols(slopes, first):
    sl2 = slopes * LOG2E
    parts = jnp.stack([p.astype(F32) for p in _split3(sl2)], axis=1)
    b0 = jnp.pad(jnp.concatenate([parts, parts], axis=1), ((0, 0), (first, LANES - first - 6)))
    pv = jnp.pad(-sl2[:, None], ((0, 0), (first + 6, LANES - first - ALIBI_COLS)))
    return b0, pv


def _alibi_key_rows():
    c = np.zeros((LANES, 8), np.float32)
    c[0:3, 0] = 1.0
    c[3:6, 1] = 1.0
    c[6, 2] = 1.0
    return jnp.asarray(c)


def _nsa_weights(w_in):
    bounds = [0, NSA_QW] + [NSA_QW + (k + 1) * NSA_KVW for k in range(6)]
    wq, wkc, wvc, wks, wvs, wkw, wvw = [w_in[:, bounds[k]:bounds[k + 1]] for k in range(7)]
    wg = jnp.pad(w_in[:, bounds[7]:], ((0, 0), (0, LANES - NSA_KV * GATE_COLS)))
    wa = jnp.concatenate([wq, wkc, wvc, wvs, wvw, wg], axis=1).astype(BF16)
    scale = NSA_DK ** -0.5 * LOG2E
    per = ROW_TILE // LANES
    qblocks = NSA_QW // LANES
    a_chunks = []
    for c in range(qblocks // per):
        ents = tuple(("half", b, 2 * (c * per + b) + hi, scale, hi, 2 * (c * per + b) + hi)
                     for b in range(per) for hi in range(2))
        a_chunks.append((c * ROW_TILE, ROW_TILE, "mm", ents))
    a_chunks.append((NSA_QW, 2 * NSA_KVW, "kv", ()))
    v0 = NSA_QW + 2 * NSA_KVW
    vs_blk, vw_blk, gt_blk = NSA_HEADS, NSA_HEADS + NSA_KV, NSA_HEADS + 2 * NSA_KV
    ents = tuple(("half", g // 2, vs_blk + g, 1.0, g % 2, -1) for g in range(NSA_KV)) + tuple(
        ("half", NSA_KV // 2 + g // 2, vw_blk + g, 1.0, g % 2, -1) for g in range(NSA_KV))
    a_chunks.append((v0, 2 * NSA_KVW, "mm", ents))
    a_chunks.append((v0 + 2 * NSA_KVW, LANES, "mm",
                     tuple(("gate", 0, gt_blk + g, GATE_COLS * g) for g in range(NSA_KV))))
    b0, pv = _alibi_query_cols(_alibi_slopes(NSA_HEADS), NSA_DK)
    wb = jnp.concatenate([wks, wkw], axis=1).T.astype(BF16)
    pieces = tuple((NSA_DK * g, NSA_DK, LANES * g) for g in range(2 * NSA_KV))
    b_aug = tuple((LANES * g + NSA_DK, LANES - NSA_DK) for g in range(2 * NSA_KV))
    plan = (tuple(a_chunks), (), gt_blk + NSA_KV, ((0, 2 * NSA_KVW, pieces),), b_aug, 2 * NSA_KV * LANES, True)
    return wa, b0, pv, wb, _alibi_key_rows(), plan


def _sb_weights(w_in):
    d = w_in.shape[0]
    wq, wk, wv = jnp.split(w_in, 3, axis=1)
    wa = jnp.concatenate([wq, wv], axis=1).astype(BF16)
    per = ROW_TILE // LANES
    nq = d // ROW_TILE
    a_chunks = tuple((c * ROW_TILE, ROW_TILE, "mm",
                      tuple(("mm", b, c * per + b, SB_DH ** -0.5 * LOG2E if c < nq else 1.0) for b in range(per)))
                     for c in range(2 * nq))
    b_chunks = tuple((c * ROW_TILE, ROW_TILE, ((0, ROW_TILE, c * ROW_TILE),)) for c in range(nq))
    zero = jnp.asarray(np.zeros((1, LANES), np.float32))
    plan = (a_chunks, (), 2 * d // LANES, b_chunks, (), d, False)
    return wa, zero, zero, wk.T.astype(BF16), jnp.asarray(np.zeros((LANES, 8), np.float32)), plan


def _diff_weights(w_in):
    d = w_in.shape[0]
    wq, wk, wv = jnp.split(w_in, 3, axis=1)
    wa = jnp.concatenate([wq, wv], axis=1).astype(BF16)
    scale = DIFF_DH ** -0.5 * LOG2E
    per = ROW_TILE // LANES
    nq = d // ROW_TILE
    a_chunks = tuple(
        (c * ROW_TILE, ROW_TILE, "mm",
         tuple(("mm", b, 2 * (c * per + b), scale) if c < nq else ("mm", b, 2 * DIFF_HEADS + (c - nq) * per + b, 1.0)
               for b in range(per)))
        for c in range(2 * nq))
    aug_blocks = tuple((2 * h + 1, h) for h in range(DIFF_HEADS))
    b0, pv = _alibi_query_cols(_alibi_slopes(DIFF_HEADS), 0)
    b_chunks = tuple((c * ROW_TILE, ROW_TILE,
                      tuple((LANES * k, LANES, 2 * LANES * (c * per + k)) for k in range(per)))
                     for c in range(nq))
    b_aug = tuple((2 * LANES * h + LANES, LANES) for h in range(DIFF_HEADS))
    plan = (a_chunks, aug_blocks, 3 * DIFF_HEADS, b_chunks, b_aug, 2 * DIFF_HEADS * LANES, False)
    return wa, b0, pv, wk.T.astype(BF16), _alibi_key_rows(), plan


def _nsa_mixer(x, g, mod, scl, shift, w_in, cmp_pe, cmp_w1, cmp_w2):
    t = x.shape[1]
    wa, b0, pv, wb, cb, plan = _nsa_weights(w_in)
    oa, ob, kv = _inproj(x, g, mod, scl, shift, wa, b0, pv, wb, cb, plan)
    nc = t // CMP_STRIDE
    w2kt = cmp_w2[0].T.astype(BF16)
    w2v = jnp.pad(cmp_w2[1], ((0, 0), (0, LANES - NSA_DK))).astype(BF16)
    kct, vcp = _nsa_compress(kv, cmp_pe, cmp_w1.astype(BF16), w2kt, w2v)
    tok = np.arange(t)
    e_mat = jnp.asarray(np.arange(N_SEL_PAD)[:, None] == (tok // SEL_LEN)[None, :], BF16)
    cstart = np.arange(nc) * CMP_STRIDE
    sstart = np.arange(N_SEL_PAD) * SEL_LEN
    ov_mat = jnp.asarray((cstart[None, :] < sstart[:, None] + SEL_LEN)
                         & (cstart[None, :] + CMP_LEN > sstart[:, None])
                         & (np.arange(nc)[None, :] < nc - 1), BF16)
    return _nsa_attention(oa, ob, kct, vcp, e_mat, ov_mat, t)


def _sb_mixer(x, g, mod, scl, shift, w_in):
    t = x.shape[1]
    wa, b0, pv, wb, cb, plan = _sb_weights(w_in)
    oa, ob = _inproj(x, g, mod, scl, shift, wa, b0, pv, wb, cb, plan)
    return _sb_attention(oa, ob, t)


def _diff_mixer(x, g, mod, scl, shift, w_in, lam, subln_g, layer_idx):
    t = x.shape[1]
    wa, b0, pv, wb, cb, plan = _diff_weights(w_in)
    oa, ob = _inproj(x, g, mod, scl, shift, wa, b0, pv, wb, cb, plan)
    lam_init = 0.8 - 0.6 * math.exp(-0.3 * layer_idx)
    return _diff_attention(oa, ob, lam, subln_g, lam_init, t)


def kernel(x, c, ada_w, ada_b, norm_g, ffn_w1, ffn_w2, nsa_w_in, nsa_cmp_pe, nsa_cmp_w1, nsa_cmp_w2,
           nsa_w_out, sb_w_in, sb_w_out, diff_w_in, diff_lam, diff_subln_g, diff_w_out):
    bsz, t, d = x.shape
    assert d == D_MODEL and t % ROW_TILE == 0 and t >= WIN + 128 and t <= N_SEL_PAD * SEL_LEN
    mod = _modulation(c, ada_w, ada_b)
    for i in range(DEPTH):
        def mods(sidx):
            return tuple((i, 3 * sidx + k) for k in range(3))

        shift, scl, gate = mods(0)
        x = _ffn(x, norm_g[i, 0], mod, scl, shift, gate, norm_g[i, 1],
                 ffn_w1[i, 0].astype(BF16), ffn_w2[i, 0].astype(BF16))
        shift, scl, gate = mods(1)
        kind, j = i % N_MIXERS, i // N_MIXERS
        if kind == 0:
            o = _nsa_mixer(x, norm_g[i, 2], mod, scl, shift, nsa_w_in[j], nsa_cmp_pe[j], nsa_cmp_w1[j],
                           nsa_cmp_w2[j])
            w_out = nsa_w_out[j]
        elif kind == 1:
            o = _sb_mixer(x, norm_g[i, 2], mod, scl, shift, sb_w_in[j])
            w_out = sb_w_out[j]
        else:
            o = _diff_mixer(x, norm_g[i, 2], mod, scl, shift, diff_w_in[j], diff_lam[j], diff_subln_g[j], i)
            w_out = diff_w_out[j]
        mixer = (o, w_out.astype(BF16), gate, norm_g[i, 3])
        shift, scl, gate = mods(2)
        x = _ffn(x, norm_g[i, 4], mod, scl, shift, gate, norm_g[i, 5],
                 ffn_w1[i, 1].astype(BF16), ffn_w2[i, 1].astype(BF16), mixer=mixer)
    return x
```

```python
import functools
import math

import jax
import jax.numpy as jnp
import numpy as np
from jax import lax
from jax.experimental import pallas as pl
from jax.experimental.pallas import tpu as pltpu

F32 = jnp.float32
BF16 = jnp.bfloat16

D_MODEL = 1024
DEPTH = 4
N_MIXERS = 3
EPS = 1e-6
NEG = -1e30
FORCE = 1e4
LOG2E = 1.4426950408889634
NSA_HEADS = 16
NSA_DK = 64
NSA_KV = 4
NSA_REP = NSA_HEADS // NSA_KV
CMP_LEN = 32
CMP_STRIDE = 16
CMP_HID = 256
SEL_LEN = 64
SEL_TOPK = 16
WIN = 512
NSA_QW = NSA_HEADS * NSA_DK
NSA_KVW = NSA_KV * NSA_DK
N_SEL_PAD = 128
SB_HEADS = 16
SB_DH = D_MODEL // SB_HEADS
DIFF_HEADS = 8
DIFF_DH = D_MODEL // (2 * DIFF_HEADS)
D_FF = 2816

LANES = 128
MOD_ROWS = 8
ROW_TILE = 512
FF_CHUNK = 256
DIFF_TQ, DIFF_TK = 512, 512
SB_TQ, SB_TK = 256, 256
SB_LINEAR_FROM = 64.0
SB_DEAD = 192.0
DIFF_DEAD = 160.0
ALIBI_COLS = 7
GATE_COLS = NSA_REP * 3
NSA_TQ, NSA_TK = 256, 512
CMP_CHUNK = 128
VMEM_LIMIT = 56 * 1024 * 1024


def _cparams(sem):
    return pltpu.CompilerParams(dimension_semantics=sem, vmem_limit_bytes=VMEM_LIMIT)


def _const_spec(shape):
    n = len(shape)
    return pl.BlockSpec(shape, lambda *_: (0,) * n)


def _rms(x, g):
    return x * lax.rsqrt(jnp.mean(x * x, axis=-1, keepdims=True) + EPS) * g


def _dot(a, b):
    return jnp.dot(a, b, preferred_element_type=F32)


def _dot_nt(a, b):
    return lax.dot_general(a, b, (((1,), (1,)), ((), ())), preferred_element_type=F32)


def _mod_kernel(c_ref, w_ref, b_ref, o_ref):
    c = c_ref[...]
    cond = c * jax.nn.sigmoid(c)
    o_ref[0] = _dot(cond.astype(BF16), w_ref[0].astype(BF16)) + b_ref[0]


def _modulation(c, ada_w, ada_b):
    depth, d, n = ada_w.shape
    b = c.shape[0]
    rows = MOD_ROWS
    c_pad = jnp.zeros((rows, d), F32).at[:b].set(c)
    nc = 1152
    out = pl.pallas_call(
        _mod_kernel,
        grid=(depth, n // nc),
        in_specs=[
            pl.BlockSpec((rows, d), lambda i, j: (0, 0)),
            pl.BlockSpec((1, d, nc), lambda i, j: (i, 0, j)),
            pl.BlockSpec((1, 1, nc), lambda i, j: (i, 0, j)),
        ],
        out_specs=pl.BlockSpec((1, rows, nc), lambda i, j: (i, 0, j)),
        out_shape=jax.ShapeDtypeStruct((depth, rows, n), F32),
        compiler_params=_cparams(("arbitrary", "arbitrary")),
        name="adaln_mod",
    )(c_pad, ada_w, ada_b.reshape(depth, 1, n))
    return out


def _mod_spec(sel):
    layer, blk = sel
    return pl.BlockSpec((1, MOD_ROWS, D_MODEL), lambda b, i: (layer, 0, blk))


def _mod_row(ref):
    return ref[0, pl.ds(pl.program_id(0), 1), :]


def _inproj_kernel(*refs, tt, a_chunks, aug_blocks, b_chunks, b_aug, has_kv):
    if has_kv:
        (x_ref, g_ref, scl_ref, sh_ref, wa_ref, b0_ref, pv_ref, wb_ref, cb_ref,
         oa_ref, ob_ref, oc_ref, kv_ref) = refs
    else:
        x_ref, g_ref, scl_ref, sh_ref, wa_ref, b0_ref, pv_ref, wb_ref, cb_ref, oa_ref, ob_ref = refs
    x = x_ref[0]
    hh = (_rms(x, g_ref[...]) * (1.0 + _mod_row(scl_ref)) + _mod_row(sh_ref)).astype(BF16)
    t0 = pl.program_id(1) * tt
    pos = (t0 + lax.broadcasted_iota(jnp.int32, (tt, 1), 0)).astype(F32)
    lane = lax.broadcasted_iota(jnp.int32, (1, LANES), 1)

    def aug(k):
        return b0_ref[k:k + 1, :] + pos * pv_ref[k:k + 1, :]

    def put(dst, val):
        oa_ref[0, :, dst * LANES:(dst + 1) * LANES] = val.astype(oa_ref.dtype)

    for lo, width, kind, entries in a_chunks:
        acc = _dot(hh, wa_ref[:, lo:lo + width])
        if kind == "kv":
            for k in range(width // LANES):
                kv_ref[k] = acc[:, k * LANES:(k + 1) * LANES]
                for l in range(CMP_STRIDE):
                    oc_ref[0, l, :, k * LANES:(k + 1) * LANES] = kv_ref[
                        k, pl.ds(l, tt // CMP_STRIDE, stride=CMP_STRIDE), :].astype(oc_ref.dtype)
            continue
        for e in entries:
            blk = acc[:, e[1] * LANES:(e[1] + 1) * LANES]
            if e[0] == "mm":
                put(e[2], blk * e[3] if e[3] != 1.0 else blk)
            elif e[0] == "half":
                if e[3] != 1.0:
                    blk = blk * e[3]
                if e[4]:
                    blk = pltpu.roll(blk, NSA_DK, axis=1)
                put(e[2], jnp.where(lane < NSA_DK, blk, aug(e[5]) if e[5] >= 0 else 0.0))
            else:
                if e[3]:
                    blk = pltpu.roll(blk, LANES - e[3], axis=1)
                put(e[2], jnp.where(lane < GATE_COLS, blk, 0.0))
    for dst, k in aug_blocks:
        put(dst, aug(k))

    tpos = t0 + lax.broadcasted_iota(jnp.int32, (1, tt), 1)
    phi = ((tpos >> 7) << 7).astype(F32)
    plo = (tpos & 127).astype(F32)
    if b_aug:
        cc = cb_ref[...]
        key_rows = (cc[:, 0:1] * phi + cc[:, 1:2] * plo + cc[:, 2:3]).astype(ob_ref.dtype)
        for dst, nrows in b_aug:
            ob_ref[0, dst:dst + nrows, :] = key_rows[0:nrows]
    for lo, rows, pieces in b_chunks:
        acc = _dot_nt(wb_ref[lo:lo + rows, :], hh)
        for src, nrows, dst in pieces:
            ob_ref[0, dst:dst + nrows, :] = acc[src:src + nrows].astype(ob_ref.dtype)


def _inproj(x, g, mod, scl, shift, wa, b0, pv, wb, cb, plan):
    bsz, t, d = x.shape
    tt = min(ROW_TILE, t)
    a_chunks, aug_blocks, n_blocks, b_chunks, b_aug, n_rows, has_kv = plan
    na = n_blocks * LANES
    in_specs = [
        pl.BlockSpec((1, tt, d), lambda b, i: (b, i, 0)),
        _const_spec((1, d)),
        _mod_spec(scl),
        _mod_spec(shift),
        _const_spec(wa.shape),
        _const_spec(b0.shape),
        _const_spec(pv.shape),
        _const_spec(wb.shape),
        _const_spec(cb.shape),
    ]
    out_specs = [pl.BlockSpec((1, tt, na), lambda b, i: (b, i, 0)),
                 pl.BlockSpec((1, n_rows, tt), lambda b, i: (b, 0, i))]
    out_shape = [jax.ShapeDtypeStruct((bsz, t, na), BF16), jax.ShapeDtypeStruct((bsz, n_rows, t), BF16)]
    scratch = []
    if has_kv:
        kvw = 2 * NSA_KVW
        out_specs.append(pl.BlockSpec((1, CMP_STRIDE, tt // CMP_STRIDE, kvw), lambda b, i: (b, 0, i, 0)))
        out_shape.append(jax.ShapeDtypeStruct((bsz, CMP_STRIDE, t // CMP_STRIDE, kvw), BF16))
        scratch.append(pltpu.VMEM((kvw // LANES, tt, LANES), F32))
    return pl.pallas_call(
        functools.partial(_inproj_kernel, tt=tt, a_chunks=a_chunks, aug_blocks=aug_blocks, b_chunks=b_chunks,
                          b_aug=b_aug, has_kv=has_kv),
        grid=(bsz, t // tt),
        in_specs=in_specs,
        out_specs=out_specs,
        out_shape=out_shape,
        scratch_shapes=scratch,
        compiler_params=_cparams(("arbitrary", "arbitrary")),
        name="mixer_inproj",
    )(x, g.reshape(1, d), mod, mod, wa, b0, pv, wb, cb)


def _ffn_kernel(*refs, after_mixer):
    if after_mixer:
        mo_ref, wo_ref, mgate_ref, gm_ref = refs[:4]
        refs = refs[4:]
    x_ref, g1_ref, scl_ref, sh_ref, gate_ref, g2_ref, w1_ref, w2_ref, o_ref, acc_ref = refs
    x = x_ref[0]
    if after_mixer:
        x = x + _mod_row(mgate_ref) * _rms(_dot(mo_ref[0], wo_ref[...]), gm_ref[...])
    hh = (_rms(x, g1_ref[...]) * (1.0 + _mod_row(scl_ref)) + _mod_row(sh_ref)).astype(BF16)
    dff = w2_ref.shape[0]
    for c in range(dff // FF_CHUNK):
        lo = c * FF_CHUNK
        gt = _dot(hh, w1_ref[:, lo:lo + FF_CHUNK])
        up = _dot(hh, w1_ref[:, dff + lo:dff + lo + FF_CHUNK])
        act = (gt * jax.nn.sigmoid(gt) * up).astype(BF16)
        part = _dot(act, w2_ref[lo:lo + FF_CHUNK, :])
        if c == 0:
            acc_ref[...] = part
        else:
            acc_ref[...] += part
    o_ref[0] = x + 0.5 * _mod_row(gate_ref) * _rms(acc_ref[...], g2_ref[...])


def _ffn(x, g1, mod, scl, shift, gate, g2, w1, w2, mixer=None):
    bsz, t, d = x.shape
    tt = min(ROW_TILE, t)
    in_specs = [
        pl.BlockSpec((1, tt, d), lambda b, i: (b, i, 0)),
        _const_spec((1, d)), _mod_spec(scl), _mod_spec(shift), _mod_spec(gate), _const_spec((1, d)),
        _const_spec(w1.shape), _const_spec(w2.shape),
    ]
    args = [x, g1.reshape(1, d), mod, mod, mod, g2.reshape(1, d), w1, w2]
    if mixer is not None:
        mo, wo, mgate, gm = mixer
        in_specs = [pl.BlockSpec((1, tt, mo.shape[2]), lambda b, i: (b, i, 0)), _const_spec(wo.shape),
                    _mod_spec(mgate), _const_spec((1, d))] + in_specs
        args = [mo, wo, mod, gm.reshape(1, d)] + args
    return pl.pallas_call(
        functools.partial(_ffn_kernel, after_mixer=mixer is not None),
        grid=(bsz, t // tt),
        in_specs=in_specs,
        out_specs=pl.BlockSpec((1, tt, d), lambda b, i: (b, i, 0)),
        out_shape=jax.ShapeDtypeStruct((bsz, t, d), F32),
        scratch_shapes=[pltpu.VMEM((tt, d), F32)],
        compiler_params=_cparams(("arbitrary", "arbitrary")),
        name="ffn",
    )(*args)


def _online_step(s, m, l, acc, v):
    m_new = jnp.maximum(m, jnp.max(s, axis=-1, keepdims=True))
    alpha = jnp.exp2(m - m_new)
    p = jnp.exp2(s - m_new)
    l = alpha * l + jnp.sum(p, axis=-1, keepdims=True)
    acc = alpha * acc + _dot(p.astype(BF16), v)
    return m_new, l, acc


def _softmax_init(rows, width):
    return (jnp.full((rows, 1), NEG, F32), jnp.zeros((rows, 1), F32), jnp.zeros((rows, width), F32))


def _gelu_tanh(x):
    return 0.5 * x * (1.0 + jnp.tanh(math.sqrt(2.0 / math.pi) * (x + 0.044715 * (x * x * x))))


def _cmp_kernel(kv_ref, pe_ref, w1_ref, w2kt_ref, w2v_ref, kt_ref, v_ref):
    nc = kv_ref.shape[2]

    def hidden(j, g):
        lo = j * NSA_KVW + g * NSA_DK
        top = bot = None
        for l in range(CMP_STRIDE):
            x = kv_ref[0, l, :, lo:lo + NSA_DK].astype(F32)
            hi = CMP_STRIDE + l
            a = _dot((x + pe_ref[j, l:l + 1, :]).astype(BF16), w1_ref[j, l * NSA_DK:(l + 1) * NSA_DK, :])
            b = _dot((x + pe_ref[j, hi:hi + 1, :]).astype(BF16), w1_ref[j, hi * NSA_DK:(hi + 1) * NSA_DK, :])
            top = a if top is None else top + a
            bot = b if bot is None else bot + b
        return _gelu_tanh(top + pltpu.roll(bot, nc - 1, axis=0)).astype(BF16)

    n = lax.broadcasted_iota(jnp.int32, (1, nc), 1)
    end = n * CMP_STRIDE + (CMP_LEN - 1)
    phi = ((end >> 7) << 7).astype(F32)
    plo = (end & 127).astype(F32)
    r = lax.broadcasted_iota(jnp.int32, (LANES - NSA_DK, 1), 0)
    aug = jnp.where(r < 3, phi, jnp.where(r < 6, plo, jnp.where(r == 6, 1.0, 0.0))).astype(BF16)
    for g in range(NSA_KV):
        kt_ref[0, g, 0:NSA_DK, :] = _dot_nt(w2kt_ref[...], hidden(0, g)).astype(BF16)
        kt_ref[0, g, NSA_DK:LANES, :] = aug
        v_ref[0, g] = _dot(hidden(1, g), w2v_ref[...]).astype(BF16)


def _nsa_compress(kv, pe, w1, w2kt, w2v):
    bsz, _, nc, kvw = kv.shape
    return pl.pallas_call(
        _cmp_kernel,
        grid=(bsz,),
        in_specs=[pl.BlockSpec((1, CMP_STRIDE, nc, kvw), lambda b: (b, 0, 0, 0)), _const_spec(pe.shape),
                  _const_spec(w1.shape), _const_spec(w2kt.shape), _const_spec(w2v.shape)],
        out_specs=[pl.BlockSpec((1, NSA_KV, LANES, nc), lambda b: (b, 0, 0, 0)),
                   pl.BlockSpec((1, NSA_KV, nc, LANES), lambda b: (b, 0, 0, 0))],
        out_shape=[jax.ShapeDtypeStruct((bsz, NSA_KV, LANES, nc), BF16),
                   jax.ShapeDtypeStruct((bsz, NSA_KV, nc, LANES), BF16)],
        compiler_params=_cparams(("arbitrary",)),
        name="nsa_compress",
    )(kv, pe, w1, w2kt, w2v)


def _split3(x):
    hi = x.astype(BF16)
    r1 = x - hi.astype(F32)
    mid = r1.astype(BF16)
    lo = (r1 - mid.astype(F32)).astype(BF16)
    return hi, mid, lo


def _nsa_attn_kernel(qa_ref, kct_ref, vc_ref, kst_ref, vs_ref, kwt_ref, vw_ref, gt_ref, e_ref, ov_ref,
                     o_ref, tiles_ref, *, tq, tk):
    i = pl.program_id(2)
    q0 = i * tq
    rows = NSA_REP * tq
    qa = qa_ref[0]
    qs = jnp.concatenate([qa[:, r * LANES:(r + 1) * LANES] for r in range(NSA_REP)], axis=0)
    t_row = q0 + (lax.broadcasted_iota(jnp.int32, (rows, 1), 0) & (tq - 1))

    nc = kct_ref.shape[3]

    def compressed(width):
        s = _dot(qs, kct_ref[0, 0, :, 0:width])
        cend = lax.broadcasted_iota(jnp.int32, (1, width), 1) * CMP_STRIDE + (CMP_LEN - 1)
        valid = cend <= t_row
        s = jnp.where(valid, s, NEG)
        e = jnp.where(valid, jnp.exp2(s - jnp.max(s, axis=-1, keepdims=True)), 0.0)
        l = jnp.sum(e, axis=-1, keepdims=True)
        p = e / jnp.where(l > 0.0, l, 1.0)
        o = _dot(p.astype(BF16), vc_ref[0, 0, 0:width, :])
        p4 = p[0:tq]
        for r in range(1, NSA_REP):
            p4 = p4 + p[r * tq:(r + 1) * tq]
        ov = ov_ref[:, 0:width]
        return o, sum(_dot_nt(ov, part) for part in _split3(p4))

    chunk = min(CMP_CHUNK, nc)
    widths = list(range(chunk, nc + 1, chunk))
    n_ended = jnp.maximum(q0 + tq - CMP_LEN, 0) // CMP_STRIDE + 1
    o_cmp, imp = lax.switch((n_ended - 1) // chunk, [functools.partial(compressed, w) for w in widths])

    wk = WIN + tq
    st = pl.multiple_of(jnp.maximum(q0 - WIN, 0), LANES)
    col = st + lax.broadcasted_iota(jnp.int32, (1, wk), 1)
    valid = (t_row - col).astype(jnp.uint32) < WIN
    s = jnp.where(valid, _dot(qs, kwt_ref[0, :, pl.ds(st, wk)]), NEG)
    e = jnp.exp2(s - jnp.max(s, axis=-1, keepdims=True))
    o_win = _dot(e.astype(BF16), vw_ref[0, pl.ds(st, wk), :]) / jnp.sum(e, axis=-1, keepdims=True)

    cur = (q0 + lax.broadcasted_iota(jnp.int32, (1, tq), 1)) >> 6
    sid = lax.broadcasted_iota(jnp.int32, (N_SEL_PAD, 1), 0)
    forced = (sid == 0) | (sid == cur) | (sid == cur - 1)
    work = jnp.where(forced | (sid > cur), -3.0e38, imp)
    selb = jnp.where(forced, 0.0, NEG)
    for _ in range(SEL_TOPK - 3):
        mx = jnp.max(work, axis=0, keepdims=True)
        first = jnp.min(jnp.where(work == mx, sid, N_SEL_PAD), axis=0, keepdims=True)
        pick = sid == first
        selb = jnp.where(pick, 0.0, selb)
        work = jnp.where(pick, -3.0e38, work)
    selb = selb.T

    jd = q0 // tk
    n_tiles = kst_ref.shape[2] // tk
    any_sel = jnp.max(jnp.where(selb == 0.0, 1.0, 0.0), axis=0, keepdims=True)
    blk_tile = (lax.broadcasted_iota(jnp.int32, (N_SEL_PAD, LANES), 0) // (tk // SEL_LEN)
                == lax.broadcasted_iota(jnp.int32, (N_SEL_PAD, LANES), 1)).astype(BF16)
    tile_cnt = _dot(jnp.broadcast_to(any_sel, (16, N_SEL_PAD)).astype(BF16), blk_tile)
    n_act = jnp.int32(0)
    for j in range(n_tiles):
        tiles_ref[n_act] = j
        n_act = n_act + ((tile_cnt[0, j] > 0.0) & (j < jd)).astype(jnp.int32)

    selb4 = jnp.concatenate([selb.astype(BF16)] * NSA_REP, axis=0)
    qsel = jnp.concatenate([qs, selb4], axis=1)

    def sel_scores(st, width):
        return _dot(qsel, jnp.concatenate([kst_ref[0, :, pl.ds(st, width)], e_ref[:, pl.ds(st, width)]], axis=0))

    def sel_body(it, carry):
        st = pl.multiple_of(tiles_ref[it] * tk, tk)
        return _online_step(sel_scores(st, tk), *carry, vs_ref[0, pl.ds(st, tk), :])

    carry = lax.fori_loop(0, n_act, sel_body, _softmax_init(rows, LANES))
    st = pl.multiple_of(jd * tk, tk)

    def diagonal(width):
        col = st + lax.broadcasted_iota(jnp.int32, (1, width), 1)
        s = jnp.where(col <= t_row, sel_scores(st, width), NEG)
        _, l, acc = _online_step(s, *carry, vs_ref[0, pl.ds(st, width), :])
        return acc / l

    o_sel = lax.switch((q0 - st) // tq, [functools.partial(diagonal, w) for w in range(tq, tk + 1, tq)])

    gs = jax.nn.sigmoid(gt_ref[0].astype(F32))
    outs = []
    for r in range(NSA_REP):
        sl = slice(r * tq, (r + 1) * tq)
        outs.append(gs[:, 3 * r:3 * r + 1] * o_cmp[sl] + gs[:, 3 * r + 1:3 * r + 2] * o_sel[sl]
                    + gs[:, 3 * r + 2:3 * r + 3] * o_win[sl])
    pairs = [outs[2 * a] + pltpu.roll(outs[2 * a + 1], NSA_DK, axis=1) for a in range(NSA_REP // 2)]
    o_ref[0] = jnp.concatenate(pairs, axis=1).astype(o_ref.dtype)


def _nsa_attention(oa, ob, kct, vcp, e_mat, ov_mat, t):
    bsz = oa.shape[0]
    tq, tk = NSA_TQ, NSA_TK
    qw = NSA_REP * LANES
    vs_blk = NSA_HEADS
    vw_blk = vs_blk + NSA_KV
    gt_blk = vw_blk + NSA_KV
    nc = kct.shape[3]
    return pl.pallas_call(
        functools.partial(_nsa_attn_kernel, tq=tq, tk=tk),
        grid=(bsz, NSA_KV, t // tq),
        in_specs=[
            pl.BlockSpec((1, tq, qw), lambda b, g, i: (b, i, g)),
            pl.BlockSpec((1, 1, LANES, nc), lambda b, g, i: (b, g, 0, 0)),
            pl.BlockSpec((1, 1, nc, LANES), lambda b, g, i: (b, g, 0, 0)),
            pl.BlockSpec((1, LANES, t), lambda b, g, i: (b, g, 0)),
            pl.BlockSpec((1, t, LANES), lambda b, g, i: (b, 0, vs_blk + g)),
            pl.BlockSpec((1, LANES, t), lambda b, g, i: (b, NSA_KV + g, 0)),
            pl.BlockSpec((1, t, LANES), lambda b, g, i: (b, 0, vw_blk + g)),
            pl.BlockSpec((1, tq, LANES), lambda b, g, i: (b, i, gt_blk + g)),
            _const_spec(e_mat.shape),
            _const_spec(ov_mat.shape),
        ],
        out_specs=pl.BlockSpec((1, tq, NSA_REP * NSA_DK), lambda b, g, i: (b, i, g)),
        out_shape=jax.ShapeDtypeStruct((bsz, t, NSA_QW), BF16),
        scratch_shapes=[pltpu.SMEM((t // tk,), jnp.int32)],
        compiler_params=_cparams(("arbitrary", "arbitrary", "arbitrary")),
        name="nsa_attention",
    )(oa, kct, vcp, ob, oa, ob, oa, oa, e_mat, ov_mat)


def _sb_kernel(q_ref, kt_ref, v_ref, o_ref, rem_ref, acc_ref, *, tq, tk):
    i = pl.program_id(2)
    q0 = i * tq
    q = q_ref[0]
    lane = lax.broadcasted_iota(jnp.int32, (1, LANES), 1)
    zero = jnp.zeros_like(q)
    qs = jnp.concatenate([jnp.where(lane < SB_DH, q, zero), jnp.where(lane >= SB_DH, q, zero)], axis=0)
    t_row = q0 + (lax.broadcasted_iota(jnp.int32, (2 * tq, 1), 0) & (tq - 1))
    upper = (lax.broadcasted_iota(jnp.int32, (tk, tk), 0) > lax.broadcasted_iota(jnp.int32, (tk, tk), 1)
             ).astype(BF16)
    rem_ref[...] = jnp.zeros(rem_ref.shape, F32)
    acc_ref[...] = jnp.zeros(acc_ref.shape, F32)

    def tiles(js, masked):
        sts = [pl.multiple_of(j * tk, tk) for j in js]
        zs = [_dot(qs, kt_ref[0, :, pl.ds(st, tk)]) for st in sts]
        costs, log_bs, befores = [], [], []
        for st, z in zip(sts, zs):
            cost = jnp.where(z > SB_LINEAR_FROM, z, jnp.log(1.0 + jnp.exp2(z)) * LOG2E)
            log_bs.append(z - cost)
            before = None
            if masked:
                before = (st + lax.broadcasted_iota(jnp.int32, (1, tk), 1)) < t_row
                cost = jnp.where(before, cost, 0.0)
            costs.append(cost.astype(BF16))
            befores.append(before)
        rems = [_dot(c, upper) for c in costs]
        rem_right = rem_ref[...]
        probs = []
        for log_b, c, rem, before in zip(log_bs, costs, rems, befores):
            ex = log_b - (rem + rem_right)
            if masked:
                ex = jnp.where(before, ex, NEG)
            probs.append(jnp.exp2(ex).astype(BF16))
            rem_right = rem_right + (rem[:, 0:1] + c[:, 0:1].astype(F32))
        rem_ref[...] = rem_right
        acc = acc_ref[...]
        for st, p in zip(sts, probs):
            acc = acc + _dot(p, v_ref[0, pl.ds(st, tk), :])
        acc_ref[...] = acc
        return jnp.min(rem_right)

    per_step = tq // tk
    n_full = q0 // tk
    spent = tiles([n_full + per_step - 1 - d for d in range(per_step)], True)

    def go_on(c):
        return (c[0] < n_full // per_step) & (c[1] < SB_DEAD)

    def body(c):
        it = c[0]
        return it + 1, tiles([n_full - 1 - per_step * it - d for d in range(per_step)], False)

    lax.while_loop(go_on, body, (jnp.int32(0), spent))
    acc = acc_ref[...]
    o_ref[0] = jnp.where(lane < SB_DH, acc[0:tq], acc[tq:2 * tq]).astype(o_ref.dtype)


def _sb_attention(oa, ob, t):
    bsz = oa.shape[0]
    tq, tk = SB_TQ, SB_TK
    npair = SB_HEADS // 2
    return pl.pallas_call(
        functools.partial(_sb_kernel, tq=tq, tk=tk),
        grid=(bsz, npair, t // tq),
        in_specs=[
            pl.BlockSpec((1, tq, LANES), lambda b, h, i: (b, i, h)),
            pl.BlockSpec((1, LANES, t), lambda b, h, i: (b, h, 0)),
            pl.BlockSpec((1, t, LANES), lambda b, h, i: (b, 0, npair + h)),
        ],
        out_specs=pl.BlockSpec((1, tq, LANES), lambda b, h, i: (b, i, h)),
        out_shape=jax.ShapeDtypeStruct((bsz, t, D_MODEL), BF16),
        scratch_shapes=[pltpu.VMEM((2 * tq, 1), F32), pltpu.VMEM((2 * tq, LANES), F32)],
        compiler_params=_cparams(("arbitrary", "arbitrary", "arbitrary")),
        name="sb_attention",
    )(oa, ob, oa)


def _diff_kernel(slope_ref, lam_ref, g_ref, qa_ref, kt_ref, v_ref, o_ref, kmax_ref, *, tq, tk, lam_init):
    h = pl.program_id(1)
    i = pl.program_id(2)
    q0 = i * tq
    n_tiles = kt_ref.shape[2] // tk

    @pl.when(i == 0)
    def _():
        run = jnp.float32(0.0)
        for j in range(n_tiles):
            kf = kt_ref[0, 0:LANES, j * tk:(j + 1) * tk].astype(F32)
            sq = kf * kf
            norms = jnp.sqrt(jnp.maximum(jnp.sum(sq[0:DIFF_DH], axis=0, keepdims=True),
                                         jnp.sum(sq[DIFF_DH:LANES], axis=0, keepdims=True)))
            run = jnp.maximum(run, jnp.max(norms))
            kmax_ref[j] = run

    qa = qa_ref[0]
    lane = lax.broadcasted_iota(jnp.int32, (1, 2 * LANES), 1)
    zero = jnp.zeros_like(qa)
    qs = jnp.concatenate(
        [jnp.where(((lane >= m_ * DIFF_DH) & (lane < (m_ + 1) * DIFF_DH)) | (lane >= LANES), qa, zero)
         for m_ in range(2)], axis=0)
    t_row = q0 + (lax.broadcasted_iota(jnp.int32, (2 * tq, 1), 0) & (tq - 1))

    def scores(j):
        return _dot(qs, kt_ref[0, :, pl.ds(pl.multiple_of(j * tk, tk), tk)])

    def values(j):
        return v_ref[0, pl.ds(pl.multiple_of(j * tk, tk), tk), :]

    jd = q0 // tk
    col = jd * tk + lax.broadcasted_iota(jnp.int32, (1, tk), 1)
    state = _online_step(jnp.where(col <= t_row, scores(jd), NEG), *_softmax_init(2 * tq, LANES), values(jd))
    qf = jnp.where(lane < LANES, qs, jnp.zeros_like(qs)).astype(F32)
    q_norm = jnp.max(jnp.sqrt(jnp.sum(qf * qf, axis=-1, keepdims=True)))
    row_col = 2 * DIFF_DH + ALIBI_COLS - 1
    row_const = qs[:, row_col:row_col + 1].astype(F32)
    slope = slope_ref[h]

    def live(c):
        j = jnp.maximum(c[0], 0)
        bound = q_norm * kmax_ref[j] * 1.001 + slope * ((j + 1) * tk - 1).astype(F32)
        return (c[0] >= 0) & (bound - c[4] > -DIFF_DEAD)

    def body(c):
        m, l, acc = _online_step(scores(c[0]), c[1], c[2], c[3], values(c[0]))
        return c[0] - 1, m, l, acc, jnp.min(m - row_const)

    _, _, l, acc, _ = lax.while_loop(live, body, (jd - 1, *state, jnp.min(state[0] - row_const)))
    out = acc / l
    lam = lam_ref[...]
    lam_full = (jnp.exp(jnp.sum(lam[0:1] * lam[1:2], axis=-1, keepdims=True))
                - jnp.exp(jnp.sum(lam[2:3] * lam[3:4], axis=-1, keepdims=True)) + lam_init)
    o = out[0:tq] - lam_full * out[tq:2 * tq]
    o_ref[0] = (_rms(o, g_ref[...]) * (1.0 - lam_init)).astype(o_ref.dtype)


def _diff_attention(oa, ob, lam, subln_g, lam_init, t):
    bsz = oa.shape[0]
    tq, tk = DIFF_TQ, DIFF_TK
    v_blk = DIFF_HEADS * 2
    slopes = _alibi_slopes(DIFF_HEADS) * LOG2E
    return pl.pallas_call(
        functools.partial(_diff_kernel, tq=tq, tk=tk, lam_init=lam_init),
        grid=(bsz, DIFF_HEADS, t // tq),
        in_specs=[
            pl.BlockSpec(memory_space=pltpu.SMEM),
            _const_spec(lam.shape),
            _const_spec((1, 2 * DIFF_DH)),
            pl.BlockSpec((1, tq, 2 * LANES), lambda b, h, i: (b, i, h)),
            pl.BlockSpec((1, 2 * LANES, t), lambda b, h, i: (b, h, 0)),
            pl.BlockSpec((1, t, LANES), lambda b, h, i: (b, 0, v_blk + h)),
        ],
        out_specs=pl.BlockSpec((1, tq, LANES), lambda b, h, i: (b, i, h)),
        out_shape=jax.ShapeDtypeStruct((bsz, t, D_MODEL), BF16),
        scratch_shapes=[pltpu.SMEM((t // tk,), F32)],
        compiler_params=_cparams(("arbitrary", "arbitrary", "arbitrary")),
        name="diff_attention",
    )(slopes, lam, subln_g.reshape(1, 2 * DIFF_DH), oa, ob, oa)


def _alibi_slopes(n):
    return jnp.exp2(-8.0 * jnp.arange(1, n + 1, dtype=F32) / n)


def _alibi_query_cols(slopes, first):
    sl2 = slopes * LOG2E
    parts = jnp.stack([p.astype(F32) for p in _split3(sl2)], axis=1)
    b0 = jnp.pad(jnp.concatenate([parts, parts], axis=1), ((0, 0), (first, LANES - first - 6)))
    pv = jnp.pad(-sl2[:, None], ((0, 0), (first + 6, LANES - first - ALIBI_COLS)))
    return b0, pv


def _alibi_key_rows():
    c = np.zeros((LANES, 8), np.float32)
    c[0:3, 0] = 1.0
    c[3:6, 1] = 1.0
    c[6, 2] = 1.0
    return jnp.asarray(c)


def _nsa_weights(w_in):
    bounds = [0, NSA_QW] + [NSA_QW + (k + 1) * NSA_KVW for k in range(6)]
    wq, wkc, wvc, wks, wvs, wkw, wvw = [w_in[:, bounds[k]:bounds[k + 1]] for k in range(7)]
    wg = jnp.pad(w_in[:, bounds[7]:], ((0, 0), (0, LANES - NSA_KV * GATE_COLS)))
    wa = jnp.concatenate([wq, wkc, wvc, wvs, wvw, wg], axis=1).astype(BF16)
    scale = NSA_DK ** -0.5 * LOG2E
    per = ROW_TILE // LANES
    qblocks = NSA_QW // LANES
    a_chunks = []
    for c in range(qblocks // per):
        ents = tuple(("half", b, 2 * (c * per + b) + hi, scale, hi, 2 * (c * per + b) + hi)
                     for b in range(per) for hi in range(2))
        a_chunks.append((c * ROW_TILE, ROW_TILE, "mm", ents))
    a_chunks.append((NSA_QW, 2 * NSA_KVW, "kv", ()))
    v0 = NSA_QW + 2 * NSA_KVW
    vs_blk, vw_blk, gt_blk = NSA_HEADS, NSA_HEADS + NSA_KV, NSA_HEADS + 2 * NSA_KV
    ents = tuple(("half", g // 2, vs_blk + g, 1.0, g % 2, -1) for g in range(NSA_KV)) + tuple(
        ("half", NSA_KV // 2 + g // 2, vw_blk + g, 1.0, g % 2, -1) for g in range(NSA_KV))
    a_chunks.append((v0, 2 * NSA_KVW, "mm", ents))
    a_chunks.append((v0 + 2 * NSA_KVW, LANES, "mm",
                     tuple(("gate", 0, gt_blk + g, GATE_COLS * g) for g in range(NSA_KV))))
    b0, pv = _alibi_query_cols(_alibi_slopes(NSA_HEADS), NSA_DK)
    wb = jnp.concatenate([wks, wkw], axis=1).T.astype(BF16)
    pieces = tuple((NSA_DK * g, NSA_DK, LANES * g) for g in range(2 * NSA_KV))
    b_aug = tuple((LANES * g + NSA_DK, LANES - NSA_DK) for g in range(2 * NSA_KV))
    plan = (tuple(a_chunks), (), gt_blk + NSA_KV, ((0, 2 * NSA_KVW, pieces),), b_aug, 2 * NSA_KV * LANES, True)
    return wa, b0, pv, wb, _alibi_key_rows(), plan


def _sb_weights(w_in):
    d = w_in.shape[0]
    wq, wk, wv = jnp.split(w_in, 3, axis=1)
    wa = jnp.concatenate([wq, wv], axis=1).astype(BF16)
    per = ROW_TILE // LANES
    nq = d // ROW_TILE
    a_chunks = tuple((c * ROW_TILE, ROW_TILE, "mm",
                      tuple(("mm", b, c * per + b, SB_DH ** -0.5 * LOG2E if c < nq else 1.0) for b in range(per)))
                     for c in range(2 * nq))
    b_chunks = tuple((c * ROW_TILE, ROW_TILE, ((0, ROW_TILE, c * ROW_TILE),)) for c in range(nq))
    zero = jnp.asarray(np.zeros((1, LANES), np.float32))
    plan = (a_chunks, (), 2 * d // LANES, b_chunks, (), d, False)
    return wa, zero, zero, wk.T.astype(BF16), jnp.asarray(np.zeros((LANES, 8), np.float32)), plan


def _diff_weights(w_in):
    d = w_in.shape[0]
    wq, wk, wv = jnp.split(w_in, 3, axis=1)
    wa = jnp.concatenate([wq, wv], axis=1).astype(BF16)
    scale = DIFF_DH ** -0.5 * LOG2E
    per = ROW_TILE // LANES
    nq = d // ROW_TILE
    a_chunks = tuple(
        (c * ROW_TILE, ROW_TILE, "mm",
         tuple(("mm", b, 2 * (c * per + b), scale) if c < nq else ("mm", b, 2 * DIFF_HEADS + (c - nq) * per + b, 1.0)
               for b in range(per)))
        for c in range(2 * nq))
    aug_blocks = tuple((2 * h + 1, h) for h in range(DIFF_HEADS))
    b0, pv = _alibi_query_cols(_alibi_slopes(DIFF_HEADS), 0)
    b_chunks = tuple((c * ROW_TILE, ROW_TILE,
                      tuple((LANES * k, LANES, 2 * LANES * (c * per + k)) for k in range(per)))
                     for c in range(nq))
    b_aug = tuple((2 * LANES * h + LANES, LANES) for h in range(DIFF_HEADS))
    plan = (a_chunks, aug_blocks, 3 * DIFF_HEADS, b_chunks, b_aug, 2 * DIFF_HEADS * LANES, False)
    return wa, b0, pv, wk.T.astype(BF16), _alibi_key_rows(), plan


def _nsa_mixer(x, g, mod, scl, shift, w_in, cmp_pe, cmp_w1, cmp_w2):
    t = x.shape[1]
    wa, b0, pv, wb, cb, plan = _nsa_weights(w_in)
    oa, ob, kv = _inproj(x, g, mod, scl, shift, wa, b0, pv, wb, cb, plan)
    nc = t // CMP_STRIDE
    w2kt = cmp_w2[0].T.astype(BF16)
    w2v = jnp.pad(cmp_w2[1], ((0, 0), (0, LANES - NSA_DK))).astype(BF16)
    kct, vcp = _nsa_compress(kv, cmp_pe, cmp_w1.astype(BF16), w2kt, w2v)
    tok = np.arange(t)
    e_mat = jnp.asarray(np.arange(N_SEL_PAD)[:, None] == (tok // SEL_LEN)[None, :], BF16)
    cstart = np.arange(nc) * CMP_STRIDE
    sstart = np.arange(N_SEL_PAD) * SEL_LEN
    ov_mat = jnp.asarray((cstart[None, :] < sstart[:, None] + SEL_LEN)
                         & (cstart[None, :] + CMP_LEN > sstart[:, None])
                         & (np.arange(nc)[None, :] < nc - 1), BF16)
    return _nsa_attention(oa, ob, kct, vcp, e_mat, ov_mat, t)


def _sb_mixer(x, g, mod, scl, shift, w_in):
    t = x.shape[1]
    wa, b0, pv, wb, cb, plan = _sb_weights(w_in)
    oa, ob = _inproj(x, g, mod, scl, shift, wa, b0, pv, wb, cb, plan)
    return _sb_attention(oa, ob, t)


def _diff_mixer(x, g, mod, scl, shift, w_in, lam, subln_g, layer_idx):
    t = x.shape[1]
    wa, b0, pv, wb, cb, plan = _diff_weights(w_in)
    oa, ob = _inproj(x, g, mod, scl, shift, wa, b0, pv, wb, cb, plan)
    lam_init = 0.8 - 0.6 * math.exp(-0.3 * layer_idx)
    return _diff_attention(oa, ob, lam, subln_g, lam_init, t)


def kernel(x, c, ada_w, ada_b, norm_g, ffn_w1, ffn_w2, nsa_w_in, nsa_cmp_pe, nsa_cmp_w1, nsa_cmp_w2,
           nsa_w_out, sb_w_in, sb_w_out, diff_w_in, diff_lam, diff_subln_g, diff_w_out):
    bsz, t, d = x.shape
    assert d == D_MODEL and t % ROW_TILE == 0 and t >= WIN + 128 and t <= N_SEL_PAD * SEL_LEN
    mod = _modulation(c, ada_w, ada_b)
    for i in range(DEPTH):
        def mods(sidx):
            return tuple((i, 3 * sidx + k) for k in range(3))

        shift, scl, gate = mods(0)
        x = _ffn(x, norm_g[i, 0], mod, scl, shift, gate, norm_g[i, 1],
                 ffn_w1[i, 0].astype(BF16), ffn_w2[i, 0].astype(BF16))
        shift, scl, gate = mods(1)
        kind, j = i % N_MIXERS, i // N_MIXERS
        if kind == 0:
            o = _nsa_mixer(x, norm_g[i, 2], mod, scl, shift, nsa_w_in[j], nsa_cmp_pe[j], nsa_cmp_w1[j],
                           nsa_cmp_w2[j])
            w_out = nsa_w_out[j]
        elif kind == 1:
            o = _sb_mixer(x, norm_g[i, 2], mod, scl, shift, sb_w_in[j])
            w_out = sb_w_out[j]
        else:
            o = _diff_mixer(x, norm_g[i, 2], mod, scl, shift, diff_w_in[j], diff_lam[j], diff_subln_g[j], i)
            w_out = diff_w_out[j]
        mixer = (o, w_out.astype(BF16), gate, norm_g[i, 3])
        shift, scl, gate = mods(2)
        x = _ffn(x, norm_g[i, 4], mod, scl, shift, gate, norm_g[i, 5],
                 ffn_w1[i, 1].astype(BF16), ffn_w2[i, 1].astype(BF16), mixer=mixer)
    return x
```

```python
import functools
import math

import jax
import jax.numpy as jnp
import numpy as np
from jax import lax
from jax.experimental import pallas as pl
from jax.experimental.pallas import tpu as pltpu

F32 = jnp.float32
BF16 = jnp.bfloat16

D_MODEL = 1024
DEPTH = 4
N_MIXERS = 3
EPS = 1e-6
NEG = -1e30
FORCE = 1e4
LOG2E = 1.4426950408889634
NSA_HEADS = 16
NSA_DK = 64
NSA_KV = 4
NSA_REP = NSA_HEADS // NSA_KV
CMP_LEN = 32
CMP_STRIDE = 16
CMP_HID = 256
SEL_LEN = 64
SEL_TOPK = 16
WIN = 512
NSA_QW = NSA_HEADS * NSA_DK
NSA_KVW = NSA_KV * NSA_DK
N_SEL_PAD = 128
SB_HEADS = 16
SB_DH = D_MODEL // SB_HEADS
DIFF_HEADS = 8
DIFF_DH = D_MODEL // (2 * DIFF_HEADS)
D_FF = 2816

LANES = 128
MOD_ROWS = 8
ROW_TILE = 512
FF_CHUNK = 256
DIFF_TQ, DIFF_TK = 512, 512
SB_TQ, SB_TK = 512, 256
SB_LINEAR_FROM = 64.0
SB_DEAD = 192.0
DIFF_DEAD = 160.0
ALIBI_COLS = 7
GATE_COLS = NSA_REP * 3
NSA_TQ, NSA_TK = 256, 512
CMP_CHUNK = 128
VMEM_LIMIT = 56 * 1024 * 1024


def _cparams(sem):
    return pltpu.CompilerParams(dimension_semantics=sem, vmem_limit_bytes=VMEM_LIMIT)


def _const_spec(shape):
    n = len(shape)
    return pl.BlockSpec(shape, lambda *_: (0,) * n)


def _rms(x, g):
    return x * lax.rsqrt(jnp.mean(x * x, axis=-1, keepdims=True) + EPS) * g


def _dot(a, b):
    return jnp.dot(a, b, preferred_element_type=F32)


def _dot_nt(a, b):
    return lax.dot_general(a, b, (((1,), (1,)), ((), ())), preferred_element_type=F32)


def _mod_kernel(c_ref, w_ref, b_ref, o_ref):
    c = c_ref[...]
    cond = c * jax.nn.sigmoid(c)
    o_ref[0] = _dot(cond.astype(BF16), w_ref[0].astype(BF16)) + b_ref[0]


def _modulation(c, ada_w, ada_b):
    depth, d, n = ada_w.shape
    b = c.shape[0]
    rows = MOD_ROWS
    c_pad = jnp.zeros((rows, d), F32).at[:b].set(c)
    nc = 1152
    out = pl.pallas_call(
        _mod_kernel,
        grid=(depth, n // nc),
        in_specs=[
            pl.BlockSpec((rows, d), lambda i, j: (0, 0)),
            pl.BlockSpec((1, d, nc), lambda i, j: (i, 0, j)),
            pl.BlockSpec((1, 1, nc), lambda i, j: (i, 0, j)),
        ],
        out_specs=pl.BlockSpec((1, rows, nc), lambda i, j: (i, 0, j)),
        out_shape=jax.ShapeDtypeStruct((depth, rows, n), F32),
        compiler_params=_cparams(("arbitrary", "arbitrary")),
        name="adaln_mod",
    )(c_pad, ada_w, ada_b.reshape(depth, 1, n))
    return out


def _mod_spec(sel):
    layer, blk = sel
    return pl.BlockSpec((1, MOD_ROWS, D_MODEL), lambda b, i: (layer, 0, blk))


def _mod_row(ref):
    return ref[0, pl.ds(pl.program_id(0), 1), :]


def _inproj_kernel(*refs, tt, a_chunks, aug_blocks, b_chunks, b_aug, has_kv):
    if has_kv:
        (x_ref, g_ref, scl_ref, sh_ref, wa_ref, b0_ref, pv_ref, wb_ref, cb_ref,
         oa_ref, ob_ref, oc_ref, kv_ref) = refs
    else:
        x_ref, g_ref, scl_ref, sh_ref, wa_ref, b0_ref, pv_ref, wb_ref, cb_ref, oa_ref, ob_ref = refs
    x = x_ref[0]
    hh = (_rms(x, g_ref[...]) * (1.0 + _mod_row(scl_ref)) + _mod_row(sh_ref)).astype(BF16)
    t0 = pl.program_id(1) * tt
    pos = (t0 + lax.broadcasted_iota(jnp.int32, (tt, 1), 0)).astype(F32)
    lane = lax.broadcasted_iota(jnp.int32, (1, LANES), 1)

    def aug(k):
        return b0_ref[k:k + 1, :] + pos * pv_ref[k:k + 1, :]

    def put(dst, val):
        oa_ref[0, :, dst * LANES:(dst + 1) * LANES] = val.astype(oa_ref.dtype)

    for lo, width, kind, entries in a_chunks:
        acc = _dot(hh, wa_ref[:, lo:lo + width])
        if kind == "kv":
            for k in range(width // LANES):
                kv_ref[k] = acc[:, k * LANES:(k + 1) * LANES]
                for l in range(CMP_STRIDE):
                    oc_ref[0, l, :, k * LANES:(k + 1) * LANES] = kv_ref[
                        k, pl.ds(l, tt // CMP_STRIDE, stride=CMP_STRIDE), :].astype(oc_ref.dtype)
            continue
        for e in entries:
            blk = acc[:, e[1] * LANES:(e[1] + 1) * LANES]
            if e[0] == "mm":
                put(e[2], blk * e[3] if e[3] != 1.0 else blk)
            elif e[0] == "half":
                if e[3] != 1.0:
                    blk = blk * e[3]
                if e[4]:
                    blk = pltpu.roll(blk, NSA_DK, axis=1)
                put(e[2], jnp.where(lane < NSA_DK, blk, aug(e[5]) if e[5] >= 0 else 0.0))
            else:
                if e[3]:
                    blk = pltpu.roll(blk, LANES - e[3], axis=1)
                put(e[2], jnp.where(lane < GATE_COLS, blk, 0.0))
    for dst, k in aug_blocks:
        put(dst, aug(k))

    tpos = t0 + lax.broadcasted_iota(jnp.int32, (1, tt), 1)
    phi = ((tpos >> 7) << 7).astype(F32)
    plo = (tpos & 127).astype(F32)
    if b_aug:
        cc = cb_ref[...]
        key_rows = (cc[:, 0:1] * phi + cc[:, 1:2] * plo + cc[:, 2:3]).astype(ob_ref.dtype)
        for dst, nrows in b_aug:
            ob_ref[0, dst:dst + nrows, :] = key_rows[0:nrows]
    for lo, rows, pieces in b_chunks:
        acc = _dot_nt(wb_ref[lo:lo + rows, :], hh)
        for src, nrows, dst in pieces:
            ob_ref[0, dst:dst + nrows, :] = acc[src:src + nrows].astype(ob_ref.dtype)


def _inproj(x, g, mod, scl, shift, wa, b0, pv, wb, cb, plan):
    bsz, t, d = x.shape
    tt = min(ROW_TILE, t)
    a_chunks, aug_blocks, n_blocks, b_chunks, b_aug, n_rows, has_kv = plan
    na = n_blocks * LANES
    in_specs = [
        pl.BlockSpec((1, tt, d), lambda b, i: (b, i, 0)),
        _const_spec((1, d)),
        _mod_spec(scl),
        _mod_spec(shift),
        _const_spec(wa.shape),
        _const_spec(b0.shape),
        _const_spec(pv.shape),
        _const_spec(wb.shape),
        _const_spec(cb.shape),
    ]
    out_specs = [pl.BlockSpec((1, tt, na), lambda b, i: (b, i, 0)),
                 pl.BlockSpec((1, n_rows, tt), lambda b, i: (b, 0, i))]
    out_shape = [jax.ShapeDtypeStruct((bsz, t, na), BF16), jax.ShapeDtypeStruct((bsz, n_rows, t), BF16)]
    scratch = []
    if has_kv:
        kvw = 2 * NSA_KVW
        out_specs.append(pl.BlockSpec((1, CMP_STRIDE, tt // CMP_STRIDE, kvw), lambda b, i: (b, 0, i, 0)))
        out_shape.append(jax.ShapeDtypeStruct((bsz, CMP_STRIDE, t // CMP_STRIDE, kvw), BF16))
        scratch.append(pltpu.VMEM((kvw // LANES, tt, LANES), F32))
    return pl.pallas_call(
        functools.partial(_inproj_kernel, tt=tt, a_chunks=a_chunks, aug_blocks=aug_blocks, b_chunks=b_chunks,
                          b_aug=b_aug, has_kv=has_kv),
        grid=(bsz, t // tt),
        in_specs=in_specs,
        out_specs=out_specs,
        out_shape=out_shape,
        scratch_shapes=scratch,
        compiler_params=_cparams(("arbitrary", "arbitrary")),
        name="mixer_inproj",
    )(x, g.reshape(1, d), mod, mod, wa, b0, pv, wb, cb)


def _ffn_kernel(*refs, after_mixer):
    if after_mixer:
        mo_ref, wo_ref, mgate_ref, gm_ref = refs[:4]
        refs = refs[4:]
    x_ref, g1_ref, scl_ref, sh_ref, gate_ref, g2_ref, w1_ref, w2_ref, o_ref, acc_ref = refs
    x = x_ref[0]
    if after_mixer:
        x = x + _mod_row(mgate_ref) * _rms(_dot(mo_ref[0], wo_ref[...]), gm_ref[...])
    hh = (_rms(x, g1_ref[...]) * (1.0 + _mod_row(scl_ref)) + _mod_row(sh_ref)).astype(BF16)
    dff = w2_ref.shape[0]
    for c in range(dff // FF_CHUNK):
        lo = c * FF_CHUNK
        gt = _dot(hh, w1_ref[:, lo:lo + FF_CHUNK])
        up = _dot(hh, w1_ref[:, dff + lo:dff + lo + FF_CHUNK])
        act = (gt * jax.nn.sigmoid(gt) * up).astype(BF16)
        part = _dot(act, w2_ref[lo:lo + FF_CHUNK, :])
        if c == 0:
            acc_ref[...] = part
        else:
            acc_ref[...] += part
    o_ref[0] = x + 0.5 * _mod_row(gate_ref) * _rms(acc_ref[...], g2_ref[...])


def _ffn(x, g1, mod, scl, shift, gate, g2, w1, w2, mixer=None):
    bsz, t, d = x.shape
    tt = min(ROW_TILE, t)
    in_specs = [
        pl.BlockSpec((1, tt, d), lambda b, i: (b, i, 0)),
        _const_spec((1, d)), _mod_spec(scl), _mod_spec(shift), _mod_spec(gate), _const_spec((1, d)),
        _const_spec(w1.shape), _const_spec(w2.shape),
    ]
    args = [x, g1.reshape(1, d), mod, mod, mod, g2.reshape(1, d), w1, w2]
    if mixer is not None:
        mo, wo, mgate, gm = mixer
        in_specs = [pl.BlockSpec((1, tt, mo.shape[2]), lambda b, i: (b, i, 0)), _const_spec(wo.shape),
                    _mod_spec(mgate), _const_spec((1, d))] + in_specs
        args = [mo, wo, mod, gm.reshape(1, d)] + args
    return pl.pallas_call(
        functools.partial(_ffn_kernel, after_mixer=mixer is not None),
        grid=(bsz, t // tt),
        in_specs=in_specs,
        out_specs=pl.BlockSpec((1, tt, d), lambda b, i: (b, i, 0)),
        out_shape=jax.ShapeDtypeStruct((bsz, t, d), F32),
        scratch_shapes=[pltpu.VMEM((tt, d), F32)],
        compiler_params=_cparams(("arbitrary", "arbitrary")),
        name="ffn",
    )(*args)


def _online_step(s, m, l, acc, v):
    m_new = jnp.maximum(m, jnp.max(s, axis=-1, keepdims=True))
    alpha = jnp.exp2(m - m_new)
    p = jnp.exp2(s - m_new)
    l = alpha * l + jnp.sum(p, axis=-1, keepdims=True)
    acc = alpha * acc + _dot(p.astype(BF16), v)
    return m_new, l, acc


def _softmax_init(rows, width):
    return (jnp.full((rows, 1), NEG, F32), jnp.zeros((rows, 1), F32), jnp.zeros((rows, width), F32))


def _gelu_tanh(x):
    return 0.5 * x * (1.0 + jnp.tanh(math.sqrt(2.0 / math.pi) * (x + 0.044715 * (x * x * x))))


def _cmp_kernel(kv_ref, pe_ref, w1_ref, w2kt_ref, w2v_ref, kt_ref, v_ref):
    nc = kv_ref.shape[2]

    def hidden(j, g):
        lo = j * NSA_KVW + g * NSA_DK
        top = bot = None
        for l in range(CMP_STRIDE):
            x = kv_ref[0, l, :, lo:lo + NSA_DK].astype(F32)
            hi = CMP_STRIDE + l
            a = _dot((x + pe_ref[j, l:l + 1, :]).astype(BF16), w1_ref[j, l * NSA_DK:(l + 1) * NSA_DK, :])
            b = _dot((x + pe_ref[j, hi:hi + 1, :]).astype(BF16), w1_ref[j, hi * NSA_DK:(hi + 1) * NSA_DK, :])
            top = a if top is None else top + a
            bot = b if bot is None else bot + b
        return _gelu_tanh(top + pltpu.roll(bot, nc - 1, axis=0)).astype(BF16)

    n = lax.broadcasted_iota(jnp.int32, (1, nc), 1)
    end = n * CMP_STRIDE + (CMP_LEN - 1)
    phi = ((end >> 7) << 7).astype(F32)
    plo = (end & 127).astype(F32)
    r = lax.broadcasted_iota(jnp.int32, (LANES - NSA_DK, 1), 0)
    aug = jnp.where(r < 3, phi, jnp.where(r < 6, plo, jnp.where(r == 6, 1.0, 0.0))).astype(BF16)
    for g in range(NSA_KV):
        kt_ref[0, g, 0:NSA_DK, :] = _dot_nt(w2kt_ref[...], hidden(0, g)).astype(BF16)
        kt_ref[0, g, NSA_DK:LANES, :] = aug
        v_ref[0, g] = _dot(hidden(1, g), w2v_ref[...]).astype(BF16)


def _nsa_compress(kv, pe, w1, w2kt, w2v):
    bsz, _, nc, kvw = kv.shape
    return pl.pallas_call(
        _cmp_kernel,
        grid=(bsz,),
        in_specs=[pl.BlockSpec((1, CMP_STRIDE, nc, kvw), lambda b: (b, 0, 0, 0)), _const_spec(pe.shape),
                  _const_spec(w1.shape), _const_spec(w2kt.shape), _const_spec(w2v.shape)],
        out_specs=[pl.BlockSpec((1, NSA_KV, LANES, nc), lambda b: (b, 0, 0, 0)),
                   pl.BlockSpec((1, NSA_KV, nc, LANES), lambda b: (b, 0, 0, 0))],
        out_shape=[jax.ShapeDtypeStruct((bsz, NSA_KV, LANES, nc), BF16),
                   jax.ShapeDtypeStruct((bsz, NSA_KV, nc, LANES), BF16)],
        compiler_params=_cparams(("arbitrary",)),
        name="nsa_compress",
    )(kv, pe, w1, w2kt, w2v)


def _split3(x):
    hi = x.astype(BF16)
    r1 = x - hi.astype(F32)
    mid = r1.astype(BF16)
    lo = (r1 - mid.astype(F32)).astype(BF16)
    return hi, mid, lo


def _nsa_attn_kernel(qa_ref, kct_ref, vc_ref, kst_ref, vs_ref, kwt_ref, vw_ref, gt_ref, e_ref, ov_ref,
                     o_ref, tiles_ref, *, tq, tk):
    i = pl.program_id(2)
    q0 = i * tq
    rows = NSA_REP * tq
    qa = qa_ref[0]
    qs = jnp.concatenate([qa[:, r * LANES:(r + 1) * LANES] for r in range(NSA_REP)], axis=0)
    t_row = q0 + (lax.broadcasted_iota(jnp.int32, (rows, 1), 0) & (tq - 1))

    nc = kct_ref.shape[3]

    def compressed(width):
        s = _dot(qs, kct_ref[0, 0, :, 0:width])
        cend = lax.broadcasted_iota(jnp.int32, (1, width), 1) * CMP_STRIDE + (CMP_LEN - 1)
        valid = cend <= t_row
        s = jnp.where(valid, s, NEG)
        e = jnp.where(valid, jnp.exp2(s - jnp.max(s, axis=-1, keepdims=True)), 0.0)
        l = jnp.sum(e, axis=-1, keepdims=True)
        p = e / jnp.where(l > 0.0, l, 1.0)
        o = _dot(p.astype(BF16), vc_ref[0, 0, 0:width, :])
        p4 = p[0:tq]
        for r in range(1, NSA_REP):
            p4 = p4 + p[r * tq:(r + 1) * tq]
        ov = ov_ref[:, 0:width]
        return o, sum(_dot_nt(ov, part) for part in _split3(p4))

    chunk = min(CMP_CHUNK, nc)
    widths = list(range(chunk, nc + 1, chunk))
    n_ended = jnp.maximum(q0 + tq - CMP_LEN, 0) // CMP_STRIDE + 1
    o_cmp, imp = lax.switch((n_ended - 1) // chunk, [functools.partial(compressed, w) for w in widths])

    wk = WIN + tq
    st = pl.multiple_of(jnp.maximum(q0 - WIN, 0), LANES)
    col = st + lax.broadcasted_iota(jnp.int32, (1, wk), 1)
    valid = (t_row - col).astype(jnp.uint32) < WIN
    s = jnp.where(valid, _dot(qs, kwt_ref[0, :, pl.ds(st, wk)]), NEG)
    e = jnp.exp2(s - jnp.max(s, axis=-1, keepdims=True))
    o_win = _dot(e.astype(BF16), vw_ref[0, pl.ds(st, wk), :]) / jnp.sum(e, axis=-1, keepdims=True)

    cur = (q0 + lax.broadcasted_iota(jnp.int32, (1, tq), 1)) >> 6
    sid = lax.broadcasted_iota(jnp.int32, (N_SEL_PAD, 1), 0)
    forced = (sid == 0) | (sid == cur) | (sid == cur - 1)
    work = jnp.where(forced | (sid > cur), -3.0e38, imp)
    selb = jnp.where(forced, 0.0, NEG)
    for _ in range(SEL_TOPK - 3):
        mx = jnp.max(work, axis=0, keepdims=True)
        first = jnp.min(jnp.where(work == mx, sid, N_SEL_PAD), axis=0, keepdims=True)
        pick = sid == first
        selb = jnp.where(pick, 0.0, selb)
        work = jnp.where(pick, -3.0e38, work)
    selb = selb.T

    jd = q0 // tk
    n_tiles = kst_ref.shape[2] // tk
    any_sel = jnp.max(jnp.where(selb == 0.0, 1.0, 0.0), axis=0, keepdims=True)
    blk_tile = (lax.broadcasted_iota(jnp.int32, (N_SEL_PAD, LANES), 0) // (tk // SEL_LEN)
                == lax.broadcasted_iota(jnp.int32, (N_SEL_PAD, LANES), 1)).astype(BF16)
    tile_cnt = _dot(jnp.broadcast_to(any_sel, (16, N_SEL_PAD)).astype(BF16), blk_tile)
    n_act = jnp.int32(0)
    for j in range(n_tiles):
        tiles_ref[n_act] = j
        n_act = n_act + ((tile_cnt[0, j] > 0.0) & (j < jd)).astype(jnp.int32)

    selb4 = jnp.concatenate([selb.astype(BF16)] * NSA_REP, axis=0)
    qsel = jnp.concatenate([qs, selb4], axis=1)

    def sel_scores(st, width):
        return _dot(qsel, jnp.concatenate([kst_ref[0, :, pl.ds(st, width)], e_ref[:, pl.ds(st, width)]], axis=0))

    def sel_body(it, carry):
        st = pl.multiple_of(tiles_ref[it] * tk, tk)
        return _online_step(sel_scores(st, tk), *carry, vs_ref[0, pl.ds(st, tk), :])

    carry = lax.fori_loop(0, n_act, sel_body, _softmax_init(rows, LANES))
    st = pl.multiple_of(jd * tk, tk)

    def diagonal(width):
        col = st + lax.broadcasted_iota(jnp.int32, (1, width), 1)
        s = jnp.where(col <= t_row, sel_scores(st, width), NEG)
        _, l, acc = _online_step(s, *carry, vs_ref[0, pl.ds(st, width), :])
        return acc / l

    o_sel = lax.switch((q0 - st) // tq, [functools.partial(diagonal, w) for w in range(tq, tk + 1, tq)])

    gs = jax.nn.sigmoid(gt_ref[0].astype(F32))
    outs = []
    for r in range(NSA_REP):
        sl = slice(r * tq, (r + 1) * tq)
        outs.append(gs[:, 3 * r:3 * r + 1] * o_cmp[sl] + gs[:, 3 * r + 1:3 * r + 2] * o_sel[sl]
                    + gs[:, 3 * r + 2:3 * r + 3] * o_win[sl])
    pairs = [outs[2 * a] + pltpu.roll(outs[2 * a + 1], NSA_DK, axis=1) for a in range(NSA_REP // 2)]
    o_ref[0] = jnp.concatenate(pairs, axis=1).astype(o_ref.dtype)


def _nsa_attention(oa, ob, kct, vcp, e_mat, ov_mat, t):
    bsz = oa.shape[0]
    tq, tk = NSA_TQ, NSA_TK
    qw = NSA_REP * LANES
    vs_blk = NSA_HEADS
    vw_blk = vs_blk + NSA_KV
    gt_blk = vw_blk + NSA_KV
    nc = kct.shape[3]
    return pl.pallas_call(
        functools.partial(_nsa_attn_kernel, tq=tq, tk=tk),
        grid=(bsz, NSA_KV, t // tq),
        in_specs=[
            pl.BlockSpec((1, tq, qw), lambda b, g, i: (b, i, g)),
            pl.BlockSpec((1, 1, LANES, nc), lambda b, g, i: (b, g, 0, 0)),
            pl.BlockSpec((1, 1, nc, LANES), lambda b, g, i: (b, g, 0, 0)),
            pl.BlockSpec((1, LANES, t), lambda b, g, i: (b, g, 0)),
            pl.BlockSpec((1, t, LANES), lambda b, g, i: (b, 0, vs_blk + g)),
            pl.BlockSpec((1, LANES, t), lambda b, g, i: (b, NSA_KV + g, 0)),
            pl.BlockSpec((1, t, LANES), lambda b, g, i: (b, 0, vw_blk + g)),
            pl.BlockSpec((1, tq, LANES), lambda b, g, i: (b, i, gt_blk + g)),
            _const_spec(e_mat.shape),
            _const_spec(ov_mat.shape),
        ],
        out_specs=pl.BlockSpec((1, tq, NSA_REP * NSA_DK), lambda b, g, i: (b, i, g)),
        out_shape=jax.ShapeDtypeStruct((bsz, t, NSA_QW), BF16),
        scratch_shapes=[pltpu.SMEM((t // tk,), jnp.int32)],
        compiler_params=_cparams(("arbitrary", "arbitrary", "arbitrary")),
        name="nsa_attention",
    )(oa, kct, vcp, ob, oa, ob, oa, oa, e_mat, ov_mat)


def _sb_kernel(q_ref, kt_ref, v_ref, o_ref, rem_ref, acc_ref, *, tq, tk):
    i = pl.program_id(2)
    q0 = i * tq
    q = q_ref[0]
    lane = lax.broadcasted_iota(jnp.int32, (1, LANES), 1)
    zero = jnp.zeros_like(q)
    qs = jnp.concatenate([jnp.where(lane < SB_DH, q, zero), jnp.where(lane >= SB_DH, q, zero)], axis=0)
    t_row = q0 + (lax.broadcasted_iota(jnp.int32, (2 * tq, 1), 0) & (tq - 1))
    upper = (lax.broadcasted_iota(jnp.int32, (tk, tk), 0) > lax.broadcasted_iota(jnp.int32, (tk, tk), 1)
             ).astype(BF16)
    rem_ref[...] = jnp.zeros(rem_ref.shape, F32)
    acc_ref[...] = jnp.zeros(acc_ref.shape, F32)

    def tiles(js, masked):
        sts = [pl.multiple_of(j * tk, tk) for j in js]
        zs = [_dot(qs, kt_ref[0, :, pl.ds(st, tk)]) for st in sts]
        costs, log_bs, befores = [], [], []
        for st, z in zip(sts, zs):
            cost = jnp.where(z > SB_LINEAR_FROM, z, jnp.log(1.0 + jnp.exp2(z)) * LOG2E)
            log_bs.append(z - cost)
            before = None
            if masked:
                before = (st + lax.broadcasted_iota(jnp.int32, (1, tk), 1)) < t_row
                cost = jnp.where(before, cost, 0.0)
            costs.append(cost.astype(BF16))
            befores.append(before)
        rems = [_dot(c, upper) for c in costs]
        rem_right = rem_ref[...]
        probs = []
        for log_b, c, rem, before in zip(log_bs, costs, rems, befores):
            ex = log_b - (rem + rem_right)
            if masked:
                ex = jnp.where(before, ex, NEG)
            probs.append(jnp.exp2(ex).astype(BF16))
            rem_right = rem_right + (rem[:, 0:1] + c[:, 0:1].astype(F32))
        rem_ref[...] = rem_right
        acc = acc_ref[...]
        for st, p in zip(sts, probs):
            acc = acc + _dot(p, v_ref[0, pl.ds(st, tk), :])
        acc_ref[...] = acc
        return jnp.min(rem_right)

    per_step = tq // tk
    n_full = q0 // tk
    spent = tiles([n_full + per_step - 1 - d for d in range(per_step)], True)

    def go_on(c):
        return (c[0] < n_full // per_step) & (c[1] < SB_DEAD)

    def body(c):
        it = c[0]
        return it + 1, tiles([n_full - 1 - per_step * it - d for d in range(per_step)], False)

    lax.while_loop(go_on, body, (jnp.int32(0), spent))
    acc = acc_ref[...]
    o_ref[0] = jnp.where(lane < SB_DH, acc[0:tq], acc[tq:2 * tq]).astype(o_ref.dtype)


def _sb_attention(oa, ob, t):
    bsz = oa.shape[0]
    tq, tk = SB_TQ, SB_TK
    npair = SB_HEADS // 2
    return pl.pallas_call(
        functools.partial(_sb_kernel, tq=tq, tk=tk),
        grid=(bsz, npair, t // tq),
        in_specs=[
            pl.BlockSpec((1, tq, LANES), lambda b, h, i: (b, i, h)),
            pl.BlockSpec((1, LANES, t), lambda b, h, i: (b, h, 0)),
            pl.BlockSpec((1, t, LANES), lambda b, h, i: (b, 0, npair + h)),
        ],
        out_specs=pl.BlockSpec((1, tq, LANES), lambda b, h, i: (b, i, h)),
        out_shape=jax.ShapeDtypeStruct((bsz, t, D_MODEL), BF16),
        scratch_shapes=[pltpu.VMEM((2 * tq, 1), F32), pltpu.VMEM((2 * tq, LANES), F32)],
        compiler_params=_cparams(("arbitrary", "arbitrary", "arbitrary")),
        name="sb_attention",
    )(oa, ob, oa)


def _diff_kernel(slope_ref, lam_ref, g_ref, qa_ref, kt_ref, v_ref, o_ref, kmax_ref, *, tq, tk, lam_init):
    h = pl.program_id(1)
    i = pl.program_id(2)
    q0 = i * tq
    n_tiles = kt_ref.shape[2] // tk

    @pl.when(i == 0)
    def _():
        run = jnp.float32(0.0)
        for j in range(n_tiles):
            kf = kt_ref[0, 0:LANES, j * tk:(j + 1) * tk].astype(F32)
            sq = kf * kf
            norms = jnp.sqrt(jnp.maximum(jnp.sum(sq[0:DIFF_DH], axis=0, keepdims=True),
                                         jnp.sum(sq[DIFF_DH:LANES], axis=0, keepdims=True)))
            run = jnp.maximum(run, jnp.max(norms))
            kmax_ref[j] = run

    qa = qa_ref[0]
    lane = lax.broadcasted_iota(jnp.int32, (1, 2 * LANES), 1)
    zero = jnp.zeros_like(qa)
    qs = jnp.concatenate(
        [jnp.where(((lane >= m_ * DIFF_DH) & (lane < (m_ + 1) * DIFF_DH)) | (lane >= LANES), qa, zero)
         for m_ in range(2)], axis=0)
    t_row = q0 + (lax.broadcasted_iota(jnp.int32, (2 * tq, 1), 0) & (tq - 1))

    def scores(j):
        return _dot(qs, kt_ref[0, :, pl.ds(pl.multiple_of(j * tk, tk), tk)])

    def values(j):
        return v_ref[0, pl.ds(pl.multiple_of(j * tk, tk), tk), :]

    jd = q0 // tk
    col = jd * tk + lax.broadcasted_iota(jnp.int32, (1, tk), 1)
    state = _online_step(jnp.where(col <= t_row, scores(jd), NEG), *_softmax_init(2 * tq, LANES), values(jd))
    qf = jnp.where(lane < LANES, qs, jnp.zeros_like(qs)).astype(F32)
    q_norm = jnp.max(jnp.sqrt(jnp.sum(qf * qf, axis=-1, keepdims=True)))
    row_col = 2 * DIFF_DH + ALIBI_COLS - 1
    row_const = qs[:, row_col:row_col + 1].astype(F32)
    slope = slope_ref[h]

    floor = jnp.min(state[0] - row_const)

    def live(c):
        j = jnp.maximum(c[0], 0)
        bound = q_norm * kmax_ref[j] * 1.001 + slope * ((j + 1) * tk - 1).astype(F32)
        return (c[0] >= 0) & (bound - floor > -DIFF_DEAD)

    def body(c):
        return (c[0] - 1, *_online_step(scores(c[0]), c[1], c[2], c[3], values(c[0])))

    _, _, l, acc = lax.while_loop(live, body, (jd - 1, *state))
    out = acc / l
    lam = lam_ref[...]
    lam_full = (jnp.exp(jnp.sum(lam[0:1] * lam[1:2], axis=-1, keepdims=True))
                - jnp.exp(jnp.sum(lam[2:3] * lam[3:4], axis=-1, keepdims=True)) + lam_init)
    o = out[0:tq] - lam_full * out[tq:2 * tq]
    o_ref[0] = (_rms(o, g_ref[...]) * (1.0 - lam_init)).astype(o_ref.dtype)


def _diff_attention(oa, ob, lam, subln_g, lam_init, t):
    bsz = oa.shape[0]
    tq, tk = DIFF_TQ, DIFF_TK
    v_blk = DIFF_HEADS * 2
    slopes = _alibi_slopes(DIFF_HEADS) * LOG2E
    return pl.pallas_call(
        functools.partial(_diff_kernel, tq=tq, tk=tk, lam_init=lam_init),
        grid=(bsz, DIFF_HEADS, t // tq),
        in_specs=[
            pl.BlockSpec(memory_space=pltpu.SMEM),
            _const_spec(lam.shape),
            _const_spec((1, 2 * DIFF_DH)),
            pl.BlockSpec((1, tq, 2 * LANES), lambda b, h, i: (b, i, h)),
            pl.BlockSpec((1, 2 * LANES, t), lambda b, h, i: (b, h, 0)),
            pl.BlockSpec((1, t, LANES), lambda b, h, i: (b, 0, v_blk + h)),
        ],
        out_specs=pl.BlockSpec((1, tq, LANES), lambda b, h, i: (b, i, h)),
        out_shape=jax.ShapeDtypeStruct((bsz, t, D_MODEL), BF16),
        scratch_shapes=[pltpu.SMEM((t // tk,), F32)],
        compiler_params=_cparams(("arbitrary", "arbitrary", "arbitrary")),
        name="diff_attention",
    )(slopes, lam, subln_g.reshape(1, 2 * DIFF_DH), oa, ob, oa)


def _alibi_slopes(n):
    return jnp.exp2(-8.0 * jnp.arange(1, n + 1, dtype=F32) / n)


def _alibi_query_cols(slopes, first):
    sl2 = slopes * LOG2E
    parts = jnp.stack([p.astype(F32) for p in _split3(sl2)], axis=1)
    b0 = jnp.pad(jnp.concatenate([parts, parts], axis=1), ((0, 0), (first, LANES - first - 6)))
    pv = jnp.pad(-sl2[:, None], ((0, 0), (first + 6, LANES - first - ALIBI_COLS)))
    return b0, pv


def _alibi_key_rows():
    c = np.zeros((LANES, 8), np.float32)
    c[0:3, 0] = 1.0
    c[3:6, 1] = 1.0
    c[6, 2] = 1.0
    return jnp.asarray(c)


def _nsa_weights(w_in):
    bounds = [0, NSA_QW] + [NSA_QW + (k + 1) * NSA_KVW for k in range(6)]
    wq, wkc, wvc, wks, wvs, wkw, wvw = [w_in[:, bounds[k]:bounds[k + 1]] for k in range(7)]
    wg = jnp.pad(w_in[:, bounds[7]:], ((0, 0), (0, LANES - NSA_KV * GATE_COLS)))
    wa = jnp.concatenate([wq, wkc, wvc, wvs, wvw, wg], axis=1).astype(BF16)
    scale = NSA_DK ** -0.5 * LOG2E
    per = ROW_TILE // LANES
    qblocks = NSA_QW // LANES
    a_chunks = []
    for c in range(qblocks // per):
        ents = tuple(("half", b, 2 * (c * per + b) + hi, scale, hi, 2 * (c * per + b) + hi)
                     for b in range(per) for hi in range(2))
        a_chunks.append((c * ROW_TILE, ROW_TILE, "mm", ents))
    a_chunks.append((NSA_QW, 2 * NSA_KVW, "kv", ()))
    v0 = NSA_QW + 2 * NSA_KVW
    vs_blk, vw_blk, gt_blk = NSA_HEADS, NSA_HEADS + NSA_KV, NSA_HEADS + 2 * NSA_KV
    ents = tuple(("half", g // 2, vs_blk + g, 1.0, g % 2, -1) for g in range(NSA_KV)) + tuple(
        ("half", NSA_KV // 2 + g // 2, vw_blk + g, 1.0, g % 2, -1) for g in range(NSA_KV))
    a_chunks.append((v0, 2 * NSA_KVW, "mm", ents))
    a_chunks.append((v0 + 2 * NSA_KVW, LANES, "mm",
                     tuple(("gate", 0, gt_blk + g, GATE_COLS * g) for g in range(NSA_KV))))
    b0, pv = _alibi_query_cols(_alibi_slopes(NSA_HEADS), NSA_DK)
    wb = jnp.concatenate([wks, wkw], axis=1).T.astype(BF16)
    pieces = tuple((NSA_DK * g, NSA_DK, LANES * g) for g in range(2 * NSA_KV))
    b_aug = tuple((LANES * g + NSA_DK, LANES - NSA_DK) for g in range(2 * NSA_KV))
    plan = (tuple(a_chunks), (), gt_blk + NSA_KV, ((0, 2 * NSA_KVW, pieces),), b_aug, 2 * NSA_KV * LANES, True)
    return wa, b0, pv, wb, _alibi_key_rows(), plan


def _sb_weights(w_in):
    d = w_in.shape[0]
    wq, wk, wv = jnp.split(w_in, 3, axis=1)
    wa = jnp.concatenate([wq, wv], axis=1).astype(BF16)
    per = ROW_TILE // LANES
    nq = d // ROW_TILE
    a_chunks = tuple((c * ROW_TILE, ROW_TILE, "mm",
                      tuple(("mm", b, c * per + b, SB_DH ** -0.5 * LOG2E if c < nq else 1.0) for b in range(per)))
                     for c in range(2 * nq))
    b_chunks = tuple((c * ROW_TILE, ROW_TILE, ((0, ROW_TILE, c * ROW_TILE),)) for c in range(nq))
    zero = jnp.asarray(np.zeros((1, LANES), np.float32))
    plan = (a_chunks, (), 2 * d // LANES, b_chunks, (), d, False)
    return wa, zero, zero, wk.T.astype(BF16), jnp.asarray(np.zeros((LANES, 8), np.float32)), plan


def _diff_weights(w_in):
    d = w_in.shape[0]
    wq, wk, wv = jnp.split(w_in, 3, axis=1)
    wa = jnp.concatenate([wq, wv], axis=1).astype(BF16)
    scale = DIFF_DH ** -0.5 * LOG2E
    per = ROW_TILE // LANES
    nq = d // ROW_TILE
    a_chunks = tuple(
        (c * ROW_TILE, ROW_TILE, "mm",
         tuple(("mm", b, 2 * (c * per + b), scale) if c < nq else ("mm", b, 2 * DIFF_HEADS + (c - nq) * per + b, 1.0)
               for b in range(per)))
        for c in range(2 * nq))
    aug_blocks = tuple((2 * h + 1, h) for h in range(DIFF_HEADS))
    b0, pv = _alibi_query_cols(_alibi_slopes(DIFF_HEADS), 0)
    b_chunks = tuple((c * ROW_TILE, ROW_TILE,
                      tuple((LANES * k, LANES, 2 * LANES * (c * per + k)) for k in range(per)))
                     for c in range(nq))
    b_aug = tuple((2 * LANES * h + LANES, LANES) for h in range(DIFF_HEADS))
    plan = (a_chunks, aug_blocks, 3 * DIFF_HEADS, b_chunks, b_aug, 2 * DIFF_HEADS * LANES, False)
    return wa, b0, pv, wk.T.astype(BF16), _alibi_key_rows(), plan


def _nsa_mixer(x, g, mod, scl, shift, w_in, cmp_pe, cmp_w1, cmp_w2):
    t = x.shape[1]
    wa, b0, pv, wb, cb, plan = _nsa_weights(w_in)
    oa, ob, kv = _inproj(x, g, mod, scl, shift, wa, b0, pv, wb, cb, plan)
    nc = t // CMP_STRIDE
    w2kt = cmp_w2[0].T.astype(BF16)
    w2v = jnp.pad(cmp_w2[1], ((0, 0), (0, LANES - NSA_DK))).astype(BF16)
    kct, vcp = _nsa_compress(kv, cmp_pe, cmp_w1.astype(BF16), w2kt, w2v)
    tok = np.arange(t)
    e_mat = jnp.asarray(np.arange(N_SEL_PAD)[:, None] == (tok // SEL_LEN)[None, :], BF16)
    cstart = np.arange(nc) * CMP_STRIDE
    sstart = np.arange(N_SEL_PAD) * SEL_LEN
    ov_mat = jnp.asarray((cstart[None, :] < sstart[:, None] + SEL_LEN)
                         & (cstart[None, :] + CMP_LEN > sstart[:, None])
                         & (np.arange(nc)[None, :] < nc - 1), BF16)
    return _nsa_attention(oa, ob, kct, vcp, e_mat, ov_mat, t)


def _sb_mixer(x, g, mod, scl, shift, w_in):
    t = x.shape[1]
    wa, b0, pv, wb, cb, plan = _sb_weights(w_in)
    oa, ob = _inproj(x, g, mod, scl, shift, wa, b0, pv, wb, cb, plan)
    return _sb_attention(oa, ob, t)


def _diff_mixer(x, g, mod, scl, shift, w_in, lam, subln_g, layer_idx):
    t = x.shape[1]
    wa, b0, pv, wb, cb, plan = _diff_weights(w_in)
    oa, ob = _inproj(x, g, mod, scl, shift, wa, b0, pv, wb, cb, plan)
    lam_init = 0.8 - 0.6 * math.exp(-0.3 * layer_idx)
    return _diff_attention(oa, ob, lam, subln_g, lam_init, t)


def kernel(x, c, ada_w, ada_b, norm_g, ffn_w1, ffn_w2, nsa_w_in, nsa_cmp_pe, nsa_cmp_w1, nsa_cmp_w2,
           nsa_w_out, sb_w_in, sb_w_out, diff_w_in, diff_lam, diff_subln_g, diff_w_out):
    bsz, t, d = x.shape
    assert d == D_MODEL and t % ROW_TILE == 0 and t >= WIN + 128 and t <= N_SEL_PAD * SEL_LEN
    mod = _modulation(c, ada_w, ada_b)
    for i in range(DEPTH):
        def mods(sidx):
            return tuple((i, 3 * sidx + k) for k in range(3))

        shift, scl, gate = mods(0)
        x = _ffn(x, norm_g[i, 0], mod, scl, shift, gate, norm_g[i, 1],
                 ffn_w1[i, 0].astype(BF16), ffn_w2[i, 0].astype(BF16))
        shift, scl, gate = mods(1)
        kind, j = i % N_MIXERS, i // N_MIXERS
        if kind == 0:
            o = _nsa_mixer(x, norm_g[i, 2], mod, scl, shift, nsa_w_in[j], nsa_cmp_pe[j], nsa_cmp_w1[j],
                           nsa_cmp_w2[j])
            w_out = nsa_w_out[j]
        elif kind == 1:
            o = _sb_mixer(x, norm_g[i, 2], mod, scl, shift, sb_w_in[j])
            w_out = sb_w_out[j]
        else:
            o = _diff_mixer(x, norm_g[i, 2], mod, scl, shift, diff_w_in[j], diff_lam[j], diff_subln_g[j], i)
            w_out = diff_w_out[j]
        mixer = (o, w_out.astype(BF16), gate, norm_g[i, 3])
        shift, scl, gate = mods(2)
        x = _ffn(x, norm_g[i, 4], mod, scl, shift, gate, norm_g[i, 5],
                 ffn_w1[i, 1].astype(BF16), ffn_w2[i, 1].astype(BF16), mixer=mixer)
    return x
```

```python
import functools
import math

import jax
import jax.numpy as jnp
import numpy as np
from jax import lax
from jax.experimental import pallas as pl
from jax.experimental.pallas import tpu as pltpu

F32 = jnp.float32
BF16 = jnp.bfloat16

D_MODEL = 1024
DEPTH = 4
N_MIXERS = 3
EPS = 1e-6
NEG = -1e30
FORCE = 1e4
LOG2E = 1.4426950408889634
NSA_HEADS = 16
NSA_DK = 64
NSA_KV = 4
NSA_REP = NSA_HEADS // NSA_KV
CMP_LEN = 32
CMP_STRIDE = 16
CMP_HID = 256
SEL_LEN = 64
SEL_TOPK = 16
WIN = 512
NSA_QW = NSA_HEADS * NSA_DK
NSA_KVW = NSA_KV * NSA_DK
N_SEL_PAD = 128
SB_HEADS = 16
SB_DH = D_MODEL // SB_HEADS
DIFF_HEADS = 8
DIFF_DH = D_MODEL // (2 * DIFF_HEADS)
D_FF = 2816

LANES = 128
MOD_ROWS = 8
ROW_TILE = 512
FF_CHUNK = 256
DIFF_TQ, DIFF_TK = 512, 512
SB_TQ, SB_TK = 512, 256
SB_LINEAR_FROM = 64.0
SB_DEAD = 192.0
DIFF_DEAD = 160.0
ALIBI_COLS = 7
GATE_COLS = NSA_REP * 3
NSA_TQ, NSA_TK = 256, 512
CMP_CHUNK = 128
VMEM_LIMIT = 56 * 1024 * 1024


def _cparams(sem):
    return pltpu.CompilerParams(dimension_semantics=sem, vmem_limit_bytes=VMEM_LIMIT)


def _const_spec(shape):
    n = len(shape)
    return pl.BlockSpec(shape, lambda *_: (0,) * n)


def _rms(x, g):
    return x * lax.rsqrt(jnp.mean(x * x, axis=-1, keepdims=True) + EPS) * g


def _dot(a, b):
    return jnp.dot(a, b, preferred_element_type=F32)


def _dot_nt(a, b):
    return lax.dot_general(a, b, (((1,), (1,)), ((), ())), preferred_element_type=F32)


def _mod_kernel(c_ref, w_ref, b_ref, o_ref):
    c = c_ref[...]
    cond = c * jax.nn.sigmoid(c)
    o_ref[0] = _dot(cond.astype(BF16), w_ref[0].astype(BF16)) + b_ref[0]


def _modulation(c, ada_w, ada_b):
    depth, d, n = ada_w.shape
    b = c.shape[0]
    rows = MOD_ROWS
    c_pad = jnp.zeros((rows, d), F32).at[:b].set(c)
    nc = 1152
    out = pl.pallas_call(
        _mod_kernel,
        grid=(depth, n // nc),
        in_specs=[
            pl.BlockSpec((rows, d), lambda i, j: (0, 0)),
            pl.BlockSpec((1, d, nc), lambda i, j: (i, 0, j)),
            pl.BlockSpec((1, 1, nc), lambda i, j: (i, 0, j)),
        ],
        out_specs=pl.BlockSpec((1, rows, nc), lambda i, j: (i, 0, j)),
        out_shape=jax.ShapeDtypeStruct((depth, rows, n), F32),
        compiler_params=_cparams(("arbitrary", "arbitrary")),
        name="adaln_mod",
    )(c_pad, ada_w, ada_b.reshape(depth, 1, n))
    return out


def _mod_spec(sel):
    layer, blk = sel
    return pl.BlockSpec((1, MOD_ROWS, D_MODEL), lambda b, i: (layer, 0, blk))


def _mod_row(ref):
    return ref[0, pl.ds(pl.program_id(0), 1), :]


def _inproj_kernel(*refs, tt, a_chunks, aug_blocks, b_chunks, b_aug, has_kv):
    if has_kv:
        (x_ref, g_ref, scl_ref, sh_ref, wa_ref, b0_ref, pv_ref, wb_ref, cb_ref,
         oa_ref, ob_ref, oc_ref, kv_ref) = refs
    else:
        x_ref, g_ref, scl_ref, sh_ref, wa_ref, b0_ref, pv_ref, wb_ref, cb_ref, oa_ref, ob_ref = refs
    x = x_ref[0]
    hh = (_rms(x, g_ref[...]) * (1.0 + _mod_row(scl_ref)) + _mod_row(sh_ref)).astype(BF16)
    t0 = pl.program_id(1) * tt
    pos = (t0 + lax.broadcasted_iota(jnp.int32, (tt, 1), 0)).astype(F32)
    lane = lax.broadcasted_iota(jnp.int32, (1, LANES), 1)

    def aug(k):
        return b0_ref[k:k + 1, :] + pos * pv_ref[k:k + 1, :]

    def put(dst, val):
        oa_ref[0, :, dst * LANES:(dst + 1) * LANES] = val.astype(oa_ref.dtype)

    for lo, width, kind, entries in a_chunks:
        acc = _dot(hh, wa_ref[:, lo:lo + width])
        if kind == "kv":
            for k in range(width // LANES):
                kv_ref[k] = acc[:, k * LANES:(k + 1) * LANES]
                for l in range(CMP_STRIDE):
                    oc_ref[0, l, :, k * LANES:(k + 1) * LANES] = kv_ref[
                        k, pl.ds(l, tt // CMP_STRIDE, stride=CMP_STRIDE), :].astype(oc_ref.dtype)
            continue
        for e in entries:
            blk = acc[:, e[1] * LANES:(e[1] + 1) * LANES]
            if e[0] == "mm":
                put(e[2], blk * e[3] if e[3] != 1.0 else blk)
            elif e[0] == "half":
                if e[3] != 1.0:
                    blk = blk * e[3]
                if e[4]:
                    blk = pltpu.roll(blk, NSA_DK, axis=1)
                fill = aug(e[5]) if e[5] >= 0 else jnp.where(lane == NSA_DK, 1.0, 0.0)
                put(e[2], jnp.where(lane < NSA_DK, blk, fill))
            else:
                if e[3]:
                    blk = pltpu.roll(blk, LANES - e[3], axis=1)
                put(e[2], jnp.where(lane < GATE_COLS, blk, 0.0))
    for dst, k in aug_blocks:
        put(dst, aug(k))

    tpos = t0 + lax.broadcasted_iota(jnp.int32, (1, tt), 1)
    phi = ((tpos >> 7) << 7).astype(F32)
    plo = (tpos & 127).astype(F32)
    if b_aug:
        cc = cb_ref[...]
        key_rows = (cc[:, 0:1] * phi + cc[:, 1:2] * plo + cc[:, 2:3]).astype(ob_ref.dtype)
        for dst, nrows in b_aug:
            ob_ref[0, dst:dst + nrows, :] = key_rows[0:nrows]
    for lo, rows, pieces in b_chunks:
        acc = _dot_nt(wb_ref[lo:lo + rows, :], hh)
        for src, nrows, dst in pieces:
            ob_ref[0, dst:dst + nrows, :] = acc[src:src + nrows].astype(ob_ref.dtype)


def _inproj(x, g, mod, scl, shift, wa, b0, pv, wb, cb, plan):
    bsz, t, d = x.shape
    tt = min(ROW_TILE, t)
    a_chunks, aug_blocks, n_blocks, b_chunks, b_aug, n_rows, has_kv = plan
    na = n_blocks * LANES
    in_specs = [
        pl.BlockSpec((1, tt, d), lambda b, i: (b, i, 0)),
        _const_spec((1, d)),
        _mod_spec(scl),
        _mod_spec(shift),
        _const_spec(wa.shape),
        _const_spec(b0.shape),
        _const_spec(pv.shape),
        _const_spec(wb.shape),
        _const_spec(cb.shape),
    ]
    out_specs = [pl.BlockSpec((1, tt, na), lambda b, i: (b, i, 0)),
                 pl.BlockSpec((1, n_rows, tt), lambda b, i: (b, 0, i))]
    out_shape = [jax.ShapeDtypeStruct((bsz, t, na), BF16), jax.ShapeDtypeStruct((bsz, n_rows, t), BF16)]
    scratch = []
    if has_kv:
        kvw = 2 * NSA_KVW
        out_specs.append(pl.BlockSpec((1, CMP_STRIDE, tt // CMP_STRIDE, kvw), lambda b, i: (b, 0, i, 0)))
        out_shape.append(jax.ShapeDtypeStruct((bsz, CMP_STRIDE, t // CMP_STRIDE, kvw), BF16))
        scratch.append(pltpu.VMEM((kvw // LANES, tt, LANES), F32))
    return pl.pallas_call(
        functools.partial(_inproj_kernel, tt=tt, a_chunks=a_chunks, aug_blocks=aug_blocks, b_chunks=b_chunks,
                          b_aug=b_aug, has_kv=has_kv),
        grid=(bsz, t // tt),
        in_specs=in_specs,
        out_specs=out_specs,
        out_shape=out_shape,
        scratch_shapes=scratch,
        compiler_params=_cparams(("arbitrary", "arbitrary")),
        name="mixer_inproj",
    )(x, g.reshape(1, d), mod, mod, wa, b0, pv, wb, cb)


def _ffn_kernel(*refs, after_mixer):
    if after_mixer:
        mo_ref, wo_ref, mgate_ref, gm_ref = refs[:4]
        refs = refs[4:]
    x_ref, g1_ref, scl_ref, sh_ref, gate_ref, g2_ref, w1_ref, w2_ref, o_ref, acc_ref = refs
    x = x_ref[0]
    if after_mixer:
        x = x + _mod_row(mgate_ref) * _rms(_dot(mo_ref[0], wo_ref[...]), gm_ref[...])
    hh = (_rms(x, g1_ref[...]) * (1.0 + _mod_row(scl_ref)) + _mod_row(sh_ref)).astype(BF16)
    dff = w2_ref.shape[0]
    for c in range(dff // FF_CHUNK):
        lo = c * FF_CHUNK
        gt = _dot(hh, w1_ref[:, lo:lo + FF_CHUNK])
        up = _dot(hh, w1_ref[:, dff + lo:dff + lo + FF_CHUNK])
        act = (gt * jax.nn.sigmoid(gt) * up).astype(BF16)
        part = _dot(act, w2_ref[lo:lo + FF_CHUNK, :])
        if c == 0:
            acc_ref[...] = part
        else:
            acc_ref[...] += part
    o_ref[0] = x + 0.5 * _mod_row(gate_ref) * _rms(acc_ref[...], g2_ref[...])


def _ffn(x, g1, mod, scl, shift, gate, g2, w1, w2, mixer=None):
    bsz, t, d = x.shape
    tt = min(ROW_TILE, t)
    in_specs = [
        pl.BlockSpec((1, tt, d), lambda b, i: (b, i, 0)),
        _const_spec((1, d)), _mod_spec(scl), _mod_spec(shift), _mod_spec(gate), _const_spec((1, d)),
        _const_spec(w1.shape), _const_spec(w2.shape),
    ]
    args = [x, g1.reshape(1, d), mod, mod, mod, g2.reshape(1, d), w1, w2]
    if mixer is not None:
        mo, wo, mgate, gm = mixer
        in_specs = [pl.BlockSpec((1, tt, mo.shape[2]), lambda b, i: (b, i, 0)), _const_spec(wo.shape),
                    _mod_spec(mgate), _const_spec((1, d))] + in_specs
        args = [mo, wo, mod, gm.reshape(1, d)] + args
    return pl.pallas_call(
        functools.partial(_ffn_kernel, after_mixer=mixer is not None),
        grid=(bsz, t // tt),
        in_specs=in_specs,
        out_specs=pl.BlockSpec((1, tt, d), lambda b, i: (b, i, 0)),
        out_shape=jax.ShapeDtypeStruct((bsz, t, d), F32),
        scratch_shapes=[pltpu.VMEM((tt, d), F32)],
        compiler_params=_cparams(("arbitrary", "arbitrary")),
        name="ffn",
    )(*args)


def _online_step(s, m, acc, v):
    m_new = jnp.maximum(m, jnp.max(s, axis=-1, keepdims=True))
    alpha = jnp.exp2(m - m_new)
    return m_new, alpha * acc + _dot(jnp.exp2(s - m_new).astype(BF16), v)


def _softmax_init(rows, width):
    return jnp.full((rows, 1), NEG, F32), jnp.zeros((rows, width), F32)


def _online_step_sum(s, m, l, acc, v):
    m_new = jnp.maximum(m, jnp.max(s, axis=-1, keepdims=True))
    alpha = jnp.exp2(m - m_new)
    p = jnp.exp2(s - m_new)
    return m_new, alpha * l + jnp.sum(p, axis=-1, keepdims=True), alpha * acc + _dot(p.astype(BF16), v)


def _gelu_tanh(x):
    return 0.5 * x * (1.0 + jnp.tanh(math.sqrt(2.0 / math.pi) * (x + 0.044715 * (x * x * x))))


def _cmp_kernel(kv_ref, pe_ref, w1_ref, w2kt_ref, w2v_ref, kt_ref, v_ref):
    nc = kv_ref.shape[2]

    def hidden(j, g):
        lo = j * NSA_KVW + g * NSA_DK
        top = bot = None
        for l in range(CMP_STRIDE):
            x = kv_ref[0, l, :, lo:lo + NSA_DK].astype(F32)
            hi = CMP_STRIDE + l
            a = _dot((x + pe_ref[j, l:l + 1, :]).astype(BF16), w1_ref[j, l * NSA_DK:(l + 1) * NSA_DK, :])
            b = _dot((x + pe_ref[j, hi:hi + 1, :]).astype(BF16), w1_ref[j, hi * NSA_DK:(hi + 1) * NSA_DK, :])
            top = a if top is None else top + a
            bot = b if bot is None else bot + b
        return _gelu_tanh(top + pltpu.roll(bot, nc - 1, axis=0)).astype(BF16)

    n = lax.broadcasted_iota(jnp.int32, (1, nc), 1)
    end = n * CMP_STRIDE + (CMP_LEN - 1)
    phi = ((end >> 7) << 7).astype(F32)
    plo = (end & 127).astype(F32)
    r = lax.broadcasted_iota(jnp.int32, (LANES - NSA_DK, 1), 0)
    aug = jnp.where(r < 3, phi, jnp.where(r < 6, plo, jnp.where(r == 6, 1.0, 0.0))).astype(BF16)
    for g in range(NSA_KV):
        kt_ref[0, g, 0:NSA_DK, :] = _dot_nt(w2kt_ref[...], hidden(0, g)).astype(BF16)
        kt_ref[0, g, NSA_DK:LANES, :] = aug
        v_ref[0, g] = _dot(hidden(1, g), w2v_ref[...]).astype(BF16)


def _nsa_compress(kv, pe, w1, w2kt, w2v):
    bsz, _, nc, kvw = kv.shape
    return pl.pallas_call(
        _cmp_kernel,
        grid=(bsz,),
        in_specs=[pl.BlockSpec((1, CMP_STRIDE, nc, kvw), lambda b: (b, 0, 0, 0)), _const_spec(pe.shape),
                  _const_spec(w1.shape), _const_spec(w2kt.shape), _const_spec(w2v.shape)],
        out_specs=[pl.BlockSpec((1, NSA_KV, LANES, nc), lambda b: (b, 0, 0, 0)),
                   pl.BlockSpec((1, NSA_KV, nc, LANES), lambda b: (b, 0, 0, 0))],
        out_shape=[jax.ShapeDtypeStruct((bsz, NSA_KV, LANES, nc), BF16),
                   jax.ShapeDtypeStruct((bsz, NSA_KV, nc, LANES), BF16)],
        compiler_params=_cparams(("arbitrary",)),
        name="nsa_compress",
    )(kv, pe, w1, w2kt, w2v)


def _normalized(acc):
    lane = lax.broadcasted_iota(jnp.int32, (1, LANES), 1)
    return jnp.where(lane < NSA_DK, acc / acc[:, NSA_DK:NSA_DK + 1], 0.0)


def _split3(x):
    hi = x.astype(BF16)
    r1 = x - hi.astype(F32)
    mid = r1.astype(BF16)
    lo = (r1 - mid.astype(F32)).astype(BF16)
    return hi, mid, lo


def _nsa_attn_kernel(qa_ref, kct_ref, vc_ref, kst_ref, vs_ref, kwt_ref, vw_ref, gt_ref, e_ref, ov_ref,
                     o_ref, tiles_ref, *, tq, tk):
    i = pl.program_id(2)
    q0 = i * tq
    rows = NSA_REP * tq
    qa = qa_ref[0]
    qs = jnp.concatenate([qa[:, r * LANES:(r + 1) * LANES] for r in range(NSA_REP)], axis=0)
    t_row = q0 + (lax.broadcasted_iota(jnp.int32, (rows, 1), 0) & (tq - 1))

    nc = kct_ref.shape[3]

    def compressed(width):
        s = _dot(qs, kct_ref[0, 0, :, 0:width])
        cend = lax.broadcasted_iota(jnp.int32, (1, width), 1) * CMP_STRIDE + (CMP_LEN - 1)
        valid = cend <= t_row
        s = jnp.where(valid, s, NEG)
        e = jnp.where(valid, jnp.exp2(s - jnp.max(s, axis=-1, keepdims=True)), 0.0)
        l = jnp.sum(e, axis=-1, keepdims=True)
        p = e / jnp.where(l > 0.0, l, 1.0)
        o = _dot(p.astype(BF16), vc_ref[0, 0, 0:width, :])
        p4 = p[0:tq]
        for r in range(1, NSA_REP):
            p4 = p4 + p[r * tq:(r + 1) * tq]
        ov = ov_ref[:, 0:width]
        return o, sum(_dot_nt(ov, part) for part in _split3(p4))

    chunk = min(CMP_CHUNK, nc)
    widths = list(range(chunk, nc + 1, chunk))
    n_ended = jnp.maximum(q0 + tq - CMP_LEN, 0) // CMP_STRIDE + 1
    o_cmp, imp = lax.switch((n_ended - 1) // chunk, [functools.partial(compressed, w) for w in widths])

    wk = WIN + tq
    st = pl.multiple_of(jnp.maximum(q0 - WIN, 0), LANES)
    col = st + lax.broadcasted_iota(jnp.int32, (1, wk), 1)
    valid = (t_row - col).astype(jnp.uint32) < WIN
    s = jnp.where(valid, _dot(qs, kwt_ref[0, :, pl.ds(st, wk)]), NEG)
    e = jnp.exp2(s - jnp.max(s, axis=-1, keepdims=True))
    o_win = _normalized(_dot(e.astype(BF16), vw_ref[0, pl.ds(st, wk), :]))

    cur = (q0 + lax.broadcasted_iota(jnp.int32, (1, tq), 1)) >> 6
    sid = lax.broadcasted_iota(jnp.int32, (N_SEL_PAD, 1), 0)
    forced = (sid == 0) | (sid == cur) | (sid == cur - 1)
    work = jnp.where(forced | (sid > cur), -3.0e38, imp)
    selb = jnp.where(forced, 0.0, NEG)
    for _ in range(SEL_TOPK - 3):
        mx = jnp.max(work, axis=0, keepdims=True)
        first = jnp.min(jnp.where(work == mx, sid, N_SEL_PAD), axis=0, keepdims=True)
        pick = sid == first
        selb = jnp.where(pick, 0.0, selb)
        work = jnp.where(pick, -3.0e38, work)
    selb = selb.T

    jd = q0 // tk
    n_tiles = kst_ref.shape[2] // tk
    any_sel = jnp.max(jnp.where(selb == 0.0, 1.0, 0.0), axis=0, keepdims=True)
    blk_tile = (lax.broadcasted_iota(jnp.int32, (N_SEL_PAD, LANES), 0) // (tk // SEL_LEN)
                == lax.broadcasted_iota(jnp.int32, (N_SEL_PAD, LANES), 1)).astype(BF16)
    tile_cnt = _dot(jnp.broadcast_to(any_sel, (16, N_SEL_PAD)).astype(BF16), blk_tile)
    n_act = jnp.int32(0)
    for j in range(n_tiles):
        tiles_ref[n_act] = j
        n_act = n_act + ((tile_cnt[0, j] > 0.0) & (j < jd)).astype(jnp.int32)

    selb4 = jnp.concatenate([selb.astype(BF16)] * NSA_REP, axis=0)
    qsel = jnp.concatenate([qs, selb4], axis=1)

    def sel_scores(st, width):
        return _dot(qsel, jnp.concatenate([kst_ref[0, :, pl.ds(st, width)], e_ref[:, pl.ds(st, width)]], axis=0))

    def sel_body(it, carry):
        st = pl.multiple_of(tiles_ref[it] * tk, tk)
        return _online_step(sel_scores(st, tk), *carry, vs_ref[0, pl.ds(st, tk), :])

    carry = lax.fori_loop(0, n_act, sel_body, _softmax_init(rows, LANES))
    st = pl.multiple_of(jd * tk, tk)

    def diagonal(width):
        col = st + lax.broadcasted_iota(jnp.int32, (1, width), 1)
        s = jnp.where(col <= t_row, sel_scores(st, width), NEG)
        return _normalized(_online_step(s, *carry, vs_ref[0, pl.ds(st, width), :])[1])

    o_sel = lax.switch((q0 - st) // tq, [functools.partial(diagonal, w) for w in range(tq, tk + 1, tq)])

    gs = jax.nn.sigmoid(gt_ref[0].astype(F32))
    outs = []
    for r in range(NSA_REP):
        sl = slice(r * tq, (r + 1) * tq)
        outs.append(gs[:, 3 * r:3 * r + 1] * o_cmp[sl] + gs[:, 3 * r + 1:3 * r + 2] * o_sel[sl]
                    + gs[:, 3 * r + 2:3 * r + 3] * o_win[sl])
    pairs = [outs[2 * a] + pltpu.roll(outs[2 * a + 1], NSA_DK, axis=1) for a in range(NSA_REP // 2)]
    o_ref[0] = jnp.concatenate(pairs, axis=1).astype(o_ref.dtype)


def _nsa_attention(oa, ob, kct, vcp, e_mat, ov_mat, t):
    bsz = oa.shape[0]
    tq, tk = NSA_TQ, NSA_TK
    qw = NSA_REP * LANES
    vs_blk = NSA_HEADS
    vw_blk = vs_blk + NSA_KV
    gt_blk = vw_blk + NSA_KV
    nc = kct.shape[3]
    return pl.pallas_call(
        functools.partial(_nsa_attn_kernel, tq=tq, tk=tk),
        grid=(bsz, NSA_KV, t // tq),
        in_specs=[
            pl.BlockSpec((1, tq, qw), lambda b, g, i: (b, i, g)),
            pl.BlockSpec((1, 1, LANES, nc), lambda b, g, i: (b, g, 0, 0)),
            pl.BlockSpec((1, 1, nc, LANES), lambda b, g, i: (b, g, 0, 0)),
            pl.BlockSpec((1, LANES, t), lambda b, g, i: (b, g, 0)),
            pl.BlockSpec((1, t, LANES), lambda b, g, i: (b, 0, vs_blk + g)),
            pl.BlockSpec((1, LANES, t), lambda b, g, i: (b, NSA_KV + g, 0)),
            pl.BlockSpec((1, t, LANES), lambda b, g, i: (b, 0, vw_blk + g)),
            pl.BlockSpec((1, tq, LANES), lambda b, g, i: (b, i, gt_blk + g)),
            _const_spec(e_mat.shape),
            _const_spec(ov_mat.shape),
        ],
        out_specs=pl.BlockSpec((1, tq, NSA_REP * NSA_DK), lambda b, g, i: (b, i, g)),
        out_shape=jax.ShapeDtypeStruct((bsz, t, NSA_QW), BF16),
        scratch_shapes=[pltpu.SMEM((t // tk,), jnp.int32)],
        compiler_params=_cparams(("arbitrary", "arbitrary", "arbitrary")),
        name="nsa_attention",
    )(oa, kct, vcp, ob, oa, ob, oa, oa, e_mat, ov_mat)


def _sb_kernel(q_ref, kt_ref, v_ref, o_ref, rem_ref, acc_ref, *, tq, tk):
    i = pl.program_id(2)
    q0 = i * tq
    q = q_ref[0]
    lane = lax.broadcasted_iota(jnp.int32, (1, LANES), 1)
    zero = jnp.zeros_like(q)
    qs = jnp.concatenate([jnp.where(lane < SB_DH, q, zero), jnp.where(lane >= SB_DH, q, zero)], axis=0)
    t_row = q0 + (lax.broadcasted_iota(jnp.int32, (2 * tq, 1), 0) & (tq - 1))
    upper = (lax.broadcasted_iota(jnp.int32, (tk, tk), 0) > lax.broadcasted_iota(jnp.int32, (tk, tk), 1)
             ).astype(BF16)
    rem_ref[...] = jnp.zeros(rem_ref.shape, F32)
    acc_ref[...] = jnp.zeros(acc_ref.shape, F32)

    def tiles(js, masked):
        sts = [pl.multiple_of(j * tk, tk) for j in js]
        zs = [_dot(qs, kt_ref[0, :, pl.ds(st, tk)]) for st in sts]
        costs, log_bs, befores = [], [], []
        for st, z in zip(sts, zs):
            cost = jnp.where(z > SB_LINEAR_FROM, z, jnp.log(1.0 + jnp.exp2(z)) * LOG2E)
            log_bs.append(z - cost)
            before = None
            if masked:
                before = (st + lax.broadcasted_iota(jnp.int32, (1, tk), 1)) < t_row
                cost = jnp.where(before, cost, 0.0)
            costs.append(cost.astype(BF16))
            befores.append(before)
        rems = [_dot(c, upper) for c in costs]
        rem_right = rem_ref[...]
        probs = []
        for log_b, c, rem, before in zip(log_bs, costs, rems, befores):
            ex = log_b - (rem + rem_right)
            if masked:
                ex = jnp.where(before, ex, NEG)
            probs.append(jnp.exp2(ex).astype(BF16))
            rem_right = rem_right + (rem[:, 0:1] + c[:, 0:1].astype(F32))
        rem_ref[...] = rem_right
        acc = acc_ref[...]
        for st, p in zip(sts, probs):
            acc = acc + _dot(p, v_ref[0, pl.ds(st, tk), :])
        acc_ref[...] = acc
        return jnp.min(rem_right)

    per_step = tq // tk
    n_full = q0 // tk
    spent = tiles([n_full + per_step - 1 - d for d in range(per_step)], True)

    def go_on(c):
        return (c[0] < n_full // per_step) & (c[1] < SB_DEAD)

    def body(c):
        it = c[0]
        return it + 1, tiles([n_full - 1 - per_step * it - d for d in range(per_step)], False)

    lax.while_loop(go_on, body, (jnp.int32(0), spent))
    acc = acc_ref[...]
    o_ref[0] = jnp.where(lane < SB_DH, acc[0:tq], acc[tq:2 * tq]).astype(o_ref.dtype)


def _sb_attention(oa, ob, t):
    bsz = oa.shape[0]
    tq, tk = SB_TQ, SB_TK
    npair = SB_HEADS // 2
    return pl.pallas_call(
        functools.partial(_sb_kernel, tq=tq, tk=tk),
        grid=(bsz, npair, t // tq),
        in_specs=[
            pl.BlockSpec((1, tq, LANES), lambda b, h, i: (b, i, h)),
            pl.BlockSpec((1, LANES, t), lambda b, h, i: (b, h, 0)),
            pl.BlockSpec((1, t, LANES), lambda b, h, i: (b, 0, npair + h)),
        ],
        out_specs=pl.BlockSpec((1, tq, LANES), lambda b, h, i: (b, i, h)),
        out_shape=jax.ShapeDtypeStruct((bsz, t, D_MODEL), BF16),
        scratch_shapes=[pltpu.VMEM((2 * tq, 1), F32), pltpu.VMEM((2 * tq, LANES), F32)],
        compiler_params=_cparams(("arbitrary", "arbitrary", "arbitrary")),
        name="sb_attention",
    )(oa, ob, oa)


def _diff_kernel(slope_ref, lam_ref, g_ref, qa_ref, kt_ref, v_ref, o_ref, kmax_ref, *, tq, tk, lam_init):
    h = pl.program_id(1)
    i = pl.program_id(2)
    q0 = i * tq
    n_tiles = kt_ref.shape[2] // tk

    @pl.when(i == 0)
    def _():
        run = jnp.float32(0.0)
        for j in range(n_tiles):
            kf = kt_ref[0, 0:LANES, j * tk:(j + 1) * tk].astype(F32)
            sq = kf * kf
            norms = jnp.sqrt(jnp.maximum(jnp.sum(sq[0:DIFF_DH], axis=0, keepdims=True),
                                         jnp.sum(sq[DIFF_DH:LANES], axis=0, keepdims=True)))
            run = jnp.maximum(run, jnp.max(norms))
            kmax_ref[j] = run

    qa = qa_ref[0]
    lane = lax.broadcasted_iota(jnp.int32, (1, 2 * LANES), 1)
    zero = jnp.zeros_like(qa)
    qs = jnp.concatenate(
        [jnp.where(((lane >= m_ * DIFF_DH) & (lane < (m_ + 1) * DIFF_DH)) | (lane >= LANES), qa, zero)
         for m_ in range(2)], axis=0)
    t_row = q0 + (lax.broadcasted_iota(jnp.int32, (2 * tq, 1), 0) & (tq - 1))

    def scores(j):
        return _dot(qs, kt_ref[0, :, pl.ds(pl.multiple_of(j * tk, tk), tk)])

    def values(j):
        return v_ref[0, pl.ds(pl.multiple_of(j * tk, tk), tk), :]

    jd = q0 // tk
    col = jd * tk + lax.broadcasted_iota(jnp.int32, (1, tk), 1)
    m0, acc0 = _softmax_init(2 * tq, LANES)
    state = _online_step_sum(jnp.where(col <= t_row, scores(jd), NEG), m0, jnp.zeros_like(m0), acc0, values(jd))
    qf = jnp.where(lane < LANES, qs, jnp.zeros_like(qs)).astype(F32)
    q_norm = jnp.max(jnp.sqrt(jnp.sum(qf * qf, axis=-1, keepdims=True)))
    row_col = 2 * DIFF_DH + ALIBI_COLS - 1
    row_const = qs[:, row_col:row_col + 1].astype(F32)
    slope = slope_ref[h]

    floor = jnp.min(state[0] - row_const)

    def live(c):
        j = jnp.maximum(c[0], 0)
        bound = q_norm * kmax_ref[j] * 1.001 + slope * ((j + 1) * tk - 1).astype(F32)
        return (c[0] >= 0) & (bound - floor > -DIFF_DEAD)

    def body(c):
        return (c[0] - 1, *_online_step_sum(scores(c[0]), c[1], c[2], c[3], values(c[0])))

    _, _, l, acc = lax.while_loop(live, body, (jd - 1, *state))
    out = acc / l
    lam = lam_ref[...]
    lam_full = (jnp.exp(jnp.sum(lam[0:1] * lam[1:2], axis=-1, keepdims=True))
                - jnp.exp(jnp.sum(lam[2:3] * lam[3:4], axis=-1, keepdims=True)) + lam_init)
    o = out[0:tq] - lam_full * out[tq:2 * tq]
    o_ref[0] = (_rms(o, g_ref[...]) * (1.0 - lam_init)).astype(o_ref.dtype)


def _diff_attention(oa, ob, lam, subln_g, lam_init, t):
    bsz = oa.shape[0]
    tq, tk = DIFF_TQ, DIFF_TK
    slopes = _alibi_slopes(DIFF_HEADS) * LOG2E
    return pl.pallas_call(
        functools.partial(_diff_kernel, tq=tq, tk=tk, lam_init=lam_init),
        grid=(bsz, DIFF_HEADS, t // tq),
        in_specs=[
            pl.BlockSpec(memory_space=pltpu.SMEM),
            _const_spec(lam.shape),
            _const_spec((1, 2 * DIFF_DH)),
            pl.BlockSpec((1, tq, 2 * LANES), lambda b, h, i: (b, i, h)),
            pl.BlockSpec((1, 2 * LANES, t), lambda b, h, i: (b, h, 0)),
            pl.BlockSpec((1, t, LANES), lambda b, h, i: (b, 0, 2 * DIFF_HEADS + h)),
        ],
        out_specs=pl.BlockSpec((1, tq, LANES), lambda b, h, i: (b, i, h)),
        out_shape=jax.ShapeDtypeStruct((bsz, t, D_MODEL), BF16),
        scratch_shapes=[pltpu.SMEM((t // tk,), F32)],
        compiler_params=_cparams(("arbitrary", "arbitrary", "arbitrary")),
        name="diff_attention",
    )(slopes, lam, subln_g.reshape(1, 2 * DIFF_DH), oa, ob, oa)


def _alibi_slopes(n):
    return jnp.exp2(-8.0 * jnp.arange(1, n + 1, dtype=F32) / n)


def _alibi_query_cols(slopes, first):
    sl2 = slopes * LOG2E
    parts = jnp.stack([p.astype(F32) for p in _split3(sl2)], axis=1)
    b0 = jnp.pad(jnp.concatenate([parts, parts], axis=1), ((0, 0), (first, LANES - first - 6)))
    pv = jnp.pad(-sl2[:, None], ((0, 0), (first + 6, LANES - first - ALIBI_COLS)))
    return b0, pv


def _alibi_key_rows():
    c = np.zeros((LANES, 8), np.float32)
    c[0:3, 0] = 1.0
    c[3:6, 1] = 1.0
    c[6, 2] = 1.0
    return jnp.asarray(c)


def _nsa_weights(w_in):
    bounds = [0, NSA_QW] + [NSA_QW + (k + 1) * NSA_KVW for k in range(6)]
    wq, wkc, wvc, wks, wvs, wkw, wvw = [w_in[:, bounds[k]:bounds[k + 1]] for k in range(7)]
    wg = jnp.pad(w_in[:, bounds[7]:], ((0, 0), (0, LANES - NSA_KV * GATE_COLS)))
    wa = jnp.concatenate([wq, wkc, wvc, wvs, wvw, wg], axis=1).astype(BF16)
    scale = NSA_DK ** -0.5 * LOG2E
    per = ROW_TILE // LANES
    qblocks = NSA_QW // LANES
    a_chunks = []
    for c in range(qblocks // per):
        ents = tuple(("half", b, 2 * (c * per + b) + hi, scale, hi, 2 * (c * per + b) + hi)
                     for b in range(per) for hi in range(2))
        a_chunks.append((c * ROW_TILE, ROW_TILE, "mm", ents))
    a_chunks.append((NSA_QW, 2 * NSA_KVW, "kv", ()))
    v0 = NSA_QW + 2 * NSA_KVW
    vs_blk, vw_blk, gt_blk = NSA_HEADS, NSA_HEADS + NSA_KV, NSA_HEADS + 2 * NSA_KV
    ents = tuple(("half", g // 2, vs_blk + g, 1.0, g % 2, -1) for g in range(NSA_KV)) + tuple(
        ("half", NSA_KV // 2 + g // 2, vw_blk + g, 1.0, g % 2, -1) for g in range(NSA_KV))
    a_chunks.append((v0, 2 * NSA_KVW, "mm", ents))
    a_chunks.append((v0 + 2 * NSA_KVW, LANES, "mm",
                     tuple(("gate", 0, gt_blk + g, GATE_COLS * g) for g in range(NSA_KV))))
    b0, pv = _alibi_query_cols(_alibi_slopes(NSA_HEADS), NSA_DK)
    wb = jnp.concatenate([wks, wkw], axis=1).T.astype(BF16)
    pieces = tuple((NSA_DK * g, NSA_DK, LANES * g) for g in range(2 * NSA_KV))
    b_aug = tuple((LANES * g + NSA_DK, LANES - NSA_DK) for g in range(2 * NSA_KV))
    plan = (tuple(a_chunks), (), gt_blk + NSA_KV, ((0, 2 * NSA_KVW, pieces),), b_aug, 2 * NSA_KV * LANES, True)
    return wa, b0, pv, wb, _alibi_key_rows(), plan


def _sb_weights(w_in):
    d = w_in.shape[0]
    wq, wk, wv = jnp.split(w_in, 3, axis=1)
    wa = jnp.concatenate([wq, wv], axis=1).astype(BF16)
    per = ROW_TILE // LANES
    nq = d // ROW_TILE
    a_chunks = tuple((c * ROW_TILE, ROW_TILE, "mm",
                      tuple(("mm", b, c * per + b, SB_DH ** -0.5 * LOG2E if c < nq else 1.0) for b in range(per)))
                     for c in range(2 * nq))
    b_chunks = tuple((c * ROW_TILE, ROW_TILE, ((0, ROW_TILE, c * ROW_TILE),)) for c in range(nq))
    zero = jnp.asarray(np.zeros((1, LANES), np.float32))
    plan = (a_chunks, (), 2 * d // LANES, b_chunks, (), d, False)
    return wa, zero, zero, wk.T.astype(BF16), jnp.asarray(np.zeros((LANES, 8), np.float32)), plan


def _diff_weights(w_in):
    d = w_in.shape[0]
    wq, wk, wv = jnp.split(w_in, 3, axis=1)
    wa = jnp.concatenate([wq, wv], axis=1).astype(BF16)
    scale = DIFF_DH ** -0.5 * LOG2E
    per = ROW_TILE // LANES
    nq = d // ROW_TILE
    a_chunks = tuple(
        (c * ROW_TILE, ROW_TILE, "mm",
         tuple(("mm", b, 2 * (c * per + b), scale) if c < nq else ("mm", b, 2 * DIFF_HEADS + (c - nq) * per + b, 1.0)
               for b in range(per)))
        for c in range(2 * nq))
    aug_blocks = tuple((2 * h + 1, h) for h in range(DIFF_HEADS))
    b0, pv = _alibi_query_cols(_alibi_slopes(DIFF_HEADS), 0)
    b_chunks = tuple((c * ROW_TILE, ROW_TILE,
                      tuple((LANES * k, LANES, 2 * LANES * (c * per + k)) for k in range(per)))
                     for c in range(nq))
    b_aug = tuple((2 * LANES * h + LANES, LANES) for h in range(DIFF_HEADS))
    plan = (a_chunks, aug_blocks, 3 * DIFF_HEADS, b_chunks, b_aug, 2 * DIFF_HEADS * LANES, False)
    return wa, b0, pv, wk.T.astype(BF16), _alibi_key_rows(), plan


def _nsa_mixer(x, g, mod, scl, shift, w_in, cmp_pe, cmp_w1, cmp_w2):
    t = x.shape[1]
    wa, b0, pv, wb, cb, plan = _nsa_weights(w_in)
    oa, ob, kv = _inproj(x, g, mod, scl, shift, wa, b0, pv, wb, cb, plan)
    nc = t // CMP_STRIDE
    w2kt = cmp_w2[0].T.astype(BF16)
    w2v = jnp.pad(cmp_w2[1], ((0, 0), (0, LANES - NSA_DK))).astype(BF16)
    kct, vcp = _nsa_compress(kv, cmp_pe, cmp_w1.astype(BF16), w2kt, w2v)
    tok = np.arange(t)
    e_mat = jnp.asarray(np.arange(N_SEL_PAD)[:, None] == (tok // SEL_LEN)[None, :], BF16)
    cstart = np.arange(nc) * CMP_STRIDE
    sstart = np.arange(N_SEL_PAD) * SEL_LEN
    ov_mat = jnp.asarray((cstart[None, :] < sstart[:, None] + SEL_LEN)
                         & (cstart[None, :] + CMP_LEN > sstart[:, None])
                         & (np.arange(nc)[None, :] < nc - 1), BF16)
    return _nsa_attention(oa, ob, kct, vcp, e_mat, ov_mat, t)


def _sb_mixer(x, g, mod, scl, shift, w_in):
    t = x.shape[1]
    wa, b0, pv, wb, cb, plan = _sb_weights(w_in)
    oa, ob = _inproj(x, g, mod, scl, shift, wa, b0, pv, wb, cb, plan)
    return _sb_attention(oa, ob, t)


def _diff_mixer(x, g, mod, scl, shift, w_in, lam, subln_g, layer_idx):
    t = x.shape[1]
    wa, b0, pv, wb, cb, plan = _diff_weights(w_in)
    oa, ob = _inproj(x, g, mod, scl, shift, wa, b0, pv, wb, cb, plan)
    lam_init = 0.8 - 0.6 * math.exp(-0.3 * layer_idx)
    return _diff_attention(oa, ob, lam, subln_g, lam_init, t)


def kernel(x, c, ada_w, ada_b, norm_g, ffn_w1, ffn_w2, nsa_w_in, nsa_cmp_pe, nsa_cmp_w1, nsa_cmp_w2,
           nsa_w_out, sb_w_in, sb_w_out, diff_w_in, diff_lam, diff_subln_g, diff_w_out):
    bsz, t, d = x.shape
    assert d == D_MODEL and t % ROW_TILE == 0 and t >= WIN + 128 and t <= N_SEL_PAD * SEL_LEN
    mod = _modulation(c, ada_w, ada_b)
    for i in range(DEPTH):
        def mods(sidx):
            return tuple((i, 3 * sidx + k) for k in range(3))

        shift, scl, gate = mods(0)
        x = _ffn(x, norm_g[i, 0], mod, scl, shift, gate, norm_g[i, 1],
                 ffn_w1[i, 0].astype(BF16), ffn_w2[i, 0].astype(BF16))
        shift, scl, gate = mods(1)
        kind, j = i % N_MIXERS, i // N_MIXERS
        if kind == 0:
            o = _nsa_mixer(x, norm_g[i, 2], mod, scl, shift, nsa_w_in[j], nsa_cmp_pe[j], nsa_cmp_w1[j],
                           nsa_cmp_w2[j])
            w_out = nsa_w_out[j]
        elif kind == 1:
            o = _sb_mixer(x, norm_g[i, 2], mod, scl, shift, sb_w_in[j])
            w_out = sb_w_out[j]
        else:
            o = _diff_mixer(x, norm_g[i, 2], mod, scl, shift, diff_w_in[j], diff_lam[j], diff_subln_g[j], i)
            w_out = diff_w_out[j]
        mixer = (o, w_out.astype(BF16), gate, norm_g[i, 3])
        shift, scl, gate = mods(2)
        x = _ffn(x, norm_g[i, 4], mod, scl, shift, gate, norm_g[i, 5],
                 ffn_w1[i, 1].astype(BF16), ffn_w2[i, 1].astype(BF16), mixer=mixer)
    return x
```

```python
import functools
import math

import jax
import jax.numpy as jnp
import numpy as np
from jax import lax
from jax.experimental import pallas as pl
from jax.experimental.pallas import tpu as pltpu

F32 = jnp.float32
BF16 = jnp.bfloat16

D_MODEL = 1024
DEPTH = 4
N_MIXERS = 3
EPS = 1e-6
NEG = -1e30
FORCE = 1e4
LOG2E = 1.4426950408889634
NSA_HEADS = 16
NSA_DK = 64
NSA_KV = 4
NSA_REP = NSA_HEADS // NSA_KV
CMP_LEN = 32
CMP_STRIDE = 16
CMP_HID = 256
SEL_LEN = 64
SEL_TOPK = 16
WIN = 512
NSA_QW = NSA_HEADS * NSA_DK
NSA_KVW = NSA_KV * NSA_DK
N_SEL_PAD = 128
SB_HEADS = 16
SB_DH = D_MODEL // SB_HEADS
DIFF_HEADS = 8
DIFF_DH = D_MODEL // (2 * DIFF_HEADS)
D_FF = 2816

LANES = 128
MOD_ROWS = 8
ROW_TILE = 512
FF_CHUNK = 256
DIFF_TQ, DIFF_TK = 512, 512
SB_TQ, SB_TK = 512, 256
SB_LINEAR_FROM = 64.0
SB_DEAD = 192.0
DIFF_DEAD = 160.0
ALIBI_COLS = 7
GATE_COLS = NSA_REP * 3
NSA_TQ, NSA_TK = 256, 512
CMP_CHUNK = 128
VMEM_LIMIT = 56 * 1024 * 1024


def _cparams(sem):
    return pltpu.CompilerParams(dimension_semantics=sem, vmem_limit_bytes=VMEM_LIMIT)


def _const_spec(shape):
    n = len(shape)
    return pl.BlockSpec(shape, lambda *_: (0,) * n)


def _resident_spec(shape):
    n = len(shape)
    return pl.BlockSpec(shape, lambda *_: (0,) * n, pipeline_mode=pl.Buffered(1))


def _rms(x, g):
    return x * lax.rsqrt(jnp.mean(x * x, axis=-1, keepdims=True) + EPS) * g


def _dot(a, b):
    return jnp.dot(a, b, preferred_element_type=F32)


def _dot_nt(a, b):
    return lax.dot_general(a, b, (((1,), (1,)), ((), ())), preferred_element_type=F32)


def _mod_kernel(c_ref, w_ref, b_ref, o_ref):
    c = c_ref[...]
    cond = c * jax.nn.sigmoid(c)
    o_ref[0] = _dot(cond.astype(BF16), w_ref[0].astype(BF16)) + b_ref[0]


def _modulation(c, ada_w, ada_b):
    depth, d, n = ada_w.shape
    b = c.shape[0]
    rows = MOD_ROWS
    c_pad = jnp.zeros((rows, d), F32).at[:b].set(c)
    nc = 1152
    out = pl.pallas_call(
        _mod_kernel,
        grid=(depth, n // nc),
        in_specs=[
            pl.BlockSpec((rows, d), lambda i, j: (0, 0)),
            pl.BlockSpec((1, d, nc), lambda i, j: (i, 0, j)),
            pl.BlockSpec((1, 1, nc), lambda i, j: (i, 0, j)),
        ],
        out_specs=pl.BlockSpec((1, rows, nc), lambda i, j: (i, 0, j)),
        out_shape=jax.ShapeDtypeStruct((depth, rows, n), F32),
        compiler_params=_cparams(("arbitrary", "arbitrary")),
        name="adaln_mod",
    )(c_pad, ada_w, ada_b.reshape(depth, 1, n))
    return out


def _mod_spec(sel):
    layer, blk = sel
    return pl.BlockSpec((1, MOD_ROWS, D_MODEL), lambda b, i: (layer, 0, blk))


def _mod_row(ref):
    return ref[0, pl.ds(pl.program_id(0), 1), :]


def _inproj_kernel(*refs, tt, a_chunks, aug_blocks, b_chunks, b_aug, has_kv):
    if has_kv:
        (x_ref, g_ref, scl_ref, sh_ref, wa_ref, b0_ref, pv_ref, wb_ref, cb_ref,
         oa_ref, ob_ref, oc_ref, kv_ref) = refs
    else:
        x_ref, g_ref, scl_ref, sh_ref, wa_ref, b0_ref, pv_ref, wb_ref, cb_ref, oa_ref, ob_ref = refs
    x = x_ref[0]
    hh = (_rms(x, g_ref[...]) * (1.0 + _mod_row(scl_ref)) + _mod_row(sh_ref)).astype(BF16)
    t0 = pl.program_id(1) * tt
    pos = (t0 + lax.broadcasted_iota(jnp.int32, (tt, 1), 0)).astype(F32)
    lane = lax.broadcasted_iota(jnp.int32, (1, LANES), 1)

    def aug(k):
        return b0_ref[k:k + 1, :] + pos * pv_ref[k:k + 1, :]

    def put(dst, val):
        oa_ref[0, :, dst * LANES:(dst + 1) * LANES] = val.astype(oa_ref.dtype)

    for lo, width, kind, entries in a_chunks:
        acc = _dot(hh, wa_ref[:, lo:lo + width])
        if kind == "kv":
            for k in range(width // LANES):
                kv_ref[k] = acc[:, k * LANES:(k + 1) * LANES]
                for l in range(CMP_STRIDE):
                    oc_ref[0, l, :, k * LANES:(k + 1) * LANES] = kv_ref[
                        k, pl.ds(l, tt // CMP_STRIDE, stride=CMP_STRIDE), :].astype(oc_ref.dtype)
            continue
        for e in entries:
            blk = acc[:, e[1] * LANES:(e[1] + 1) * LANES]
            if e[0] == "mm":
                put(e[2], blk * e[3] if e[3] != 1.0 else blk)
            elif e[0] == "half":
                if e[3] != 1.0:
                    blk = blk * e[3]
                if e[4]:
                    blk = pltpu.roll(blk, NSA_DK, axis=1)
                fill = aug(e[5]) if e[5] >= 0 else jnp.where(lane == NSA_DK, 1.0, 0.0)
                put(e[2], jnp.where(lane < NSA_DK, blk, fill))
            else:
                if e[3]:
                    blk = pltpu.roll(blk, LANES - e[3], axis=1)
                put(e[2], jnp.where(lane < GATE_COLS, blk, 0.0))
    for dst, k in aug_blocks:
        put(dst, aug(k))

    tpos = t0 + lax.broadcasted_iota(jnp.int32, (1, tt), 1)
    phi = ((tpos >> 7) << 7).astype(F32)
    plo = (tpos & 127).astype(F32)
    if b_aug:
        cc = cb_ref[...]
        key_rows = (cc[:, 0:1] * phi + cc[:, 1:2] * plo + cc[:, 2:3]).astype(ob_ref.dtype)
        for dst, nrows in b_aug:
            ob_ref[0, dst:dst + nrows, :] = key_rows[0:nrows]
    for lo, rows, pieces in b_chunks:
        acc = _dot_nt(wb_ref[lo:lo + rows, :], hh)
        for src, nrows, dst in pieces:
            ob_ref[0, dst:dst + nrows, :] = acc[src:src + nrows].astype(ob_ref.dtype)


def _inproj(x, g, mod, scl, shift, wa, b0, pv, wb, cb, plan):
    bsz, t, d = x.shape
    tt = min(ROW_TILE, t)
    a_chunks, aug_blocks, n_blocks, b_chunks, b_aug, n_rows, has_kv = plan
    na = n_blocks * LANES
    in_specs = [
        pl.BlockSpec((1, tt, d), lambda b, i: (b, i, 0)),
        _const_spec((1, d)),
        _mod_spec(scl),
        _mod_spec(shift),
        _const_spec(wa.shape),
        _const_spec(b0.shape),
        _const_spec(pv.shape),
        _const_spec(wb.shape),
        _const_spec(cb.shape),
    ]
    out_specs = [pl.BlockSpec((1, tt, na), lambda b, i: (b, i, 0)),
                 pl.BlockSpec((1, n_rows, tt), lambda b, i: (b, 0, i))]
    out_shape = [jax.ShapeDtypeStruct((bsz, t, na), BF16), jax.ShapeDtypeStruct((bsz, n_rows, t), BF16)]
    scratch = []
    if has_kv:
        kvw = 2 * NSA_KVW
        out_specs.append(pl.BlockSpec((1, CMP_STRIDE, tt // CMP_STRIDE, kvw), lambda b, i: (b, 0, i, 0)))
        out_shape.append(jax.ShapeDtypeStruct((bsz, CMP_STRIDE, t // CMP_STRIDE, kvw), BF16))
        scratch.append(pltpu.VMEM((kvw // LANES, tt, LANES), F32))
    return pl.pallas_call(
        functools.partial(_inproj_kernel, tt=tt, a_chunks=a_chunks, aug_blocks=aug_blocks, b_chunks=b_chunks,
                          b_aug=b_aug, has_kv=has_kv),
        grid=(bsz, t // tt),
        in_specs=in_specs,
        out_specs=out_specs,
        out_shape=out_shape,
        scratch_shapes=scratch,
        compiler_params=_cparams(("arbitrary", "arbitrary")),
        name="mixer_inproj",
    )(x, g.reshape(1, d), mod, mod, wa, b0, pv, wb, cb)


FFN_REFS = 7


def _ffn_kernel(*refs, after_mixer, n_ffn):
    if after_mixer:
        mo_ref, wo_ref, mgate_ref, gm_ref = refs[:4]
        refs = refs[4:]
    x_ref, o_ref, acc_ref = refs[0], refs[-2], refs[-1]
    x = x_ref[0]
    if after_mixer:
        x = x + _mod_row(mgate_ref) * _rms(_dot(mo_ref[0], wo_ref[...]), gm_ref[...])
    for k in range(n_ffn):
        g1_ref, scl_ref, sh_ref, gate_ref, g2_ref, w1_ref, w2_ref = refs[1 + FFN_REFS * k:1 + FFN_REFS * (k + 1)]
        hh = (_rms(x, g1_ref[...]) * (1.0 + _mod_row(scl_ref)) + _mod_row(sh_ref)).astype(BF16)
        dff = w2_ref.shape[0]
        for c in range(dff // FF_CHUNK):
            lo = c * FF_CHUNK
            gt = _dot(hh, w1_ref[:, lo:lo + FF_CHUNK])
            up = _dot(hh, w1_ref[:, dff + lo:dff + lo + FF_CHUNK])
            act = (gt * jax.nn.sigmoid(gt) * up).astype(BF16)
            part = _dot(act, w2_ref[lo:lo + FF_CHUNK, :])
            if c == 0:
                acc_ref[...] = part
            else:
                acc_ref[...] += part
        x = x + 0.5 * _mod_row(gate_ref) * _rms(acc_ref[...], g2_ref[...])
    o_ref[0] = x


def _ffn(x, mod, blocks, mixer=None):
    bsz, t, d = x.shape
    tt = min(ROW_TILE, t)
    wspec = _const_spec if len(blocks) == 1 else _resident_spec
    in_specs = [pl.BlockSpec((1, tt, d), lambda b, i: (b, i, 0))]
    args = [x]
    for g1, scl, shift, gate, g2, w1, w2 in blocks:
        in_specs += [_const_spec((1, d)), _mod_spec(scl), _mod_spec(shift), _mod_spec(gate), _const_spec((1, d)),
                     wspec(w1.shape), wspec(w2.shape)]
        args += [g1.reshape(1, d), mod, mod, mod, g2.reshape(1, d), w1, w2]
    if mixer is not None:
        mo, wo, mgate, gm = mixer
        in_specs = [pl.BlockSpec((1, tt, mo.shape[2]), lambda b, i: (b, i, 0)), _const_spec(wo.shape),
                    _mod_spec(mgate), _const_spec((1, d))] + in_specs
        args = [mo, wo, mod, gm.reshape(1, d)] + args
    return pl.pallas_call(
        functools.partial(_ffn_kernel, after_mixer=mixer is not None, n_ffn=len(blocks)),
        grid=(bsz, t // tt),
        in_specs=in_specs,
        out_specs=pl.BlockSpec((1, tt, d), lambda b, i: (b, i, 0)),
        out_shape=jax.ShapeDtypeStruct((bsz, t, d), F32),
        scratch_shapes=[pltpu.VMEM((tt, d), F32)],
        compiler_params=_cparams(("arbitrary", "arbitrary")),
        name="ffn",
    )(*args)


def _online_step(s, m, acc, v):
    m_new = jnp.maximum(m, jnp.max(s, axis=-1, keepdims=True))
    alpha = jnp.exp2(m - m_new)
    return m_new, alpha * acc + _dot(jnp.exp2(s - m_new).astype(BF16), v)


def _softmax_init(rows, width):
    return jnp.full((rows, 1), NEG, F32), jnp.zeros((rows, width), F32)


def _online_step_sum(s, m, l, acc, v):
    m_new = jnp.maximum(m, jnp.max(s, axis=-1, keepdims=True))
    alpha = jnp.exp2(m - m_new)
    p = jnp.exp2(s - m_new)
    return m_new, alpha * l + jnp.sum(p, axis=-1, keepdims=True), alpha * acc + _dot(p.astype(BF16), v)


def _gelu_tanh(x):
    return 0.5 * x * (1.0 + jnp.tanh(math.sqrt(2.0 / math.pi) * (x + 0.044715 * (x * x * x))))


def _cmp_kernel(kv_ref, pe_ref, w1_ref, w2kt_ref, w2v_ref, kt_ref, v_ref):
    nc = kv_ref.shape[2]

    def hidden(j, g):
        lo = j * NSA_KVW + g * NSA_DK
        top = bot = None
        for l in range(CMP_STRIDE):
            x = kv_ref[0, l, :, lo:lo + NSA_DK].astype(F32)
            hi = CMP_STRIDE + l
            a = _dot((x + pe_ref[j, l:l + 1, :]).astype(BF16), w1_ref[j, l * NSA_DK:(l + 1) * NSA_DK, :])
            b = _dot((x + pe_ref[j, hi:hi + 1, :]).astype(BF16), w1_ref[j, hi * NSA_DK:(hi + 1) * NSA_DK, :])
            top = a if top is None else top + a
            bot = b if bot is None else bot + b
        return _gelu_tanh(top + pltpu.roll(bot, nc - 1, axis=0)).astype(BF16)

    n = lax.broadcasted_iota(jnp.int32, (1, nc), 1)
    end = n * CMP_STRIDE + (CMP_LEN - 1)
    phi = ((end >> 7) << 7).astype(F32)
    plo = (end & 127).astype(F32)
    r = lax.broadcasted_iota(jnp.int32, (LANES - NSA_DK, 1), 0)
    aug = jnp.where(r < 3, phi, jnp.where(r < 6, plo, jnp.where(r == 6, 1.0, 0.0))).astype(BF16)
    for g in range(NSA_KV):
        kt_ref[0, g, 0:NSA_DK, :] = _dot_nt(w2kt_ref[...], hidden(0, g)).astype(BF16)
        kt_ref[0, g, NSA_DK:LANES, :] = aug
        v_ref[0, g] = _dot(hidden(1, g), w2v_ref[...]).astype(BF16)


def _nsa_compress(kv, pe, w1, w2kt, w2v):
    bsz, _, nc, kvw = kv.shape
    return pl.pallas_call(
        _cmp_kernel,
        grid=(bsz,),
        in_specs=[pl.BlockSpec((1, CMP_STRIDE, nc, kvw), lambda b: (b, 0, 0, 0)), _const_spec(pe.shape),
                  _const_spec(w1.shape), _const_spec(w2kt.shape), _const_spec(w2v.shape)],
        out_specs=[pl.BlockSpec((1, NSA_KV, LANES, nc), lambda b: (b, 0, 0, 0)),
                   pl.BlockSpec((1, NSA_KV, nc, LANES), lambda b: (b, 0, 0, 0))],
        out_shape=[jax.ShapeDtypeStruct((bsz, NSA_KV, LANES, nc), BF16),
                   jax.ShapeDtypeStruct((bsz, NSA_KV, nc, LANES), BF16)],
        compiler_params=_cparams(("arbitrary",)),
        name="nsa_compress",
    )(kv, pe, w1, w2kt, w2v)


def _normalized(acc):
    lane = lax.broadcasted_iota(jnp.int32, (1, LANES), 1)
    return jnp.where(lane < NSA_DK, acc / acc[:, NSA_DK:NSA_DK + 1], 0.0)


def _split3(x):
    hi = x.astype(BF16)
    r1 = x - hi.astype(F32)
    mid = r1.astype(BF16)
    lo = (r1 - mid.astype(F32)).astype(BF16)
    return hi, mid, lo


def _nsa_attn_kernel(qa_ref, kct_ref, vc_ref, kst_ref, vs_ref, kwt_ref, vw_ref, gt_ref, e_ref, ov_ref,
                     o_ref, tiles_ref, *, tq, tk):
    i = pl.program_id(2)
    q0 = i * tq
    rows = NSA_REP * tq
    qa = qa_ref[0]
    qs = jnp.concatenate([qa[:, r * LANES:(r + 1) * LANES] for r in range(NSA_REP)], axis=0)
    t_row = q0 + (lax.broadcasted_iota(jnp.int32, (rows, 1), 0) & (tq - 1))

    nc = kct_ref.shape[3]

    def compressed(width):
        s = _dot(qs, kct_ref[0, 0, :, 0:width])
        cend = lax.broadcasted_iota(jnp.int32, (1, width), 1) * CMP_STRIDE + (CMP_LEN - 1)
        valid = cend <= t_row
        s = jnp.where(valid, s, NEG)
        e = jnp.where(valid, jnp.exp2(s - jnp.max(s, axis=-1, keepdims=True)), 0.0)
        l = jnp.sum(e, axis=-1, keepdims=True)
        p = e / jnp.where(l > 0.0, l, 1.0)
        o = _dot(p.astype(BF16), vc_ref[0, 0, 0:width, :])
        p4 = p[0:tq]
        for r in range(1, NSA_REP):
            p4 = p4 + p[r * tq:(r + 1) * tq]
        ov = ov_ref[:, 0:width]
        return o, sum(_dot_nt(ov, part) for part in _split3(p4))

    chunk = min(CMP_CHUNK, nc)
    widths = list(range(chunk, nc + 1, chunk))
    n_ended = jnp.maximum(q0 + tq - CMP_LEN, 0) // CMP_STRIDE + 1
    o_cmp, imp = lax.switch((n_ended - 1) // chunk, [functools.partial(compressed, w) for w in widths])

    wk = WIN + tq
    st = pl.multiple_of(jnp.maximum(q0 - WIN, 0), LANES)
    col = st + lax.broadcasted_iota(jnp.int32, (1, wk), 1)
    valid = (t_row - col).astype(jnp.uint32) < WIN
    s = jnp.where(valid, _dot(qs, kwt_ref[0, :, pl.ds(st, wk)]), NEG)
    e = jnp.exp2(s - jnp.max(s, axis=-1, keepdims=True))
    o_win = _normalized(_dot(e.astype(BF16), vw_ref[0, pl.ds(st, wk), :]))

    cur = (q0 + lax.broadcasted_iota(jnp.int32, (1, tq), 1)) >> 6
    sid = lax.broadcasted_iota(jnp.int32, (N_SEL_PAD, 1), 0)
    forced = (sid == 0) | (sid == cur) | (sid == cur - 1)
    work = jnp.where(forced | (sid > cur), -3.0e38, imp)
    selb = jnp.where(forced, 0.0, NEG)
    for _ in range(SEL_TOPK - 3):
        mx = jnp.max(work, axis=0, keepdims=True)
        first = jnp.min(jnp.where(work == mx, sid, N_SEL_PAD), axis=0, keepdims=True)
        pick = sid == first
        selb = jnp.where(pick, 0.0, selb)
        work = jnp.where(pick, -3.0e38, work)
    selb = selb.T

    jd = q0 // tk
    n_tiles = kst_ref.shape[2] // tk
    any_sel = jnp.max(jnp.where(selb == 0.0, 1.0, 0.0), axis=0, keepdims=True)
    blk_tile = (lax.broadcasted_iota(jnp.int32, (N_SEL_PAD, LANES), 0) // (tk // SEL_LEN)
                == lax.broadcasted_iota(jnp.int32, (N_SEL_PAD, LANES), 1)).astype(BF16)
    tile_cnt = _dot(jnp.broadcast_to(any_sel, (16, N_SEL_PAD)).astype(BF16), blk_tile)
    n_act = jnp.int32(0)
    for j in range(n_tiles):
        tiles_ref[n_act] = j
        n_act = n_act + ((tile_cnt[0, j] > 0.0) & (j < jd)).astype(jnp.int32)

    selb4 = jnp.concatenate([selb.astype(BF16)] * NSA_REP, axis=0)
    qsel = jnp.concatenate([qs, selb4], axis=1)

    def sel_scores(st, width):
        return _dot(qsel, jnp.concatenate([kst_ref[0, :, pl.ds(st, width)], e_ref[:, pl.ds(st, width)]], axis=0))

    def sel_body(it, carry):
        st = pl.multiple_of(tiles_ref[it] * tk, tk)
        return _online_step(sel_scores(st, tk), *carry, vs_ref[0, pl.ds(st, tk), :])

    carry = lax.fori_loop(0, n_act, sel_body, _softmax_init(rows, LANES))
    st = pl.multiple_of(jd * tk, tk)

    def diagonal(width):
        col = st + lax.broadcasted_iota(jnp.int32, (1, width), 1)
        s = jnp.where(col <= t_row, sel_scores(st, width), NEG)
        return _normalized(_online_step(s, *carry, vs_ref[0, pl.ds(st, width), :])[1])

    o_sel = lax.switch((q0 - st) // tq, [functools.partial(diagonal, w) for w in range(tq, tk + 1, tq)])

    gs = jax.nn.sigmoid(gt_ref[0].astype(F32))
    outs = []
    for r in range(NSA_REP):
        sl = slice(r * tq, (r + 1) * tq)
        outs.append(gs[:, 3 * r:3 * r + 1] * o_cmp[sl] + gs[:, 3 * r + 1:3 * r + 2] * o_sel[sl]
                    + gs[:, 3 * r + 2:3 * r + 3] * o_win[sl])
    pairs = [outs[2 * a] + pltpu.roll(outs[2 * a + 1], NSA_DK, axis=1) for a in range(NSA_REP // 2)]
    o_ref[0] = jnp.concatenate(pairs, axis=1).astype(o_ref.dtype)


def _nsa_attention(oa, ob, kct, vcp, e_mat, ov_mat, t):
    bsz = oa.shape[0]
    tq, tk = NSA_TQ, NSA_TK
    qw = NSA_REP * LANES
    vs_blk = NSA_HEADS
    vw_blk = vs_blk + NSA_KV
    gt_blk = vw_blk + NSA_KV
    nc = kct.shape[3]
    return pl.pallas_call(
        functools.partial(_nsa_attn_kernel, tq=tq, tk=tk),
        grid=(bsz, NSA_KV, t // tq),
        in_specs=[
            pl.BlockSpec((1, tq, qw), lambda b, g, i: (b, i, g)),
            pl.BlockSpec((1, 1, LANES, nc), lambda b, g, i: (b, g, 0, 0)),
            pl.BlockSpec((1, 1, nc, LANES), lambda b, g, i: (b, g, 0, 0)),
            pl.BlockSpec((1, LANES, t), lambda b, g, i: (b, g, 0)),
            pl.BlockSpec((1, t, LANES), lambda b, g, i: (b, 0, vs_blk + g)),
            pl.BlockSpec((1, LANES, t), lambda b, g, i: (b, NSA_KV + g, 0)),
            pl.BlockSpec((1, t, LANES), lambda b, g, i: (b, 0, vw_blk + g)),
            pl.BlockSpec((1, tq, LANES), lambda b, g, i: (b, i, gt_blk + g)),
            _const_spec(e_mat.shape),
            _const_spec(ov_mat.shape),
        ],
        out_specs=pl.BlockSpec((1, tq, NSA_REP * NSA_DK), lambda b, g, i: (b, i, g)),
        out_shape=jax.ShapeDtypeStruct((bsz, t, NSA_QW), BF16),
        scratch_shapes=[pltpu.SMEM((t // tk,), jnp.int32)],
        compiler_params=_cparams(("arbitrary", "arbitrary", "arbitrary")),
        name="nsa_attention",
    )(oa, kct, vcp, ob, oa, ob, oa, oa, e_mat, ov_mat)


def _sb_kernel(q_ref, kt_ref, v_ref, o_ref, rem_ref, acc_ref, *, tq, tk):
    i = pl.program_id(2)
    q0 = i * tq
    q = q_ref[0]
    lane = lax.broadcasted_iota(jnp.int32, (1, LANES), 1)
    zero = jnp.zeros_like(q)
    qs = jnp.concatenate([jnp.where(lane < SB_DH, q, zero), jnp.where(lane >= SB_DH, q, zero)], axis=0)
    t_row = q0 + (lax.broadcasted_iota(jnp.int32, (2 * tq, 1), 0) & (tq - 1))
    upper = (lax.broadcasted_iota(jnp.int32, (tk, tk), 0) > lax.broadcasted_iota(jnp.int32, (tk, tk), 1)
             ).astype(BF16)
    rem_ref[...] = jnp.zeros(rem_ref.shape, F32)
    acc_ref[...] = jnp.zeros(acc_ref.shape, F32)

    def tiles(js, masked):
        sts = [pl.multiple_of(j * tk, tk) for j in js]
        zs = [_dot(qs, kt_ref[0, :, pl.ds(st, tk)]) for st in sts]
        costs, log_bs, befores = [], [], []
        for st, z in zip(sts, zs):
            cost = jnp.where(z > SB_LINEAR_FROM, z, jnp.log(1.0 + jnp.exp2(z)) * LOG2E)
            log_bs.append(z - cost)
            before = None
            if masked:
                before = (st + lax.broadcasted_iota(jnp.int32, (1, tk), 1)) < t_row
                cost = jnp.where(before, cost, 0.0)
            costs.append(cost.astype(BF16))
            befores.append(before)
        rems = [_dot(c, upper) for c in costs]
        rem_right = rem_ref[...]
        probs = []
        for log_b, c, rem, before in zip(log_bs, costs, rems, befores):
            ex = log_b - (rem + rem_right)
            if masked:
                ex = jnp.where(before, ex, NEG)
            probs.append(jnp.exp2(ex).astype(BF16))
            rem_right = rem_right + (rem[:, 0:1] + c[:, 0:1].astype(F32))
        rem_ref[...] = rem_right
        acc = acc_ref[...]
        for st, p in zip(sts, probs):
            acc = acc + _dot(p, v_ref[0, pl.ds(st, tk), :])
        acc_ref[...] = acc
        return jnp.min(rem_right)

    per_step = tq // tk
    n_full = q0 // tk
    spent = tiles([n_full + per_step - 1 - d for d in range(per_step)], True)

    def go_on(c):
        return (c[0] < n_full // per_step) & (c[1] < SB_DEAD)

    def body(c):
        it = c[0]
        return it + 1, tiles([n_full - 1 - per_step * it - d for d in range(per_step)], False)

    lax.while_loop(go_on, body, (jnp.int32(0), spent))
    acc = acc_ref[...]
    o_ref[0] = jnp.where(lane < SB_DH, acc[0:tq], acc[tq:2 * tq]).astype(o_ref.dtype)


def _sb_attention(oa, ob, t):
    bsz = oa.shape[0]
    tq, tk = SB_TQ, SB_TK
    npair = SB_HEADS // 2
    return pl.pallas_call(
        functools.partial(_sb_kernel, tq=tq, tk=tk),
        grid=(bsz, npair, t // tq),
        in_specs=[
            pl.BlockSpec((1, tq, LANES), lambda b, h, i: (b, i, h)),
            pl.BlockSpec((1, LANES, t), lambda b, h, i: (b, h, 0)),
            pl.BlockSpec((1, t, LANES), lambda b, h, i: (b, 0, npair + h)),
        ],
        out_specs=pl.BlockSpec((1, tq, LANES), lambda b, h, i: (b, i, h)),
        out_shape=jax.ShapeDtypeStruct((bsz, t, D_MODEL), BF16),
        scratch_shapes=[pltpu.VMEM((2 * tq, 1), F32), pltpu.VMEM((2 * tq, LANES), F32)],
        compiler_params=_cparams(("arbitrary", "arbitrary", "arbitrary")),
        name="sb_attention",
    )(oa, ob, oa)


def _diff_kernel(slope_ref, lam_ref, g_ref, qa_ref, kt_ref, v_ref, o_ref, kmax_ref, *, tq, tk, lam_init):
    h = pl.program_id(1)
    i = pl.program_id(2)
    q0 = i * tq
    n_tiles = kt_ref.shape[2] // tk

    @pl.when(i == 0)
    def _():
        run = jnp.float32(0.0)
        for j in range(n_tiles):
            kf = kt_ref[0, 0:LANES, j * tk:(j + 1) * tk].astype(F32)
            sq = kf * kf
            norms = jnp.sqrt(jnp.maximum(jnp.sum(sq[0:DIFF_DH], axis=0, keepdims=True),
                                         jnp.sum(sq[DIFF_DH:LANES], axis=0, keepdims=True)))
            run = jnp.maximum(run, jnp.max(norms))
            kmax_ref[j] = run

    qa = qa_ref[0]
    lane = lax.broadcasted_iota(jnp.int32, (1, 2 * LANES), 1)
    zero = jnp.zeros_like(qa)
    qs = jnp.concatenate(
        [jnp.where(((lane >= m_ * DIFF_DH) & (lane < (m_ + 1) * DIFF_DH)) | (lane >= LANES), qa, zero)
         for m_ in range(2)], axis=0)
    t_row = q0 + (lax.broadcasted_iota(jnp.int32, (2 * tq, 1), 0) & (tq - 1))

    def scores(j):
        return _dot(qs, kt_ref[0, :, pl.ds(pl.multiple_of(j * tk, tk), tk)])

    def values(j):
        return v_ref[0, pl.ds(pl.multiple_of(j * tk, tk), tk), :]

    jd = q0 // tk
    col = jd * tk + lax.broadcasted_iota(jnp.int32, (1, tk), 1)
    m0, acc0 = _softmax_init(2 * tq, LANES)
    state = _online_step_sum(jnp.where(col <= t_row, scores(jd), NEG), m0, jnp.zeros_like(m0), acc0, values(jd))
    qf = jnp.where(lane < LANES, qs, jnp.zeros_like(qs)).astype(F32)
    q_norm = jnp.max(jnp.sqrt(jnp.sum(qf * qf, axis=-1, keepdims=True)))
    row_col = 2 * DIFF_DH + ALIBI_COLS - 1
    row_const = qs[:, row_col:row_col + 1].astype(F32)
    slope = slope_ref[h]

    floor = jnp.min(state[0] - row_const)

    def live(c):
        j = jnp.maximum(c[0], 0)
        bound = q_norm * kmax_ref[j] * 1.001 + slope * ((j + 1) * tk - 1).astype(F32)
        return (c[0] >= 0) & (bound - floor > -DIFF_DEAD)

    def body(c):
        return (c[0] - 1, *_online_step_sum(scores(c[0]), c[1], c[2], c[3], values(c[0])))

    _, _, l, acc = lax.while_loop(live, body, (jd - 1, *state))
    out = acc / l
    lam = lam_ref[...]
    lam_full = (jnp.exp(jnp.sum(lam[0:1] * lam[1:2], axis=-1, keepdims=True))
                - jnp.exp(jnp.sum(lam[2:3] * lam[3:4], axis=-1, keepdims=True)) + lam_init)
    o = out[0:tq] - lam_full * out[tq:2 * tq]
    o_ref[0] = (_rms(o, g_ref[...]) * (1.0 - lam_init)).astype(o_ref.dtype)


def _diff_attention(oa, ob, lam, subln_g, lam_init, t):
    bsz = oa.shape[0]
    tq, tk = DIFF_TQ, DIFF_TK
    slopes = _alibi_slopes(DIFF_HEADS) * LOG2E
    return pl.pallas_call(
        functools.partial(_diff_kernel, tq=tq, tk=tk, lam_init=lam_init),
        grid=(bsz, DIFF_HEADS, t // tq),
        in_specs=[
            pl.BlockSpec(memory_space=pltpu.SMEM),
            _const_spec(lam.shape),
            _const_spec((1, 2 * DIFF_DH)),
            pl.BlockSpec((1, tq, 2 * LANES), lambda b, h, i: (b, i, h)),
            pl.BlockSpec((1, 2 * LANES, t), lambda b, h, i: (b, h, 0)),
            pl.BlockSpec((1, t, LANES), lambda b, h, i: (b, 0, 2 * DIFF_HEADS + h)),
        ],
        out_specs=pl.BlockSpec((1, tq, LANES), lambda b, h, i: (b, i, h)),
        out_shape=jax.ShapeDtypeStruct((bsz, t, D_MODEL), BF16),
        scratch_shapes=[pltpu.SMEM((t // tk,), F32)],
        compiler_params=_cparams(("arbitrary", "arbitrary", "arbitrary")),
        name="diff_attention",
    )(slopes, lam, subln_g.reshape(1, 2 * DIFF_DH), oa, ob, oa)


def _alibi_slopes(n):
    return jnp.exp2(-8.0 * jnp.arange(1, n + 1, dtype=F32) / n)


def _alibi_query_cols(slopes, first):
    sl2 = slopes * LOG2E
    parts = jnp.stack([p.astype(F32) for p in _split3(sl2)], axis=1)
    b0 = jnp.pad(jnp.concatenate([parts, parts], axis=1), ((0, 0), (first, LANES - first - 6)))
    pv = jnp.pad(-sl2[:, None], ((0, 0), (first + 6, LANES - first - ALIBI_COLS)))
    return b0, pv


def _alibi_key_rows():
    c = np.zeros((LANES, 8), np.float32)
    c[0:3, 0] = 1.0
    c[3:6, 1] = 1.0
    c[6, 2] = 1.0
    return jnp.asarray(c)


def _nsa_weights(w_in):
    bounds = [0, NSA_QW] + [NSA_QW + (k + 1) * NSA_KVW for k in range(6)]
    wq, wkc, wvc, wks, wvs, wkw, wvw = [w_in[:, bounds[k]:bounds[k + 1]] for k in range(7)]
    wg = jnp.pad(w_in[:, bounds[7]:], ((0, 0), (0, LANES - NSA_KV * GATE_COLS)))
    wa = jnp.concatenate([wq, wkc, wvc, wvs, wvw, wg], axis=1).astype(BF16)
    scale = NSA_DK ** -0.5 * LOG2E
    per = ROW_TILE // LANES
    qblocks = NSA_QW // LANES
    a_chunks = []
    for c in range(qblocks // per):
        ents = tuple(("half", b, 2 * (c * per + b) + hi, scale, hi, 2 * (c * per + b) + hi)
                     for b in range(per) for hi in range(2))
        a_chunks.append((c * ROW_TILE, ROW_TILE, "mm", ents))
    a_chunks.append((NSA_QW, 2 * NSA_KVW, "kv", ()))
    v0 = NSA_QW + 2 * NSA_KVW
    vs_blk, vw_blk, gt_blk = NSA_HEADS, NSA_HEADS + NSA_KV, NSA_HEADS + 2 * NSA_KV
    ents = tuple(("half", g // 2, vs_blk + g, 1.0, g % 2, -1) for g in range(NSA_KV)) + tuple(
        ("half", NSA_KV // 2 + g // 2, vw_blk + g, 1.0, g % 2, -1) for g in range(NSA_KV))
    a_chunks.append((v0, 2 * NSA_KVW, "mm", ents))
    a_chunks.append((v0 + 2 * NSA_KVW, LANES, "mm",
                     tuple(("gate", 0, gt_blk + g, GATE_COLS * g) for g in range(NSA_KV))))
    b0, pv = _alibi_query_cols(_alibi_slopes(NSA_HEADS), NSA_DK)
    wb = jnp.concatenate([wks, wkw], axis=1).T.astype(BF16)
    pieces = tuple((NSA_DK * g, NSA_DK, LANES * g) for g in range(2 * NSA_KV))
    b_aug = tuple((LANES * g + NSA_DK, LANES - NSA_DK) for g in range(2 * NSA_KV))
    plan = (tuple(a_chunks), (), gt_blk + NSA_KV, ((0, 2 * NSA_KVW, pieces),), b_aug, 2 * NSA_KV * LANES, True)
    return wa, b0, pv, wb, _alibi_key_rows(), plan


def _sb_weights(w_in):
    d = w_in.shape[0]
    wq, wk, wv = jnp.split(w_in, 3, axis=1)
    wa = jnp.concatenate([wq, wv], axis=1).astype(BF16)
    per = ROW_TILE // LANES
    nq = d // ROW_TILE
    a_chunks = tuple((c * ROW_TILE, ROW_TILE, "mm",
                      tuple(("mm", b, c * per + b, SB_DH ** -0.5 * LOG2E if c < nq else 1.0) for b in range(per)))
                     for c in range(2 * nq))
    b_chunks = tuple((c * ROW_TILE, ROW_TILE, ((0, ROW_TILE, c * ROW_TILE),)) for c in range(nq))
    zero = jnp.asarray(np.zeros((1, LANES), np.float32))
    plan = (a_chunks, (), 2 * d // LANES, b_chunks, (), d, False)
    return wa, zero, zero, wk.T.astype(BF16), jnp.asarray(np.zeros((LANES, 8), np.float32)), plan


def _diff_weights(w_in):
    d = w_in.shape[0]
    wq, wk, wv = jnp.split(w_in, 3, axis=1)
    wa = jnp.concatenate([wq, wv], axis=1).astype(BF16)
    scale = DIFF_DH ** -0.5 * LOG2E
    per = ROW_TILE // LANES
    nq = d // ROW_TILE
    a_chunks = tuple(
        (c * ROW_TILE, ROW_TILE, "mm",
         tuple(("mm", b, 2 * (c * per + b), scale) if c < nq else ("mm", b, 2 * DIFF_HEADS + (c - nq) * per + b, 1.0)
               for b in range(per)))
        for c in range(2 * nq))
    aug_blocks = tuple((2 * h + 1, h) for h in range(DIFF_HEADS))
    b0, pv = _alibi_query_cols(_alibi_slopes(DIFF_HEADS), 0)
    b_chunks = tuple((c * ROW_TILE, ROW_TILE,
                      tuple((LANES * k, LANES, 2 * LANES * (c * per + k)) for k in range(per)))
                     for c in range(nq))
    b_aug = tuple((2 * LANES * h + LANES, LANES) for h in range(DIFF_HEADS))
    plan = (a_chunks, aug_blocks, 3 * DIFF_HEADS, b_chunks, b_aug, 2 * DIFF_HEADS * LANES, False)
    return wa, b0, pv, wk.T.astype(BF16), _alibi_key_rows(), plan


def _nsa_mixer(x, g, mod, scl, shift, w_in, cmp_pe, cmp_w1, cmp_w2):
    t = x.shape[1]
    wa, b0, pv, wb, cb, plan = _nsa_weights(w_in)
    oa, ob, kv = _inproj(x, g, mod, scl, shift, wa, b0, pv, wb, cb, plan)
    nc = t // CMP_STRIDE
    w2kt = cmp_w2[0].T.astype(BF16)
    w2v = jnp.pad(cmp_w2[1], ((0, 0), (0, LANES - NSA_DK))).astype(BF16)
    kct, vcp = _nsa_compress(kv, cmp_pe, cmp_w1.astype(BF16), w2kt, w2v)
    tok = np.arange(t)
    e_mat = jnp.asarray(np.arange(N_SEL_PAD)[:, None] == (tok // SEL_LEN)[None, :], BF16)
    cstart = np.arange(nc) * CMP_STRIDE
    sstart = np.arange(N_SEL_PAD) * SEL_LEN
    ov_mat = jnp.asarray((cstart[None, :] < sstart[:, None] + SEL_LEN)
                         & (cstart[None, :] + CMP_LEN > sstart[:, None])
                         & (np.arange(nc)[None, :] < nc - 1), BF16)
    return _nsa_attention(oa, ob, kct, vcp, e_mat, ov_mat, t)


def _sb_mixer(x, g, mod, scl, shift, w_in):
    t = x.shape[1]
    wa, b0, pv, wb, cb, plan = _sb_weights(w_in)
    oa, ob = _inproj(x, g, mod, scl, shift, wa, b0, pv, wb, cb, plan)
    return _sb_attention(oa, ob, t)


def _diff_mixer(x, g, mod, scl, shift, w_in, lam, subln_g, layer_idx):
    t = x.shape[1]
    wa, b0, pv, wb, cb, plan = _diff_weights(w_in)
    oa, ob = _inproj(x, g, mod, scl, shift, wa, b0, pv, wb, cb, plan)
    lam_init = 0.8 - 0.6 * math.exp(-0.3 * layer_idx)
    return _diff_attention(oa, ob, lam, subln_g, lam_init, t)


def kernel(x, c, ada_w, ada_b, norm_g, ffn_w1, ffn_w2, nsa_w_in, nsa_cmp_pe, nsa_cmp_w1, nsa_cmp_w2,
           nsa_w_out, sb_w_in, sb_w_out, diff_w_in, diff_lam, diff_subln_g, diff_w_out):
    bsz, t, d = x.shape
    assert d == D_MODEL and t % ROW_TILE == 0 and t >= WIN + 128 and t <= N_SEL_PAD * SEL_LEN
    mod = _modulation(c, ada_w, ada_b)
    def ffn_block(i, which):
        shift, scl, gate = ((i, 6 * which + k) for k in range(3))
        return (norm_g[i, 4 * which], scl, shift, gate, norm_g[i, 4 * which + 1],
                ffn_w1[i, which].astype(BF16), ffn_w2[i, which].astype(BF16))

    x = _ffn(x, mod, [ffn_block(0, 0)])
    for i in range(DEPTH):
        def mods(sidx):
            return tuple((i, 3 * sidx + k) for k in range(3))

        shift, scl, gate = mods(1)
        kind, j = i % N_MIXERS, i // N_MIXERS
        if kind == 0:
            o = _nsa_mixer(x, norm_g[i, 2], mod, scl, shift, nsa_w_in[j], nsa_cmp_pe[j], nsa_cmp_w1[j],
                           nsa_cmp_w2[j])
            w_out = nsa_w_out[j]
        elif kind == 1:
            o = _sb_mixer(x, norm_g[i, 2], mod, scl, shift, sb_w_in[j])
            w_out = sb_w_out[j]
        else:
            o = _diff_mixer(x, norm_g[i, 2], mod, scl, shift, diff_w_in[j], diff_lam[j], diff_subln_g[j], i)
            w_out = diff_w_out[j]
        blocks = [ffn_block(i, 1)] + ([ffn_block(i + 1, 0)] if i + 1 < DEPTH else [])
        x = _ffn(x, mod, blocks, mixer=(o, w_out.astype(BF16), gate, norm_g[i, 3]))
    return x
```

```python
import functools
import math

import jax
import jax.numpy as jnp
import numpy as np
from jax import lax
from jax.experimental import pallas as pl
from jax.experimental.pallas import tpu as pltpu

F32 = jnp.float32
BF16 = jnp.bfloat16

D_MODEL = 1024
DEPTH = 4
N_MIXERS = 3
EPS = 1e-6
NEG = -1e30
FORCE = 1e4
LOG2E = 1.4426950408889634
NSA_HEADS = 16
NSA_DK = 64
NSA_KV = 4
NSA_REP = NSA_HEADS // NSA_KV
CMP_LEN = 32
CMP_STRIDE = 16
CMP_HID = 256
SEL_LEN = 64
SEL_TOPK = 16
WIN = 512
NSA_QW = NSA_HEADS * NSA_DK
NSA_KVW = NSA_KV * NSA_DK
N_SEL_PAD = 128
SB_HEADS = 16
SB_DH = D_MODEL // SB_HEADS
DIFF_HEADS = 8
DIFF_DH = D_MODEL // (2 * DIFF_HEADS)
D_FF = 2816

LANES = 128
MOD_ROWS = 8
ROW_TILE = 512
FF_CHUNK = 256
DIFF_TQ, DIFF_TK = 512, 512
SB_TQ, SB_TK = 512, 256
SB_LINEAR_FROM = 64.0
SB_DEAD = 192.0
DIFF_DEAD = 160.0
ALIBI_COLS = 7
GATE_COLS = NSA_REP * 3
NSA_TQ, NSA_TK = 256, 512
CMP_CHUNK = 128
VMEM_LIMIT = 56 * 1024 * 1024


def _cparams(sem):
    return pltpu.CompilerParams(dimension_semantics=sem, vmem_limit_bytes=VMEM_LIMIT)


def _const_spec(shape):
    n = len(shape)
    return pl.BlockSpec(shape, lambda *_: (0,) * n)


def _rms(x, g):
    return x * lax.rsqrt(jnp.mean(x * x, axis=-1, keepdims=True) + EPS) * g


def _dot(a, b):
    return jnp.dot(a, b, preferred_element_type=F32)


def _dot_nt(a, b):
    return lax.dot_general(a, b, (((1,), (1,)), ((), ())), preferred_element_type=F32)


def _mod_kernel(c_ref, w_ref, b_ref, o_ref):
    c = c_ref[...]
    cond = c * jax.nn.sigmoid(c)
    o_ref[0] = _dot(cond.astype(BF16), w_ref[0].astype(BF16)) + b_ref[0]


def _modulation(c, ada_w, ada_b):
    depth, d, n = ada_w.shape
    b = c.shape[0]
    rows = MOD_ROWS
    c_pad = jnp.zeros((rows, d), F32).at[:b].set(c)
    nc = 1152
    out = pl.pallas_call(
        _mod_kernel,
        grid=(depth, n // nc),
        in_specs=[
            pl.BlockSpec((rows, d), lambda i, j: (0, 0)),
            pl.BlockSpec((1, d, nc), lambda i, j: (i, 0, j)),
            pl.BlockSpec((1, 1, nc), lambda i, j: (i, 0, j)),
        ],
        out_specs=pl.BlockSpec((1, rows, nc), lambda i, j: (i, 0, j)),
        out_shape=jax.ShapeDtypeStruct((depth, rows, n), F32),
        compiler_params=_cparams(("arbitrary", "arbitrary")),
        name="adaln_mod",
    )(c_pad, ada_w, ada_b.reshape(depth, 1, n))
    return out


def _mod_spec(sel):
    layer, blk = sel
    return pl.BlockSpec((1, MOD_ROWS, D_MODEL), lambda b, i: (layer, 0, blk))


def _mod_row(ref):
    return ref[0, pl.ds(pl.program_id(0), 1), :]


def _inproj_kernel(*refs, tt, a_chunks, aug_blocks, b_chunks, b_aug, has_kv):
    if has_kv:
        (x_ref, g_ref, scl_ref, sh_ref, wa_ref, b0_ref, pv_ref, wb_ref, cb_ref,
         oa_ref, ob_ref, oc_ref, kv_ref) = refs
    else:
        x_ref, g_ref, scl_ref, sh_ref, wa_ref, b0_ref, pv_ref, wb_ref, cb_ref, oa_ref, ob_ref = refs
    x = x_ref[0]
    hh = (_rms(x, g_ref[...]) * (1.0 + _mod_row(scl_ref)) + _mod_row(sh_ref)).astype(BF16)
    t0 = pl.program_id(1) * tt
    pos = (t0 + lax.broadcasted_iota(jnp.int32, (tt, 1), 0)).astype(F32)
    lane = lax.broadcasted_iota(jnp.int32, (1, LANES), 1)

    def aug(k):
        return b0_ref[k:k + 1, :] + pos * pv_ref[k:k + 1, :]

    def put(dst, val):
        oa_ref[0, :, dst * LANES:(dst + 1) * LANES] = val.astype(oa_ref.dtype)

    for lo, width, kind, entries in a_chunks:
        acc = _dot(hh, wa_ref[:, lo:lo + width])
        if kind == "kv":
            for k in range(width // LANES):
                kv_ref[k] = acc[:, k * LANES:(k + 1) * LANES]
                for l in range(CMP_STRIDE):
                    oc_ref[0, l, :, k * LANES:(k + 1) * LANES] = kv_ref[
                        k, pl.ds(l, tt // CMP_STRIDE, stride=CMP_STRIDE), :].astype(oc_ref.dtype)
            continue
        for e in entries:
            blk = acc[:, e[1] * LANES:(e[1] + 1) * LANES]
            if e[0] == "mm":
                put(e[2], blk * e[3] if e[3] != 1.0 else blk)
            elif e[0] == "half":
                if e[3] != 1.0:
                    blk = blk * e[3]
                if e[4]:
                    blk = pltpu.roll(blk, NSA_DK, axis=1)
                fill = aug(e[5]) if e[5] >= 0 else jnp.where(lane == NSA_DK, 1.0, 0.0)
                put(e[2], jnp.where(lane < NSA_DK, blk, fill))
            else:
                if e[3]:
                    blk = pltpu.roll(blk, LANES - e[3], axis=1)
                put(e[2], jnp.where(lane < GATE_COLS, blk, 0.0))
    for dst, k in aug_blocks:
        put(dst, aug(k))

    tpos = t0 + lax.broadcasted_iota(jnp.int32, (1, tt), 1)
    phi = ((tpos >> 7) << 7).astype(F32)
    plo = (tpos & 127).astype(F32)
    if b_aug:
        cc = cb_ref[...]
        key_rows = (cc[:, 0:1] * phi + cc[:, 1:2] * plo + cc[:, 2:3]).astype(ob_ref.dtype)
        for dst, nrows in b_aug:
            ob_ref[0, dst:dst + nrows, :] = key_rows[0:nrows]
    for lo, rows, pieces in b_chunks:
        acc = _dot_nt(wb_ref[lo:lo + rows, :], hh)
        for src, nrows, dst in pieces:
            ob_ref[0, dst:dst + nrows, :] = acc[src:src + nrows].astype(ob_ref.dtype)


def _inproj(x, g, mod, scl, shift, wa, b0, pv, wb, cb, plan):
    bsz, t, d = x.shape
    tt = min(ROW_TILE, t)
    a_chunks, aug_blocks, n_blocks, b_chunks, b_aug, n_rows, has_kv = plan
    na = n_blocks * LANES
    in_specs = [
        pl.BlockSpec((1, tt, d), lambda b, i: (b, i, 0)),
        _const_spec((1, d)),
        _mod_spec(scl),
        _mod_spec(shift),
        _const_spec(wa.shape),
        _const_spec(b0.shape),
        _const_spec(pv.shape),
        _const_spec(wb.shape),
        _const_spec(cb.shape),
    ]
    out_specs = [pl.BlockSpec((1, tt, na), lambda b, i: (b, i, 0)),
                 pl.BlockSpec((1, n_rows, tt), lambda b, i: (b, 0, i))]
    out_shape = [jax.ShapeDtypeStruct((bsz, t, na), BF16), jax.ShapeDtypeStruct((bsz, n_rows, t), BF16)]
    scratch = []
    if has_kv:
        kvw = 2 * NSA_KVW
        out_specs.append(pl.BlockSpec((1, CMP_STRIDE, tt // CMP_STRIDE, kvw), lambda b, i: (b, 0, i, 0)))
        out_shape.append(jax.ShapeDtypeStruct((bsz, CMP_STRIDE, t // CMP_STRIDE, kvw), BF16))
        scratch.append(pltpu.VMEM((kvw // LANES, tt, LANES), F32))
    return pl.pallas_call(
        functools.partial(_inproj_kernel, tt=tt, a_chunks=a_chunks, aug_blocks=aug_blocks, b_chunks=b_chunks,
                          b_aug=b_aug, has_kv=has_kv),
        grid=(bsz, t // tt),
        in_specs=in_specs,
        out_specs=out_specs,
        out_shape=out_shape,
        scratch_shapes=scratch,
        compiler_params=_cparams(("arbitrary", "arbitrary")),
        name="mixer_inproj",
    )(x, g.reshape(1, d), mod, mod, wa, b0, pv, wb, cb)


def _ffn_kernel(*refs, after_mixer):
    if after_mixer:
        mo_ref, wo_ref, mgate_ref, gm_ref = refs[:4]
        refs = refs[4:]
    x_ref, g1_ref, scl_ref, sh_ref, gate_ref, g2_ref, w1_ref, w2_ref, o_ref, acc_ref = refs
    x = x_ref[0]
    if after_mixer:
        x = x + _mod_row(mgate_ref) * _rms(_dot(mo_ref[0], wo_ref[...]), gm_ref[...])
    hh = (_rms(x, g1_ref[...]) * (1.0 + _mod_row(scl_ref)) + _mod_row(sh_ref)).astype(BF16)
    dff = w2_ref.shape[0]
    for c in range(dff // FF_CHUNK):
        lo = c * FF_CHUNK
        gt = _dot(hh, w1_ref[:, lo:lo + FF_CHUNK])
        up = _dot(hh, w1_ref[:, dff + lo:dff + lo + FF_CHUNK])
        act = (gt * jax.nn.sigmoid(gt) * up).astype(BF16)
        part = _dot(act, w2_ref[lo:lo + FF_CHUNK, :])
        if c == 0:
            acc_ref[...] = part
        else:
            acc_ref[...] += part
    o_ref[0] = x + 0.5 * _mod_row(gate_ref) * _rms(acc_ref[...], g2_ref[...])


def _ffn(x, g1, mod, scl, shift, gate, g2, w1, w2, mixer=None):
    bsz, t, d = x.shape
    tt = min(ROW_TILE, t)
    in_specs = [
        pl.BlockSpec((1, tt, d), lambda b, i: (b, i, 0)),
        _const_spec((1, d)), _mod_spec(scl), _mod_spec(shift), _mod_spec(gate), _const_spec((1, d)),
        _const_spec(w1.shape), _const_spec(w2.shape),
    ]
    args = [x, g1.reshape(1, d), mod, mod, mod, g2.reshape(1, d), w1, w2]
    if mixer is not None:
        mo, wo, mgate, gm = mixer
        in_specs = [pl.BlockSpec((1, tt, mo.shape[2]), lambda b, i: (b, i, 0)), _const_spec(wo.shape),
                    _mod_spec(mgate), _const_spec((1, d))] + in_specs
        args = [mo, wo, mod, gm.reshape(1, d)] + args
    return pl.pallas_call(
        functools.partial(_ffn_kernel, after_mixer=mixer is not None),
        grid=(bsz, t // tt),
        in_specs=in_specs,
        out_specs=pl.BlockSpec((1, tt, d), lambda b, i: (b, i, 0)),
        out_shape=jax.ShapeDtypeStruct((bsz, t, d), F32),
        scratch_shapes=[pltpu.VMEM((tt, d), F32)],
        compiler_params=_cparams(("arbitrary", "arbitrary")),
        name="ffn",
    )(*args)


def _online_step(s, m, acc, v):
    m_new = jnp.maximum(m, jnp.max(s, axis=-1, keepdims=True))
    alpha = jnp.exp2(m - m_new)
    return m_new, alpha * acc + _dot(jnp.exp2(s - m_new).astype(BF16), v)


def _softmax_init(rows, width):
    return jnp.full((rows, 1), NEG, F32), jnp.zeros((rows, width), F32)


def _online_step_sum(s, m, l, acc, v):
    m_new = jnp.maximum(m, jnp.max(s, axis=-1, keepdims=True))
    alpha = jnp.exp2(m - m_new)
    p = jnp.exp2(s - m_new)
    return m_new, alpha * l + jnp.sum(p, axis=-1, keepdims=True), alpha * acc + _dot(p.astype(BF16), v)


def _gelu_tanh(x):
    return 0.5 * x * (1.0 + jnp.tanh(math.sqrt(2.0 / math.pi) * (x + 0.044715 * (x * x * x))))


def _cmp_kernel(kv_ref, pe_ref, w1_ref, w2kt_ref, w2v_ref, kt_ref, v_ref):
    nc = kv_ref.shape[2]

    def hidden(j, g):
        lo = j * NSA_KVW + g * NSA_DK
        top = bot = None
        for l in range(CMP_STRIDE):
            x = kv_ref[0, l, :, lo:lo + NSA_DK].astype(F32)
            hi = CMP_STRIDE + l
            a = _dot((x + pe_ref[j, l:l + 1, :]).astype(BF16), w1_ref[j, l * NSA_DK:(l + 1) * NSA_DK, :])
            b = _dot((x + pe_ref[j, hi:hi + 1, :]).astype(BF16), w1_ref[j, hi * NSA_DK:(hi + 1) * NSA_DK, :])
            top = a if top is None else top + a
            bot = b if bot is None else bot + b
        return _gelu_tanh(top + pltpu.roll(bot, nc - 1, axis=0)).astype(BF16)

    n = lax.broadcasted_iota(jnp.int32, (1, nc), 1)
    end = n * CMP_STRIDE + (CMP_LEN - 1)
    phi = ((end >> 7) << 7).astype(F32)
    plo = (end & 127).astype(F32)
    r = lax.broadcasted_iota(jnp.int32, (LANES - NSA_DK, 1), 0)
    aug = jnp.where(r < 3, phi, jnp.where(r < 6, plo, jnp.where(r == 6, 1.0, 0.0))).astype(BF16)
    for g in range(NSA_KV):
        kt_ref[0, g, 0:NSA_DK, :] = _dot_nt(w2kt_ref[...], hidden(0, g)).astype(BF16)
        kt_ref[0, g, NSA_DK:LANES, :] = aug
        v_ref[0, g] = _dot(hidden(1, g), w2v_ref[...]).astype(BF16)


def _nsa_compress(kv, pe, w1, w2kt, w2v):
    bsz, _, nc, kvw = kv.shape
    return pl.pallas_call(
        _cmp_kernel,
        grid=(bsz,),
        in_specs=[pl.BlockSpec((1, CMP_STRIDE, nc, kvw), lambda b: (b, 0, 0, 0)), _const_spec(pe.shape),
                  _const_spec(w1.shape), _const_spec(w2kt.shape), _const_spec(w2v.shape)],
        out_specs=[pl.BlockSpec((1, NSA_KV, LANES, nc), lambda b: (b, 0, 0, 0)),
                   pl.BlockSpec((1, NSA_KV, nc, LANES), lambda b: (b, 0, 0, 0))],
        out_shape=[jax.ShapeDtypeStruct((bsz, NSA_KV, LANES, nc), BF16),
                   jax.ShapeDtypeStruct((bsz, NSA_KV, nc, LANES), BF16)],
        compiler_params=_cparams(("arbitrary",)),
        name="nsa_compress",
    )(kv, pe, w1, w2kt, w2v)


def _normalized(acc):
    lane = lax.broadcasted_iota(jnp.int32, (1, LANES), 1)
    return jnp.where(lane < NSA_DK, acc / acc[:, NSA_DK:NSA_DK + 1], 0.0)


def _split3(x):
    hi = x.astype(BF16)
    r1 = x - hi.astype(F32)
    mid = r1.astype(BF16)
    lo = (r1 - mid.astype(F32)).astype(BF16)
    return hi, mid, lo


def _nsa_attn_kernel(qa_ref, kct_ref, vc_ref, kst_ref, vs_ref, kwt_ref, vw_ref, gt_ref, e_ref, ov_ref,
                     o_ref, tiles_ref, *, tq, tk):
    i = pl.program_id(2)
    q0 = i * tq
    rows = NSA_REP * tq
    qa = qa_ref[0]
    qs = jnp.concatenate([qa[:, r * LANES:(r + 1) * LANES] for r in range(NSA_REP)], axis=0)
    t_row = q0 + (lax.broadcasted_iota(jnp.int32, (rows, 1), 0) & (tq - 1))

    nc = kct_ref.shape[3]

    def compressed(width):
        s = _dot(qs, kct_ref[0, 0, :, 0:width])
        cend = lax.broadcasted_iota(jnp.int32, (1, width), 1) * CMP_STRIDE + (CMP_LEN - 1)
        valid = cend <= t_row
        s = jnp.where(valid, s, NEG)
        e = jnp.where(valid, jnp.exp2(s - jnp.max(s, axis=-1, keepdims=True)), 0.0)
        l = jnp.sum(e, axis=-1, keepdims=True)
        p = e / jnp.where(l > 0.0, l, 1.0)
        o = _dot(p.astype(BF16), vc_ref[0, 0, 0:width, :])
        p4 = p[0:tq]
        for r in range(1, NSA_REP):
            p4 = p4 + p[r * tq:(r + 1) * tq]
        ov = ov_ref[:, 0:width]
        return o, sum(_dot_nt(ov, part) for part in _split3(p4))

    chunk = min(CMP_CHUNK, nc)
    widths = list(range(chunk, nc + 1, chunk))
    n_ended = jnp.maximum(q0 + tq - CMP_LEN, 0) // CMP_STRIDE + 1
    o_cmp, imp = lax.switch((n_ended - 1) // chunk, [functools.partial(compressed, w) for w in widths])

    wk = WIN + tq
    st = pl.multiple_of(jnp.maximum(q0 - WIN, 0), LANES)
    col = st + lax.broadcasted_iota(jnp.int32, (1, wk), 1)
    valid = (t_row - col).astype(jnp.uint32) < WIN
    s = jnp.where(valid, _dot(qs, kwt_ref[0, :, pl.ds(st, wk)]), NEG)
    e = jnp.exp2(s - jnp.max(s, axis=-1, keepdims=True))
    o_win = _normalized(_dot(e.astype(BF16), vw_ref[0, pl.ds(st, wk), :]))

    cur = (q0 + lax.broadcasted_iota(jnp.int32, (1, tq), 1)) >> 6
    sid = lax.broadcasted_iota(jnp.int32, (N_SEL_PAD, 1), 0)
    forced = (sid == 0) | (sid == cur) | (sid == cur - 1)
    work = jnp.where(forced | (sid > cur), -3.0e38, imp)
    selb = jnp.where(forced, 0.0, NEG)
    for _ in range(SEL_TOPK - 3):
        mx = jnp.max(work, axis=0, keepdims=True)
        first = jnp.min(jnp.where(work == mx, sid, N_SEL_PAD), axis=0, keepdims=True)
        pick = sid == first
        selb = jnp.where(pick, 0.0, selb)
        work = jnp.where(pick, -3.0e38, work)
    selb = selb.T

    jd = q0 // tk
    n_tiles = kst_ref.shape[2] // tk
    any_sel = jnp.max(jnp.where(selb == 0.0, 1.0, 0.0), axis=0, keepdims=True)
    blk_tile = (lax.broadcasted_iota(jnp.int32, (N_SEL_PAD, LANES), 0) // (tk // SEL_LEN)
                == lax.broadcasted_iota(jnp.int32, (N_SEL_PAD, LANES), 1)).astype(BF16)
    tile_cnt = _dot(jnp.broadcast_to(any_sel, (16, N_SEL_PAD)).astype(BF16), blk_tile)
    n_act = jnp.int32(0)
    for j in range(n_tiles):
        tiles_ref[n_act] = j
        n_act = n_act + ((tile_cnt[0, j] > 0.0) & (j < jd)).astype(jnp.int32)

    selb4 = jnp.concatenate([selb.astype(BF16)] * NSA_REP, axis=0)
    qsel = jnp.concatenate([qs, selb4], axis=1)

    def sel_scores(st, width):
        return _dot(qsel, jnp.concatenate([kst_ref[0, :, pl.ds(st, width)], e_ref[:, pl.ds(st, width)]], axis=0))

    def sel_body(it, carry):
        st = pl.multiple_of(tiles_ref[it] * tk, tk)
        return _online_step(sel_scores(st, tk), *carry, vs_ref[0, pl.ds(st, tk), :])

    carry = lax.fori_loop(0, n_act, sel_body, _softmax_init(rows, LANES))
    st = pl.multiple_of(jd * tk, tk)

    def diagonal(width):
        col = st + lax.broadcasted_iota(jnp.int32, (1, width), 1)
        s = jnp.where(col <= t_row, sel_scores(st, width), NEG)
        return _normalized(_online_step(s, *carry, vs_ref[0, pl.ds(st, width), :])[1])

    o_sel = lax.switch((q0 - st) // tq, [functools.partial(diagonal, w) for w in range(tq, tk + 1, tq)])

    gs = jax.nn.sigmoid(gt_ref[0].astype(F32))
    outs = []
    for r in range(NSA_REP):
        sl = slice(r * tq, (r + 1) * tq)
        outs.append(gs[:, 3 * r:3 * r + 1] * o_cmp[sl] + gs[:, 3 * r + 1:3 * r + 2] * o_sel[sl]
                    + gs[:, 3 * r + 2:3 * r + 3] * o_win[sl])
    pairs = [outs[2 * a] + pltpu.roll(outs[2 * a + 1], NSA_DK, axis=1) for a in range(NSA_REP // 2)]
    o_ref[0] = jnp.concatenate(pairs, axis=1).astype(o_ref.dtype)


def _nsa_attention(oa, ob, kct, vcp, e_mat, ov_mat, t):
    bsz = oa.shape[0]
    tq, tk = NSA_TQ, NSA_TK
    qw = NSA_REP * LANES
    vs_blk = NSA_HEADS
    vw_blk = vs_blk + NSA_KV
    gt_blk = vw_blk + NSA_KV
    nc = kct.shape[3]
    return pl.pallas_call(
        functools.partial(_nsa_attn_kernel, tq=tq, tk=tk),
        grid=(bsz, NSA_KV, t // tq),
        in_specs=[
            pl.BlockSpec((1, tq, qw), lambda b, g, i: (b, i, g)),
            pl.BlockSpec((1, 1, LANES, nc), lambda b, g, i: (b, g, 0, 0)),
            pl.BlockSpec((1, 1, nc, LANES), lambda b, g, i: (b, g, 0, 0)),
            pl.BlockSpec((1, LANES, t), lambda b, g, i: (b, g, 0)),
            pl.BlockSpec((1, t, LANES), lambda b, g, i: (b, 0, vs_blk + g)),
            pl.BlockSpec((1, LANES, t), lambda b, g, i: (b, NSA_KV + g, 0)),
            pl.BlockSpec((1, t, LANES), lambda b, g, i: (b, 0, vw_blk + g)),
            pl.BlockSpec((1, tq, LANES), lambda b, g, i: (b, i, gt_blk + g)),
            _const_spec(e_mat.shape),
            _const_spec(ov_mat.shape),
        ],
        out_specs=pl.BlockSpec((1, tq, NSA_REP * NSA_DK), lambda b, g, i: (b, i, g)),
        out_shape=jax.ShapeDtypeStruct((bsz, t, NSA_QW), BF16),
        scratch_shapes=[pltpu.SMEM((t // tk,), jnp.int32)],
        compiler_params=_cparams(("arbitrary", "arbitrary", "arbitrary")),
        name="nsa_attention",
    )(oa, kct, vcp, ob, oa, ob, oa, oa, e_mat, ov_mat)


def _sb_kernel(q_ref, kt_ref, v_ref, o_ref, rem_ref, acc_ref, *, tq, tk):
    i = pl.program_id(2)
    q0 = i * tq
    q = q_ref[0]
    lane = lax.broadcasted_iota(jnp.int32, (1, LANES), 1)
    zero = jnp.zeros_like(q)
    qs = jnp.concatenate([jnp.where(lane < SB_DH, q, zero), jnp.where(lane >= SB_DH, q, zero)], axis=0)
    t_row = q0 + (lax.broadcasted_iota(jnp.int32, (2 * tq, 1), 0) & (tq - 1))
    upper = (lax.broadcasted_iota(jnp.int32, (tk, tk), 0) > lax.broadcasted_iota(jnp.int32, (tk, tk), 1)
             ).astype(BF16)
    rem_ref[...] = jnp.zeros(rem_ref.shape, F32)
    acc_ref[...] = jnp.zeros(acc_ref.shape, F32)

    def tiles(js, masked):
        sts = [pl.multiple_of(j * tk, tk) for j in js]
        zs = [_dot(qs, kt_ref[0, :, pl.ds(st, tk)]) for st in sts]
        costs, log_bs, befores = [], [], []
        for st, z in zip(sts, zs):
            cost = jnp.where(z > SB_LINEAR_FROM, z, jnp.log(1.0 + jnp.exp2(z)) * LOG2E)
            log_bs.append(z - cost)
            before = None
            if masked:
                before = (st + lax.broadcasted_iota(jnp.int32, (1, tk), 1)) < t_row
                cost = jnp.where(before, cost, 0.0)
            costs.append(cost.astype(BF16))
            befores.append(before)
        rems = [_dot(c, upper) for c in costs]
        rem_right = rem_ref[...]
        probs = []
        for log_b, c, rem, before in zip(log_bs, costs, rems, befores):
            ex = log_b - (rem + rem_right)
            if masked:
                ex = jnp.where(before, ex, NEG)
            probs.append(jnp.exp2(ex).astype(BF16))
            rem_right = rem_right + (rem[:, 0:1] + c[:, 0:1].astype(F32))
        rem_ref[...] = rem_right
        acc = acc_ref[...]
        for st, p in zip(sts, probs):
            acc = acc + _dot(p, v_ref[0, pl.ds(st, tk), :])
        acc_ref[...] = acc
        return jnp.min(rem_right)

    per_step = tq // tk
    n_full = q0 // tk
    spent = tiles([n_full + per_step - 1 - d for d in range(per_step)], True)

    def go_on(c):
        return (c[0] < n_full // per_step) & (c[1] < SB_DEAD)

    def body(c):
        it = c[0]
        return it + 1, tiles([n_full - 1 - per_step * it - d for d in range(per_step)], False)

    lax.while_loop(go_on, body, (jnp.int32(0), spent))
    acc = acc_ref[...]
    o_ref[0] = jnp.where(lane < SB_DH, acc[0:tq], acc[tq:2 * tq]).astype(o_ref.dtype)


def _sb_attention(oa, ob, t):
    bsz = oa.shape[0]
    tq, tk = SB_TQ, SB_TK
    npair = SB_HEADS // 2
    return pl.pallas_call(
        functools.partial(_sb_kernel, tq=tq, tk=tk),
        grid=(bsz, npair, t // tq),
        in_specs=[
            pl.BlockSpec((1, tq, LANES), lambda b, h, i: (b, i, h)),
            pl.BlockSpec((1, LANES, t), lambda b, h, i: (b, h, 0)),
            pl.BlockSpec((1, t, LANES), lambda b, h, i: (b, 0, npair + h)),
        ],
        out_specs=pl.BlockSpec((1, tq, LANES), lambda b, h, i: (b, i, h)),
        out_shape=jax.ShapeDtypeStruct((bsz, t, D_MODEL), BF16),
        scratch_shapes=[pltpu.VMEM((2 * tq, 1), F32), pltpu.VMEM((2 * tq, LANES), F32)],
        compiler_params=_cparams(("arbitrary", "arbitrary", "arbitrary")),
        name="sb_attention",
    )(oa, ob, oa)


def _diff_kernel(slope_ref, lam_ref, g_ref, qa_ref, kt_ref, v_ref, o_ref, kmax_ref, *, tq, tk, lam_init):
    h = pl.program_id(1)
    i = pl.program_id(2)
    q0 = i * tq
    n_tiles = kt_ref.shape[2] // tk

    @pl.when(i == 0)
    def _():
        run = jnp.float32(0.0)
        for j in range(n_tiles):
            kf = kt_ref[0, 0:LANES, j * tk:(j + 1) * tk].astype(F32)
            sq = kf * kf
            norms = jnp.sqrt(jnp.maximum(jnp.sum(sq[0:DIFF_DH], axis=0, keepdims=True),
                                         jnp.sum(sq[DIFF_DH:LANES], axis=0, keepdims=True)))
            run = jnp.maximum(run, jnp.max(norms))
            kmax_ref[j] = run

    qa = qa_ref[0]
    lane = lax.broadcasted_iota(jnp.int32, (1, 2 * LANES), 1)
    zero = jnp.zeros_like(qa)
    qs = jnp.concatenate(
        [jnp.where(((lane >= m_ * DIFF_DH) & (lane < (m_ + 1) * DIFF_DH)) | (lane >= LANES), qa, zero)
         for m_ in range(2)], axis=0)
    t_row = q0 + (lax.broadcasted_iota(jnp.int32, (2 * tq, 1), 0) & (tq - 1))

    def scores(j):
        return _dot(qs, kt_ref[0, :, pl.ds(pl.multiple_of(j * tk, tk), tk)])

    def values(j):
        return v_ref[0, pl.ds(pl.multiple_of(j * tk, tk), tk), :]

    jd = q0 // tk
    col = jd * tk + lax.broadcasted_iota(jnp.int32, (1, tk), 1)
    m0, acc0 = _softmax_init(2 * tq, LANES)
    state = _online_step_sum(jnp.where(col <= t_row, scores(jd), NEG), m0, jnp.zeros_like(m0), acc0, values(jd))
    qf = jnp.where(lane < LANES, qs, jnp.zeros_like(qs)).astype(F32)
    q_norm = jnp.sqrt(jnp.sum(jnp.max(qf * qf, axis=0, keepdims=True)))
    row_col = 2 * DIFF_DH + ALIBI_COLS - 1
    row_const = qs[:, row_col:row_col + 1].astype(F32)
    slope = slope_ref[h]

    floor = jnp.min(state[0] - row_const)

    def live(c):
        j = jnp.maximum(c[0], 0)
        bound = q_norm * kmax_ref[j] * 1.001 + slope * ((j + 1) * tk - 1).astype(F32)
        return (c[0] >= 0) & (bound - floor > -DIFF_DEAD)

    def body(c):
        return (c[0] - 1, *_online_step_sum(scores(c[0]), c[1], c[2], c[3], values(c[0])))

    _, _, l, acc = lax.while_loop(live, body, (jd - 1, *state))
    out = acc / l
    lam = lam_ref[...]
    lam_full = (jnp.exp(jnp.sum(lam[0:1] * lam[1:2], axis=-1, keepdims=True))
                - jnp.exp(jnp.sum(lam[2:3] * lam[3:4], axis=-1, keepdims=True)) + lam_init)
    o = out[0:tq] - lam_full * out[tq:2 * tq]
    o_ref[0] = (_rms(o, g_ref[...]) * (1.0 - lam_init)).astype(o_ref.dtype)


def _diff_attention(oa, ob, lam, subln_g, lam_init, t):
    bsz = oa.shape[0]
    tq, tk = DIFF_TQ, DIFF_TK
    slopes = _alibi_slopes(DIFF_HEADS) * LOG2E
    return pl.pallas_call(
        functools.partial(_diff_kernel, tq=tq, tk=tk, lam_init=lam_init),
        grid=(bsz, DIFF_HEADS, t // tq),
        in_specs=[
            pl.BlockSpec(memory_space=pltpu.SMEM),
            _const_spec(lam.shape),
            _const_spec((1, 2 * DIFF_DH)),
            pl.BlockSpec((1, tq, 2 * LANES), lambda b, h, i: (b, i, h)),
            pl.BlockSpec((1, 2 * LANES, t), lambda b, h, i: (b, h, 0)),
            pl.BlockSpec((1, t, LANES), lambda b, h, i: (b, 0, 2 * DIFF_HEADS + h)),
        ],
        out_specs=pl.BlockSpec((1, tq, LANES), lambda b, h, i: (b, i, h)),
        out_shape=jax.ShapeDtypeStruct((bsz, t, D_MODEL), BF16),
        scratch_shapes=[pltpu.SMEM((t // tk,), F32)],
        compiler_params=_cparams(("arbitrary", "arbitrary", "arbitrary")),
        name="diff_attention",
    )(slopes, lam, subln_g.reshape(1, 2 * DIFF_DH), oa, ob, oa)


def _alibi_slopes(n):
    return jnp.exp2(-8.0 * jnp.arange(1, n + 1, dtype=F32) / n)


def _alibi_query_cols(slopes, first):
    sl2 = slopes * LOG2E
    parts = jnp.stack([p.astype(F32) for p in _split3(sl2)], axis=1)
    b0 = jnp.pad(jnp.concatenate([parts, parts], axis=1), ((0, 0), (first, LANES - first - 6)))
    pv = jnp.pad(-sl2[:, None], ((0, 0), (first + 6, LANES - first - ALIBI_COLS)))
    return b0, pv


def _alibi_key_rows():
    c = np.zeros((LANES, 8), np.float32)
    c[0:3, 0] = 1.0
    c[3:6, 1] = 1.0
    c[6, 2] = 1.0
    return jnp.asarray(c)


def _nsa_weights(w_in):
    bounds = [0, NSA_QW] + [NSA_QW + (k + 1) * NSA_KVW for k in range(6)]
    wq, wkc, wvc, wks, wvs, wkw, wvw = [w_in[:, bounds[k]:bounds[k + 1]] for k in range(7)]
    wg = jnp.pad(w_in[:, bounds[7]:], ((0, 0), (0, LANES - NSA_KV * GATE_COLS)))
    wa = jnp.concatenate([wq, wkc, wvc, wvs, wvw, wg], axis=1).astype(BF16)
    scale = NSA_DK ** -0.5 * LOG2E
    per = ROW_TILE // LANES
    qblocks = NSA_QW // LANES
    a_chunks = []
    for c in range(qblocks // per):
        ents = tuple(("half", b, 2 * (c * per + b) + hi, scale, hi, 2 * (c * per + b) + hi)
                     for b in range(per) for hi in range(2))
        a_chunks.append((c * ROW_TILE, ROW_TILE, "mm", ents))
    a_chunks.append((NSA_QW, 2 * NSA_KVW, "kv", ()))
    v0 = NSA_QW + 2 * NSA_KVW
    vs_blk, vw_blk, gt_blk = NSA_HEADS, NSA_HEADS + NSA_KV, NSA_HEADS + 2 * NSA_KV
    ents = tuple(("half", g // 2, vs_blk + g, 1.0, g % 2, -1) for g in range(NSA_KV)) + tuple(
        ("half", NSA_KV // 2 + g // 2, vw_blk + g, 1.0, g % 2, -1) for g in range(NSA_KV))
    a_chunks.append((v0, 2 * NSA_KVW, "mm", ents))
    a_chunks.append((v0 + 2 * NSA_KVW, LANES, "mm",
                     tuple(("gate", 0, gt_blk + g, GATE_COLS * g) for g in range(NSA_KV))))
    b0, pv = _alibi_query_cols(_alibi_slopes(NSA_HEADS), NSA_DK)
    wb = jnp.concatenate([wks, wkw], axis=1).T.astype(BF16)
    pieces = tuple((NSA_DK * g, NSA_DK, LANES * g) for g in range(2 * NSA_KV))
    b_aug = tuple((LANES * g + NSA_DK, LANES - NSA_DK) for g in range(2 * NSA_KV))
    plan = (tuple(a_chunks), (), gt_blk + NSA_KV, ((0, 2 * NSA_KVW, pieces),), b_aug, 2 * NSA_KV * LANES, True)
    return wa, b0, pv, wb, _alibi_key_rows(), plan


def _sb_weights(w_in):
    d = w_in.shape[0]
    wq, wk, wv = jnp.split(w_in, 3, axis=1)
    wa = jnp.concatenate([wq, wv], axis=1).astype(BF16)
    per = ROW_TILE // LANES
    nq = d // ROW_TILE
    a_chunks = tuple((c * ROW_TILE, ROW_TILE, "mm",
                      tuple(("mm", b, c * per + b, SB_DH ** -0.5 * LOG2E if c < nq else 1.0) for b in range(per)))
                     for c in range(2 * nq))
    b_chunks = tuple((c * ROW_TILE, ROW_TILE, ((0, ROW_TILE, c * ROW_TILE),)) for c in range(nq))
    zero = jnp.asarray(np.zeros((1, LANES), np.float32))
    plan = (a_chunks, (), 2 * d // LANES, b_chunks, (), d, False)
    return wa, zero, zero, wk.T.astype(BF16), jnp.asarray(np.zeros((LANES, 8), np.float32)), plan


def _diff_weights(w_in):
    d = w_in.shape[0]
    wq, wk, wv = jnp.split(w_in, 3, axis=1)
    wa = jnp.concatenate([wq, wv], axis=1).astype(BF16)
    scale = DIFF_DH ** -0.5 * LOG2E
    per = ROW_TILE // LANES
    nq = d // ROW_TILE
    a_chunks = tuple(
        (c * ROW_TILE, ROW_TILE, "mm",
         tuple(("mm", b, 2 * (c * per + b), scale) if c < nq else ("mm", b, 2 * DIFF_HEADS + (c - nq) * per + b, 1.0)
               for b in range(per)))
        for c in range(2 * nq))
    aug_blocks = tuple((2 * h + 1, h) for h in range(DIFF_HEADS))
    b0, pv = _alibi_query_cols(_alibi_slopes(DIFF_HEADS), 0)
    b_chunks = tuple((c * ROW_TILE, ROW_TILE,
                      tuple((LANES * k, LANES, 2 * LANES * (c * per + k)) for k in range(per)))
                     for c in range(nq))
    b_aug = tuple((2 * LANES * h + LANES, LANES) for h in range(DIFF_HEADS))
    plan = (a_chunks, aug_blocks, 3 * DIFF_HEADS, b_chunks, b_aug, 2 * DIFF_HEADS * LANES, False)
    return wa, b0, pv, wk.T.astype(BF16), _alibi_key_rows(), plan


def _nsa_mixer(x, g, mod, scl, shift, w_in, cmp_pe, cmp_w1, cmp_w2):
    t = x.shape[1]
    wa, b0, pv, wb, cb, plan = _nsa_weights(w_in)
    oa, ob, kv = _inproj(x, g, mod, scl, shift, wa, b0, pv, wb, cb, plan)
    nc = t // CMP_STRIDE
    w2kt = cmp_w2[0].T.astype(BF16)
    w2v = jnp.pad(cmp_w2[1], ((0, 0), (0, LANES - NSA_DK))).astype(BF16)
    kct, vcp = _nsa_compress(kv, cmp_pe, cmp_w1.astype(BF16), w2kt, w2v)
    tok = np.arange(t)
    e_mat = jnp.asarray(np.arange(N_SEL_PAD)[:, None] == (tok // SEL_LEN)[None, :], BF16)
    cstart = np.arange(nc) * CMP_STRIDE
    sstart = np.arange(N_SEL_PAD) * SEL_LEN
    ov_mat = jnp.asarray((cstart[None, :] < sstart[:, None] + SEL_LEN)
                         & (cstart[None, :] + CMP_LEN > sstart[:, None])
                         & (np.arange(nc)[None, :] < nc - 1), BF16)
    return _nsa_attention(oa, ob, kct, vcp, e_mat, ov_mat, t)


def _sb_mixer(x, g, mod, scl, shift, w_in):
    t = x.shape[1]
    wa, b0, pv, wb, cb, plan = _sb_weights(w_in)
    oa, ob = _inproj(x, g, mod, scl, shift, wa, b0, pv, wb, cb, plan)
    return _sb_attention(oa, ob, t)


def _diff_mixer(x, g, mod, scl, shift, w_in, lam, subln_g, layer_idx):
    t = x.shape[1]
    wa, b0, pv, wb, cb, plan = _diff_weights(w_in)
    oa, ob = _inproj(x, g, mod, scl, shift, wa, b0, pv, wb, cb, plan)
    lam_init = 0.8 - 0.6 * math.exp(-0.3 * layer_idx)
    return _diff_attention(oa, ob, lam, subln_g, lam_init, t)


def kernel(x, c, ada_w, ada_b, norm_g, ffn_w1, ffn_w2, nsa_w_in, nsa_cmp_pe, nsa_cmp_w1, nsa_cmp_w2,
           nsa_w_out, sb_w_in, sb_w_out, diff_w_in, diff_lam, diff_subln_g, diff_w_out):
    bsz, t, d = x.shape
    assert d == D_MODEL and t % ROW_TILE == 0 and t >= WIN + 128 and t <= N_SEL_PAD * SEL_LEN
    mod = _modulation(c, ada_w, ada_b)
    for i in range(DEPTH):
        def mods(sidx):
            return tuple((i, 3 * sidx + k) for k in range(3))

        shift, scl, gate = mods(0)
        x = _ffn(x, norm_g[i, 0], mod, scl, shift, gate, norm_g[i, 1],
                 ffn_w1[i, 0].astype(BF16), ffn_w2[i, 0].astype(BF16))
        shift, scl, gate = mods(1)
        kind, j = i % N_MIXERS, i // N_MIXERS
        if kind == 0:
            o = _nsa_mixer(x, norm_g[i, 2], mod, scl, shift, nsa_w_in[j], nsa_cmp_pe[j], nsa_cmp_w1[j],
                           nsa_cmp_w2[j])
            w_out = nsa_w_out[j]
        elif kind == 1:
            o = _sb_mixer(x, norm_g[i, 2], mod, scl, shift, sb_w_in[j])
            w_out = sb_w_out[j]
        else:
            o = _diff_mixer(x, norm_g[i, 2], mod, scl, shift, diff_w_in[j], diff_lam[j], diff_subln_g[j], i)
            w_out = diff_w_out[j]
        mixer = (o, w_out.astype(BF16), gate, norm_g[i, 3])
        shift, scl, gate = mods(2)
        x = _ffn(x, norm_g[i, 4], mod, scl, shift, gate, norm_g[i, 5],
                 ffn_w1[i, 1].astype(BF16), ffn_w2[i, 1].astype(BF16), mixer=mixer)
    return x
```
